```python
import math
import jax
import jax.numpy as jnp
from jax import lax
import numpy as np

D_MODEL = 1024
BATCH = 8
SEQ = 2048
DEPTH = 1

MIX_WIDTH = D_MODEL
HGRN_WIDTH = MIX_WIDTH // 2
HGRN_HEADS = 4
HGRN_HEAD_V = HGRN_WIDTH // HGRN_HEADS
HGRN_HEAD_K = 128
HGRN_KEY = HGRN_HEADS * HGRN_HEAD_K
CONV_CH = MIX_WIDTH - HGRN_WIDTH
CONV_LEN = 31
CHUNK = 64
SUB_CHUNK = 16
N_EXPERTS = 32
TOP_K = 4
D_FF = D_MODEL
SWIGLU_LIMIT = 7.0
SWIGLU_ALPHA = 1.702
MOE_BLOCK = 256
EPS = 1e-5
IN_COLS = 3 * HGRN_KEY + 2 * HGRN_WIDTH + 2 * CONV_CH

kernel_name = "hybrid_hgrn2_conformer_moe_encoder"


def _rmsnorm(x, w):
    xf = x.astype(jnp.float32)
    y = xf * lax.rsqrt(jnp.mean(xf * xf, axis=-1, keepdims=True) + EPS)
    return (y * w).astype(x.dtype)


def _chunk_recurrence(q, k, v, logf):
    bn, nh, t_len, dk = q.shape
    dv = v.shape[-1]
    n_ch = t_len // CHUNK
    ns = CHUNK // SUB_CHUNK
    q, k, logf = [a.reshape(bn, nh, n_ch, CHUNK, dk) for a in (q, k, logf)]
    v = v.reshape(bn, nh, n_ch, CHUNK, dv)
    b = jnp.cumsum(logf, axis=3)
    b_end = b[:, :, :, -1:, :]
    k_to_end = k * jnp.exp(b_end - b)
    u = jnp.einsum('bhncd,bhnce->bhnde', k_to_end, v)
    decay = jnp.exp(b_end[:, :, :, 0, :])

    def step(state, inp):
        dec_n, u_n = inp
        return dec_n[..., None] * state + u_n, state

    s0 = jnp.zeros((bn, nh, dk, dv), jnp.float32)
    _, s_prev = lax.scan(step, s0, (jnp.moveaxis(decay, 2, 0), jnp.moveaxis(u, 2, 0)))
    o_inter = jnp.einsum('bhncd,nbhde->bhnce', q * jnp.exp(b), s_prev)
    bs = b.reshape(bn, nh, n_ch, ns, SUB_CHUNK, dk)
    qs = q.reshape(bn, nh, n_ch, ns, SUB_CHUNK, dk)
    ks = k.reshape(bn, nh, n_ch, ns, SUB_CHUNK, dk)
    e_sub = bs[:, :, :, :, -1, :]
    k_sub = ks * jnp.exp(e_sub[:, :, :, :, None, :] - bs)
    q_rel = q[:, :, :, :, None, :] * jnp.exp(
        jnp.minimum(b[:, :, :, :, None, :] - e_sub[:, :, :, None, :, :], 0.0))
    a_off = jnp.einsum('bhntjd,bhnjsd->bhntjs', q_rel, k_sub).reshape(bn, nh, n_ch, CHUNK, CHUNK)
    tri = jnp.tril(jnp.ones((SUB_CHUNK, SUB_CHUNK), bool))
    diff = bs[..., :, None, :] - bs[..., None, :, :]
    dec = jnp.exp(jnp.where(tri[:, :, None], diff, -jnp.inf))
    a_diag = jnp.einsum('bhnitd,bhnisd,bhnitsd->bhnits', qs, ks, dec)
    a_diag_full = jnp.einsum('bhnits,ij->bhnitjs', a_diag,
                             jnp.eye(ns, dtype=a_diag.dtype)).reshape(bn, nh, n_ch, CHUNK, CHUNK)
    sub_id = jnp.arange(CHUNK) // SUB_CHUNK
    off_mask = sub_id[:, None] > sub_id[None, :]
    a = jnp.where(off_mask, a_off, 0.0) + a_diag_full
    o_intra = jnp.einsum('bhnts,bhnse->bhnte', a, v)
    return (o_inter + o_intra).reshape(bn, nh, t_len, dv)


def _hgrn2_mixer(qz, fz_f, fz_b, iv, gz, lb_f, lb_b, norm_w):
    bn, t_len, _ = qz.shape

    def heads(a, d):
        return a.reshape(bn, t_len, HGRN_HEADS, d).transpose(0, 2, 1, 3).astype(jnp.float32)

    q = jax.nn.silu(heads(qz, HGRN_HEAD_K))
    v = heads(iv, HGRN_HEAD_V)

    def gate(fz, lb):
        lb = lb.astype(jnp.float32).reshape(HGRN_HEADS, 1, HGRN_HEAD_K)
        f = lb + (1.0 - lb) * jax.nn.sigmoid(heads(fz, HGRN_HEAD_K))
        return jnp.log(f), 1.0 - f

    logf_f, k_f = gate(fz_f, lb_f)
    logf_b, k_b = gate(fz_b, lb_b)
    o_f = _chunk_recurrence(q, k_f, v, logf_f)
    flip = lambda a: jnp.flip(a, axis=2)
    o_b = flip(_chunk_recurrence(flip(q), flip(k_b), flip(v), flip(logf_b)))
    o = o_f + o_b
    o = o * lax.rsqrt(jnp.mean(o * o, axis=-1, keepdims=True) + EPS)
    o = o.transpose(0, 2, 1, 3).reshape(bn, t_len, HGRN_WIDTH) * norm_w
    return (o * jax.nn.silu(gz.astype(jnp.float32))).astype(qz.dtype)


def _conformer_conv(cv, cg, dw_w, dw_b, ln_w, ln_b):
    u = cv * jax.nn.sigmoid(cg)
    pad = (CONV_LEN - 1) // 2
    u = lax.conv_general_dilated(
        u, dw_w.reshape(CONV_LEN, 1, CONV_CH).astype(u.dtype), window_strides=(1,),
        padding=[(pad, pad)], dimension_numbers=('NWC', 'WIO', 'NWC'),
        feature_group_count=CONV_CH) + dw_b
    uf = u.astype(jnp.float32)
    mu = jnp.mean(uf, axis=-1, keepdims=True)
    var = jnp.mean(jnp.square(uf - mu), axis=-1, keepdims=True)
    uf = (uf - mu) * lax.rsqrt(var + EPS) * ln_w + ln_b
    return jax.nn.silu(uf).astype(cv.dtype)


def _moe(xf, router_w, router_b, w1, b1, w2, b2):
    t_tok, d = xf.shape
    tk = t_tok * TOP_K
    logits = (xf @ router_w + router_b).astype(jnp.float32)
    top_val, top_idx = lax.top_k(logits, TOP_K)
    gates = jax.nn.softmax(top_val, axis=-1)
    flat_e = top_idx.reshape(-1).astype(jnp.int32)
    flat_tok = jnp.repeat(jnp.arange(t_tok, dtype=jnp.int32), TOP_K)
    order = jnp.argsort(flat_e)
    sorted_e = flat_e[order]
    counts = jnp.bincount(flat_e, length=N_EXPERTS)
    padded = ((counts + MOE_BLOCK - 1) // MOE_BLOCK) * MOE_BLOCK
    pad_end = jnp.cumsum(padded)
    pad_start = pad_end - padded
    grp_start = jnp.cumsum(counts) - counts
    rank = jnp.arange(tk, dtype=jnp.int32) - grp_start[sorted_e]
    dest = pad_start[sorted_e] + rank
    n_blocks = -(-tk // MOE_BLOCK) + N_EXPERTS
    rows = n_blocks * MOE_BLOCK
    src_tok = jnp.zeros((rows,), jnp.int32).at[dest].set(flat_tok[order])
    row_gate = jnp.zeros((rows,), jnp.float32).at[dest].set(gates.reshape(-1)[order])
    block_start = jnp.arange(n_blocks, dtype=pad_end.dtype) * MOE_BLOCK
    block_expert = jnp.minimum(jnp.searchsorted(pad_end, block_start, side='right'),
                               N_EXPERTS - 1).astype(jnp.int32)
    xs = xf[src_tok].reshape(n_blocks, MOE_BLOCK, d)

    def expert_block(args):
        xb, e = args
        hdn = xb @ w1[e] + b1[e]
        h_glu = jnp.minimum(hdn[:, :D_FF], SWIGLU_LIMIT)
        h_lin = jnp.clip(hdn[:, D_FF:], -SWIGLU_LIMIT, SWIGLU_LIMIT)
        act = h_glu * jax.nn.sigmoid(SWIGLU_ALPHA * h_glu) * (h_lin + 1.0)
        return act @ w2[e] + b2[e]

    ys = lax.map(expert_block, (xs, block_expert)).reshape(rows, d)
    ys = (ys.astype(jnp.float32) * row_gate[:, None]).astype(xf.dtype)
    return jnp.zeros((t_tok, d), xf.dtype).at[src_tok].add(ys)


def setup_inputs(seed: int = 0) -> dict:
    key = jax.random.key(seed)
    ks = jax.random.split(key, 20)
    nrm = lambda k, s: jax.random.normal(k, s, jnp.float32)
    return {
        "x": nrm(ks[0], (BATCH, SEQ, D_MODEL)),
        "norm1_w": 1.0 + 0.02 * nrm(ks[1], (DEPTH, D_MODEL)),
        "w_in": nrm(ks[2], (DEPTH, D_MODEL, IN_COLS)) * D_MODEL ** -0.5,
        "lb_logits": 0.1 * nrm(ks[3], (2, DEPTH + 1, HGRN_KEY)),
        "hgrn_norm_w": 1.0 + 0.02 * nrm(ks[4], (DEPTH, HGRN_WIDTH)),
        "dw_w": nrm(ks[5], (DEPTH, CONV_LEN, CONV_CH)) * CONV_LEN ** -0.5,
        "dw_b": 0.01 * nrm(ks[6], (DEPTH, CONV_CH)),
        "conv_ln_w": 1.0 + 0.02 * nrm(ks[7], (DEPTH, CONV_CH)),
        "conv_ln_b": 0.01 * nrm(ks[8], (DEPTH, CONV_CH)),
        "w_out": nrm(ks[9], (DEPTH, MIX_WIDTH, D_MODEL)) * MIX_WIDTH ** -0.5,
        "norm2_w": 1.0 + 0.02 * nrm(ks[10], (DEPTH, D_MODEL)),
        "router_w": nrm(ks[11], (DEPTH, D_MODEL, N_EXPERTS)) * D_MODEL ** -0.5,
        "router_b": 0.01 * nrm(ks[12], (DEPTH, N_EXPERTS)),
        "w1": nrm(ks[13], (DEPTH, N_EXPERTS, D_MODEL, 2 * D_FF)) * D_MODEL ** -0.5,
        "b1": 0.01 * nrm(ks[14], (DEPTH, N_EXPERTS, 2 * D_FF)),
        "w2": nrm(ks[15], (DEPTH, N_EXPERTS, D_FF, D_MODEL)) * D_FF ** -0.5,
        "b2": 0.01 * nrm(ks[16], (DEPTH, N_EXPERTS, D_MODEL)),
        "final_norm_w": 1.0 + 0.02 * nrm(ks[17], (D_MODEL,)),
    }


def reference(x, norm1_w, w_in, lb_logits, hgrn_norm_w, dw_w, dw_b, conv_ln_w, conv_ln_b,
              w_out, norm2_w, router_w, router_b, w1, b1, w2, b2, final_norm_w):
    bn, t_len, d = x.shape
    lb_table = jnp.cumsum(jax.nn.softmax(lb_logits.astype(jnp.float32), axis=1), axis=1)
    splits = list(np.cumsum([HGRN_KEY, HGRN_KEY, HGRN_KEY, HGRN_WIDTH, HGRN_WIDTH, CONV_CH]))
    h = x
    for l in range(DEPTH):
        n = _rmsnorm(h, norm1_w[l])
        proj = n @ w_in[l]
        qz, fz_f, fz_b, iv, gz, cv, cg = jnp.split(proj, splits, axis=-1)
        a_out = _hgrn2_mixer(qz, fz_f, fz_b, iv, gz, lb_table[0, l], lb_table[1, l], hgrn_norm_w[l])
        b_out = _conformer_conv(cv, cg, dw_w[l], dw_b[l], conv_ln_w[l], conv_ln_b[l])
        h = h + jnp.concatenate([a_out, b_out], axis=-1) @ w_out[l]
        n2 = _rmsnorm(h, norm2_w[l])
        h = h + _moe(n2.reshape(bn * t_len, d), router_w[l], router_b[l],
                     w1[l], b1[l], w2[l], b2[l]).reshape(bn, t_len, d)
    return _rmsnorm(h, final_norm_w)
```

```python
import functools
import math

import numpy as np
import jax
import jax.numpy as jnp
from jax import lax
from jax.experimental import pallas as pl
from jax.experimental.pallas import tpu as pltpu

F32 = jnp.float32
BF16 = jnp.bfloat16
I32 = jnp.int32

EPS = 1e-5
HGRN_HEADS = 4
HEAD_DIM = 128
HGRN_CHUNK = 128
CONV_LEN = 31
N_EXPERTS = 32
TOP_K = 4
SWIGLU_LIMIT = 7.0
SWIGLU_ALPHA = 1.702
MOE_BLOCK = 256
LANES = 128
SUBLANES = 8
VMEM_LIMIT = 56 << 20

_NT = (((1,), (1,)), ((), ()))


def _sigmoid(x):
    return 1.0 / (1.0 + jnp.exp(-x))


def _silu(x):
    return x * _sigmoid(x)


def _cparams(sem):
    return pltpu.CompilerParams(dimension_semantics=sem, vmem_limit_bytes=VMEM_LIMIT)


def _inproj_body(x_ref, nw_ref, w_ref, o_ref):
    x = x_ref[...]
    n = x * lax.rsqrt(jnp.mean(x * x, axis=-1, keepdims=True) + EPS) * nw_ref[...]
    o_ref[...] = jnp.dot(n.astype(BF16), w_ref[...], preferred_element_type=F32)


def _in_proj(x2, norm_w, w_bf, tm=256):
    n_tok, d = x2.shape
    cols = w_bf.shape[1]
    return pl.pallas_call(
        _inproj_body,
        grid=(n_tok // tm,),
        in_specs=[pl.BlockSpec((tm, d), lambda i: (i, 0)),
                  pl.BlockSpec((1, d), lambda i: (0, 0)),
                  pl.BlockSpec((d, cols), lambda i: (0, 0))],
        out_specs=pl.BlockSpec((tm, cols), lambda i: (i, 0)),
        out_shape=jax.ShapeDtypeStruct((n_tok, cols), F32),
        compiler_params=_cparams(("parallel",)),
        name="in_proj",
    )(x2, norm_w.reshape(1, d), w_bf)


def _hgrn_levels(c):
    return [c >> (i + 1) for i in range(int(math.log2(c)))]


def _hgrn_sum_matrices(c, fwd):
    r = np.arange(c)
    mats = []
    for m in _hgrn_levels(c):
        mat = np.zeros((c, c), np.float32)
        for t in range(c):
            p0 = t & ~(2 * m - 1)
            upper = (t & m) != 0
            if fwd:
                if upper:
                    mat[t, p0 + m:t + 1] = 1.0
                else:
                    mat[t, t + 1:p0 + m] = 1.0
            else:
                if upper:
                    mat[t, p0 + m:t] = 1.0
                else:
                    mat[t, t:p0 + m] = 1.0
        mats.append(mat)
    if fwd:
        mats.append((r[None, :] <= r[:, None]).astype(np.float32))
        mats.append((r[None, :] > r[:, None]).astype(np.float32))
    else:
        mats.append((r[None, :] >= r[:, None]).astype(np.float32))
        mats.append((r[None, :] < r[:, None]).astype(np.float32))
    return np.concatenate(mats, axis=0)


def _hgrn_chunk(q, k, lf, v_bf, vt_bf, st, sums_ref, fwd, c):
    levels = _hgrn_levels(c)
    nlev = len(levels)
    dk = q.shape[1]
    hi = lf.astype(BF16)
    lo = (lf - hi.astype(F32)).astype(BF16)
    both = jnp.dot(sums_ref[...], jnp.concatenate([hi, lo], axis=1), preferred_element_type=F32)
    e_all = jnp.exp(both[:, :dk] + both[:, dk:])
    row = lax.broadcasted_iota(I32, (c, dk), 0)
    sep = lax.broadcasted_iota(I32, (c, c), 0) ^ lax.broadcasted_iota(I32, (c, c), 1)
    a = lax.dot_general(q.astype(BF16), k.astype(BF16), _NT, preferred_element_type=F32)
    for l, m in enumerate(levels):
        e_l = e_all[l * c:(l + 1) * c]
        upper = (row & m) != 0
        is_q = upper if fwd else jnp.logical_not(upper)
        q_rel = jnp.where(is_q, q * e_l, 0.0).astype(BF16)
        k_rel = jnp.where(is_q, 0.0, k * e_l).astype(BF16)
        a_l = lax.dot_general(q_rel, k_rel, _NT, preferred_element_type=F32)
        a = jnp.where((sep >= m) & (sep < 2 * m), a_l, a)
    q_in = (q * e_all[nlev * c:(nlev + 1) * c]).astype(BF16)
    k_out = (k * e_all[(nlev + 1) * c:]).astype(BF16)
    o = jnp.dot(a.astype(BF16), v_bf, preferred_element_type=F32)
    o = o + lax.dot_general(q_in, st.astype(BF16), _NT, preferred_element_type=F32)
    edge = (nlev + 1) * c - 1 if fwd else nlev * c
    st_new = st * e_all[edge:edge + 1] + jnp.dot(vt_bf, k_out, preferred_element_type=F32)
    return o, st_new


def _hgrn_body(qz_ref, ff_ref, fb_ref, iv_ref, gz_ref, lbf_ref, lbb_ref, nw_ref, sf_ref, sb_ref, o_ref,
               q_s, kf_s, lff_s, kb_s, lfb_s, v_s, vt_s, acc_s, *, c):
    t_len, dk = qz_ref.shape
    nch = t_len // c
    q_s[...] = _silu(qz_ref[...])

    def gate(fz_ref, lb_ref, k_s, lf_s):
        lb = lb_ref[0]
        f = lb + (1.0 - lb) * _sigmoid(fz_ref[...])
        lf_s[...] = jnp.log(f)
        k_s[...] = 1.0 - f

    gate(ff_ref, lbf_ref, kf_s, lff_s)
    gate(fb_ref, lbb_ref, kb_s, lfb_s)
    v = iv_ref[...]
    v_s[...] = v.astype(BF16)
    for ci in range(nch):
        vt_s[ci] = v[ci * c:(ci + 1) * c, :].T.astype(BF16)

    def run(k_s, lf_s, sums_ref, fwd):
        def step(i, st):
            ci = i if fwd else nch - 1 - i
            rows = pl.ds(pl.multiple_of(ci * c, c), c)
            o, st = _hgrn_chunk(q_s[rows, :], k_s[rows, :], lf_s[rows, :], v_s[rows, :], vt_s[ci],
                                st, sums_ref, fwd, c)
            if fwd:
                acc_s[rows, :] = o
            else:
                acc_s[rows, :] += o
            return st
        lax.fori_loop(0, nch, step, jnp.zeros((dk, dk), F32))

    run(kf_s, lff_s, sf_ref, True)
    run(kb_s, lfb_s, sb_ref, False)
    o = acc_s[...]
    o = o * lax.rsqrt(jnp.mean(o * o, axis=-1, keepdims=True) + EPS) * nw_ref[0]
    o_ref[...] = (o * _silu(gz_ref[...])).astype(o_ref.dtype)


def _hgrn(proj, lb_f, lb_b, norm_w, batch, t_len):
    h, dk, c = HGRN_HEADS, HEAD_DIM, HGRN_CHUNK
    nrows = len(_hgrn_levels(c)) + 2
    sums_f = jnp.asarray(_hgrn_sum_matrices(c, True), BF16)
    sums_b = jnp.asarray(_hgrn_sum_matrices(c, False), BF16)

    def col(group):
        return pl.BlockSpec((t_len, dk), lambda b, hh: (b, group * h + hh))

    def per_head():
        return pl.BlockSpec((1, 1, dk), lambda b, hh: (hh, 0, 0))

    const = pl.BlockSpec((nrows * c, c), lambda b, hh: (0, 0))
    seq = lambda dt: pltpu.VMEM((t_len, dk), dt)
    return pl.pallas_call(
        functools.partial(_hgrn_body, c=c),
        grid=(batch, h),
        in_specs=[col(0), col(1), col(2), col(3), col(4), per_head(), per_head(), per_head(), const, const],
        out_specs=pl.BlockSpec((t_len, dk), lambda b, hh: (b, hh)),
        out_shape=jax.ShapeDtypeStruct((batch * t_len, h * dk), BF16),
        scratch_shapes=[seq(F32), seq(F32), seq(F32), seq(F32), seq(F32), seq(BF16),
                        pltpu.VMEM((t_len // c, dk, c), BF16), seq(F32)],
        compiler_params=_cparams(("parallel", "parallel")),
        name="hgrn",
    )(proj, proj, proj, proj, proj, lb_f.reshape(h, 1, dk), lb_b.reshape(h, 1, dk),
      norm_w.reshape(h, 1, dk), sums_f, sums_b)


CONV_ROWS = 64
CONV_HALO = 16


def _conv_body(cv_ref, cg_ref, w_ref, b_ref, lnw_ref, lnb_ref, o_ref, u_s):
    t_len, ch = cv_ref.shape
    halo, rows = CONV_HALO, CONV_ROWS
    shift0 = halo - (CONV_LEN - 1) // 2
    win = rows + 2 * halo
    u_s[0:halo, :] = jnp.zeros((halo, ch), F32)
    u_s[halo + t_len:, :] = jnp.zeros((halo, ch), F32)
    u_s[halo:halo + t_len, :] = cv_ref[...] * _sigmoid(cg_ref[...])

    def step(i, carry):
        t0 = pl.multiple_of(i * rows, rows)
        groups = []
        for g in range(ch // LANES):
            lanes = slice(g * LANES, (g + 1) * LANES)
            window = u_s[pl.ds(t0, win), lanes]
            acc = jnp.zeros((rows, LANES), F32)
            for s in range(SUBLANES):
                taps = [j for j in range(CONV_LEN) if (j + shift0) % SUBLANES == s]
                if not taps:
                    continue
                span = max(j + shift0 for j in taps) - s + rows
                shifted = window[s:s + span]
                for j in taps:
                    off = j + shift0 - s
                    acc = acc + w_ref[j:j + 1, lanes] * shifted[off:off + rows]
            groups.append(acc + b_ref[:, lanes])
        y = jnp.concatenate(groups, axis=1)
        mu = jnp.mean(y, axis=-1, keepdims=True)
        yc = y - mu
        var = jnp.mean(yc * yc, axis=-1, keepdims=True)
        z = yc * lax.rsqrt(var + EPS) * lnw_ref[...] + lnb_ref[...]
        o_ref[pl.ds(t0, rows), :] = _silu(z).astype(o_ref.dtype)
        return carry

    lax.fori_loop(0, t_len // rows, step, 0)


def _conv(proj, dw_w, dw_b, ln_w, ln_b, batch, t_len, first_col_block):
    ch = dw_w.shape[1]
    vec = pl.BlockSpec((1, ch), lambda b: (0, 0))
    return pl.pallas_call(
        _conv_body,
        grid=(batch,),
        in_specs=[pl.BlockSpec((t_len, ch), lambda b: (b, first_col_block)),
                  pl.BlockSpec((t_len, ch), lambda b: (b, first_col_block + 1)),
                  pl.BlockSpec((CONV_LEN, ch), lambda b: (0, 0)), vec, vec, vec],
        out_specs=pl.BlockSpec((t_len, ch), lambda b: (b, 0)),
        out_shape=jax.ShapeDtypeStruct((batch * t_len, ch), BF16),
        scratch_shapes=[pltpu.VMEM((t_len + 2 * CONV_HALO, ch), F32)],
        compiler_params=_cparams(("parallel",)),
        name="conv",
    )(proj, proj, dw_w, dw_b.reshape(1, ch), ln_w.reshape(1, ch), ln_b.reshape(1, ch))


def _mix_body(a_ref, b_ref, x_ref, wa_ref, wb_ref, nw_ref, wrh_ref, wrl_ref, rb_ref, tri_ref,
              h_ref, n2_ref, eid_ref, rank_ref, gate_ref, cnt_ref):
    tm = x_ref.shape[0]
    ne = wrh_ref.shape[0]

    @pl.when(pl.program_id(0) == 0)
    def _():
        cnt_ref[...] = jnp.zeros(cnt_ref.shape, F32)

    h = x_ref[...] + jnp.dot(a_ref[...], wa_ref[...], preferred_element_type=F32) \
        + jnp.dot(b_ref[...], wb_ref[...], preferred_element_type=F32)
    h_ref[...] = h
    n2 = h * lax.rsqrt(jnp.mean(h * h, axis=-1, keepdims=True) + EPS) * nw_ref[...]
    for j in range(n2.shape[1] // LANES):
        n2_ref[:, j, :] = n2[:, j * LANES:(j + 1) * LANES]
    hi = n2.astype(BF16)
    lo = (n2 - hi.astype(F32)).astype(BF16)
    logits = (lax.dot_general(wrh_ref[...], hi, _NT, preferred_element_type=F32)
              + lax.dot_general(wrh_ref[...], lo, _NT, preferred_element_type=F32)
              + lax.dot_general(wrl_ref[...], hi, _NT, preferred_element_type=F32)
              + rb_ref[...])
    e_iota = lax.broadcasted_iota(I32, (ne, tm), 0)
    work = logits
    sels, vals, ids = [], [], []
    for _ in range(TOP_K):
        mx = jnp.max(work, axis=0, keepdims=True)
        idx = jnp.min(jnp.where(work == mx, e_iota, ne), axis=0, keepdims=True)
        sel = e_iota == idx
        work = jnp.where(sel, -jnp.inf, work)
        sels.append(sel)
        vals.append(mx)
        ids.append(idx)
    exps = [jnp.exp(v - vals[0]) for v in vals]
    denom = exps[0] + exps[1] + exps[2] + exps[3]
    chosen = jnp.zeros((ne, tm), F32)
    for sel in sels:
        chosen = jnp.where(sel, 1.0, chosen)
    chosen_bf = chosen.astype(BF16)
    before = cnt_ref[...]
    prior = jnp.dot(chosen_bf, tri_ref[...], preferred_element_type=F32) \
        + jnp.concatenate([before] * (tm // LANES), axis=1)
    cnt_ref[...] = before + jnp.dot(chosen_bf, jnp.ones((tm, LANES), BF16), preferred_element_type=F32)
    for k in range(TOP_K):
        eid_ref[k:k + 1, :] = ids[k]
        rank_ref[k:k + 1, :] = jnp.sum(jnp.where(sels[k], prior, 0.0), axis=0, keepdims=True).astype(I32)
        gate_ref[k:k + 1, :] = exps[k] / denom


def _mix(a, b, x2, w_out_bf, norm_w, router_w, router_b, tm=256):
    n_tok, d = x2.shape
    wa, wb = w_out_bf[:a.shape[1]], w_out_bf[a.shape[1]:]
    ne = router_w.shape[1]
    wr_t = router_w.T
    wr_hi = wr_t.astype(BF16)
    wr_lo = (wr_t - wr_hi.astype(F32)).astype(BF16)
    tri = jnp.asarray(np.triu(np.ones((tm, tm), np.float32), k=1), BF16)
    const = lambda shape: pl.BlockSpec(shape, lambda i: tuple(0 for _ in shape))
    sub = d // LANES
    return pl.pallas_call(
        _mix_body,
        grid=(n_tok // tm,),
        in_specs=[pl.BlockSpec((tm, a.shape[1]), lambda i: (i, 0)),
                  pl.BlockSpec((tm, b.shape[1]), lambda i: (i, 0)),
                  pl.BlockSpec((tm, d), lambda i: (i, 0)),
                  const(wa.shape), const(wb.shape), const((1, d)),
                  const((ne, d)), const((ne, d)), const((ne, 1)), const((tm, tm))],
        out_specs=[pl.BlockSpec((tm, d), lambda i: (i, 0)),
                   pl.BlockSpec((tm, sub, LANES), lambda i: (i, 0, 0)),
                   pl.BlockSpec((TOP_K, tm), lambda i: (0, i)),
                   pl.BlockSpec((TOP_K, tm), lambda i: (0, i)),
                   pl.BlockSpec((TOP_K, tm), lambda i: (0, i)),
                   pl.BlockSpec((ne, LANES), lambda i: (0, 0))],
        out_shape=[jax.ShapeDtypeStruct((n_tok, d), F32),
                   jax.ShapeDtypeStruct((n_tok, sub, LANES), F32),
                   jax.ShapeDtypeStruct((TOP_K, n_tok), I32),
                   jax.ShapeDtypeStruct((TOP_K, n_tok), I32),
                   jax.ShapeDtypeStruct((TOP_K, n_tok), F32),
                   jax.ShapeDtypeStruct((ne, LANES), F32)],
        compiler_params=_cparams(("arbitrary",)),
        name="mix_router",
    )(a, b, x2, wa, wb, norm_w.reshape(1, d), wr_hi, wr_lo, router_b.reshape(ne, 1), tri)


def _dispatch_body(dest_ref, n2_ref, xs_hbm, sem):
    td = n2_ref.shape[0]
    base = pl.program_id(0) * (td * TOP_K)

    def row_copy(t, k):
        return pltpu.make_async_copy(n2_ref.at[t], xs_hbm.at[dest_ref[base + t * TOP_K + k]], sem)

    def issue(t, carry):
        for k in range(TOP_K):
            row_copy(t, k).start()
        return carry

    lax.fori_loop(0, td, issue, 0)

    def drain(t, carry):
        for k in range(TOP_K):
            row_copy(t, k).wait()
        return carry

    lax.fori_loop(0, td, drain, 0)


def _dispatch(dest_flat, n2_rows, n_rows, td=256):
    n_tok, sub, lanes = n2_rows.shape
    return pl.pallas_call(
        _dispatch_body,
        grid_spec=pltpu.PrefetchScalarGridSpec(
            num_scalar_prefetch=1,
            grid=(n_tok // td,),
            in_specs=[pl.BlockSpec((td, sub, lanes), lambda i, dest: (i, 0, 0))],
            out_specs=pl.BlockSpec(memory_space=pl.ANY),
            scratch_shapes=[pltpu.SemaphoreType.DMA(())]),
        out_shape=jax.ShapeDtypeStruct((n_rows, sub, lanes), F32),
        compiler_params=_cparams(("arbitrary",)),
        name="dispatch",
    )(dest_flat, n2_rows)


def _ffn_body(be_ref, nact_ref, xs_ref, w1_ref, b1_ref, w2_ref, b2_ref, ys_ref):
    @pl.when(pl.program_id(0) < nact_ref[0])
    def _():
        sub = xs_ref.shape[1]
        x = jnp.concatenate([xs_ref[:, j, :] for j in range(sub)], axis=1).astype(BF16)
        hdn = jnp.dot(x, w1_ref[0], preferred_element_type=F32) + b1_ref[0]
        d_ff = hdn.shape[1] // 2
        glu = jnp.minimum(hdn[:, :d_ff], SWIGLU_LIMIT)
        lin = jnp.clip(hdn[:, d_ff:], -SWIGLU_LIMIT, SWIGLU_LIMIT)
        act = glu * _sigmoid(SWIGLU_ALPHA * glu) * (lin + 1.0)
        y = jnp.dot(act.astype(BF16), w2_ref[0], preferred_element_type=F32) + b2_ref[0]
        for j in range(sub):
            ys_ref[:, j, :] = y[:, j * LANES:(j + 1) * LANES]


def _ffn(block_expert, n_active, xs, w1_bf, b1, w2_bf, b2):
    n_rows, sub, lanes = xs.shape
    ne, d, f2 = w1_bf.shape
    nb = n_rows // MOE_BLOCK

    def blk(i, be, nact):
        return (jnp.minimum(i, nact[0] - 1), 0, 0)

    def exp(i, be, nact):
        return (be[jnp.minimum(i, nact[0] - 1)], 0, 0)

    return pl.pallas_call(
        _ffn_body,
        grid_spec=pltpu.PrefetchScalarGridSpec(
            num_scalar_prefetch=2,
            grid=(nb,),
            in_specs=[pl.BlockSpec((MOE_BLOCK, sub, lanes), blk),
                      pl.BlockSpec((1, d, f2), exp),
                      pl.BlockSpec((1, 1, f2), exp),
                      pl.BlockSpec((1, f2 // 2, d), exp),
                      pl.BlockSpec((1, 1, d), exp)],
            out_specs=pl.BlockSpec((MOE_BLOCK, sub, lanes), blk)),
        out_shape=jax.ShapeDtypeStruct((n_rows, sub, lanes), F32),
        compiler_params=_cparams(("arbitrary",)),
        name="expert_ffn",
    )(block_expert, n_active, xs, w1_bf, b1.reshape(ne, 1, f2), w2_bf, b2.reshape(ne, 1, d))


def _combine_body(dest_ref, h_ref, gate_ref, nw_ref, ys_hbm, o_ref, buf, sem):
    tc = h_ref.shape[0]
    sub = buf.shape[2]
    base = pl.program_id(0) * (tc * TOP_K)

    def row_copy(t, k):
        return pltpu.make_async_copy(ys_hbm.at[dest_ref[base + t * TOP_K + k]], buf.at[k, t], sem)

    def issue(t, carry):
        for k in range(TOP_K):
            row_copy(t, k).start()
        return carry

    lax.fori_loop(0, tc, issue, 0)

    def drain(t, carry):
        for k in range(TOP_K):
            row_copy(t, k).wait()
        return carry

    lax.fori_loop(0, tc, drain, 0)
    gates = gate_ref[...]
    cols = []
    for j in range(sub):
        acc = gates[:, 0:1] * buf[0, :, j, :]
        for k in range(1, TOP_K):
            acc = acc + gates[:, k:k + 1] * buf[k, :, j, :]
        cols.append(acc)
    y = h_ref[...] + jnp.concatenate(cols, axis=1)
    o_ref[...] = y * lax.rsqrt(jnp.mean(y * y, axis=-1, keepdims=True) + EPS) * nw_ref[...]


def _combine(dest_flat, h, gates_tk, norm_w, ys, tc=128):
    n_tok, d = h.shape
    _, sub, lanes = ys.shape
    return pl.pallas_call(
        _combine_body,
        grid_spec=pltpu.PrefetchScalarGridSpec(
            num_scalar_prefetch=1,
            grid=(n_tok // tc,),
            in_specs=[pl.BlockSpec((tc, d), lambda i, dest: (i, 0)),
                      pl.BlockSpec((tc, TOP_K), lambda i, dest: (i, 0)),
                      pl.BlockSpec((1, d), lambda i, dest: (0, 0)),
                      pl.BlockSpec(memory_space=pl.ANY)],
            out_specs=pl.BlockSpec((tc, d), lambda i, dest: (i, 0)),
            scratch_shapes=[pltpu.VMEM((TOP_K, tc, sub, lanes), F32), pltpu.SemaphoreType.DMA(())]),
        out_shape=jax.ShapeDtypeStruct((n_tok, d), F32),
        compiler_params=_cparams(("arbitrary",)),
        name="combine_norm",
    )(dest_flat, h, gates_tk, norm_w.reshape(1, d), ys)


def kernel(x, norm1_w, w_in, lb_logits, hgrn_norm_w, dw_w, dw_b, conv_ln_w, conv_ln_b, w_out, norm2_w,
           router_w, router_b, w1, b1, w2, b2, final_norm_w):
    batch, t_len, d = x.shape
    depth = w_in.shape[0]
    n_tok = batch * t_len
    hk = HGRN_HEADS * HEAD_DIM
    lb_table = jnp.cumsum(jax.nn.softmax(lb_logits.astype(F32), axis=1), axis=1)
    h2 = x.reshape(n_tok, d)
    out = None
    for l in range(depth):
        proj = _in_proj(h2, norm1_w[l], w_in[l].astype(BF16))
        a = _hgrn(proj, lb_table[0, l], lb_table[1, l], hgrn_norm_w[l], batch, t_len)
        conv_ch = dw_w.shape[2]
        b = _conv(proj, dw_w[l], dw_b[l], conv_ln_w[l], conv_ln_b[l], batch, t_len, (5 * hk) // conv_ch)
        h_mid, n2_rows, eid, rank, gate, cnt = _mix(a, b, h2, w_out[l].astype(BF16), norm2_w[l],
                                                    router_w[l], router_b[l])
        counts = cnt[:, 0].astype(I32)
        padded = ((counts + MOE_BLOCK - 1) // MOE_BLOCK) * MOE_BLOCK
        pad_end = jnp.cumsum(padded)
        pad_start = pad_end - padded
        n_blocks = -(-(n_tok * TOP_K) // MOE_BLOCK) + N_EXPERTS
        block_start = jnp.arange(n_blocks, dtype=I32) * MOE_BLOCK
        block_expert = jnp.minimum(jnp.searchsorted(pad_end, block_start, side='right'),
                                   N_EXPERTS - 1).astype(I32)
        n_active = (pad_end[-1:] // MOE_BLOCK).astype(I32)
        dest_flat = (pad_start[eid] + rank).T.reshape(-1)
        xs = _dispatch(dest_flat, n2_rows, n_blocks * MOE_BLOCK)
        ys = _ffn(block_expert, n_active, xs, w1[l].astype(BF16), b1[l], w2[l].astype(BF16), b2[l])
        nw = final_norm_w if l == depth - 1 else jnp.ones((d,), F32)
        out = _combine(dest_flat, h_mid, gate.T, nw, ys)
        h2 = out
    return out.reshape(batch, t_len, d)
```

```python
import functools
import math

import numpy as np
import jax
import jax.numpy as jnp
from jax import lax
from jax.experimental import pallas as pl
from jax.experimental.pallas import tpu as pltpu

F32 = jnp.float32
BF16 = jnp.bfloat16
I32 = jnp.int32

EPS = 1e-5
HGRN_HEADS = 4
HEAD_DIM = 128
HGRN_CHUNK = 128
CONV_LEN = 31
N_EXPERTS = 32
TOP_K = 4
SWIGLU_LIMIT = 7.0
SWIGLU_ALPHA = 1.702
MOE_BLOCK = 256
LANES = 128
SUBLANES = 8
VMEM_LIMIT = 56 << 20

_NT = (((1,), (1,)), ((), ()))


def _sigmoid(x):
    return 1.0 / (1.0 + jnp.exp(-x))


def _silu(x):
    return x * _sigmoid(x)


def _cparams(sem):
    return pltpu.CompilerParams(dimension_semantics=sem, vmem_limit_bytes=VMEM_LIMIT)


def _row_tile_shape(rows, width):
    return (rows // SUBLANES, width // LANES, SUBLANES, LANES)


def _load_row_tiles(ref, dtype=None):
    groups, lane_groups = ref.shape[0], ref.shape[1]
    cols = [ref[:, j].reshape(groups * SUBLANES, LANES) for j in range(lane_groups)]
    if dtype is not None:
        cols = [c.astype(dtype) for c in cols]
    return jnp.concatenate(cols, axis=1)


def _store_row_tiles(ref, value):
    groups, lane_groups = ref.shape[0], ref.shape[1]
    for j in range(lane_groups):
        ref[:, j] = value[:, j * LANES:(j + 1) * LANES].reshape(groups, SUBLANES, LANES)


def _row_slice(ref, group, sub):
    return ref.at[group, :, pl.ds(sub, 1), :]


def _inproj_body(x_ref, nw_ref, w_ref, o_ref):
    x = x_ref[...]
    n = x * lax.rsqrt(jnp.mean(x * x, axis=-1, keepdims=True) + EPS) * nw_ref[...]
    o_ref[...] = jnp.dot(n.astype(BF16), w_ref[...], preferred_element_type=F32)


def _in_proj(x2, norm_w, w_bf, tm=256):
    n_tok, d = x2.shape
    cols = w_bf.shape[1]
    return pl.pallas_call(
        _inproj_body,
        grid=(n_tok // tm,),
        in_specs=[pl.BlockSpec((tm, d), lambda i: (i, 0)),
                  pl.BlockSpec((1, d), lambda i: (0, 0)),
                  pl.BlockSpec((d, cols), lambda i: (0, 0))],
        out_specs=pl.BlockSpec((tm, cols), lambda i: (i, 0)),
        out_shape=jax.ShapeDtypeStruct((n_tok, cols), F32),
        compiler_params=_cparams(("parallel",)),
        name="in_proj",
    )(x2, norm_w.reshape(1, d), w_bf)


def _hgrn_levels(c):
    return [c >> (i + 1) for i in range(int(math.log2(c)))]


def _hgrn_sum_matrices(c, fwd):
    r = np.arange(c)
    mats = []
    for m in _hgrn_levels(c):
        mat = np.zeros((c, c), np.float32)
        for t in range(c):
            p0 = t & ~(2 * m - 1)
            upper = (t & m) != 0
            if fwd:
                if upper:
                    mat[t, p0 + m:t + 1] = 1.0
                else:
                    mat[t, t + 1:p0 + m] = 1.0
            else:
                if upper:
                    mat[t, p0 + m:t] = 1.0
                else:
                    mat[t, t:p0 + m] = 1.0
        mats.append(mat)
    if fwd:
        mats.append((r[None, :] <= r[:, None]).astype(np.float32))
        mats.append((r[None, :] > r[:, None]).astype(np.float32))
    else:
        mats.append((r[None, :] >= r[:, None]).astype(np.float32))
        mats.append((r[None, :] < r[:, None]).astype(np.float32))
    return np.concatenate(mats, axis=0)


def _hgrn_chunk(q, k, lf, v_bf, vt_bf, st, sums_ref, fwd, c):
    levels = _hgrn_levels(c)
    nlev = len(levels)
    dk = q.shape[1]
    hi = lf.astype(BF16)
    lo = (lf - hi.astype(F32)).astype(BF16)
    both = jnp.dot(sums_ref[...], jnp.concatenate([hi, lo], axis=1), preferred_element_type=F32)
    e_all = jnp.exp(both[:, :dk] + both[:, dk:])
    row = lax.broadcasted_iota(I32, (c, dk), 0)
    sep = lax.broadcasted_iota(I32, (c, c), 0) ^ lax.broadcasted_iota(I32, (c, c), 1)
    a = lax.dot_general(q.astype(BF16), k.astype(BF16), _NT, preferred_element_type=F32)
    for l, m in enumerate(levels):
        e_l = e_all[l * c:(l + 1) * c]
        upper = (row & m) != 0
        is_q = upper if fwd else jnp.logical_not(upper)
        q_rel = jnp.where(is_q, q * e_l, 0.0).astype(BF16)
        k_rel = jnp.where(is_q, 0.0, k * e_l).astype(BF16)
        a_l = lax.dot_general(q_rel, k_rel, _NT, preferred_element_type=F32)
        a = jnp.where((sep >= m) & (sep < 2 * m), a_l, a)
    q_in = (q * e_all[nlev * c:(nlev + 1) * c]).astype(BF16)
    k_out = (k * e_all[(nlev + 1) * c:]).astype(BF16)
    o = jnp.dot(a.astype(BF16), v_bf, preferred_element_type=F32)
    o = o + lax.dot_general(q_in, st.astype(BF16), _NT, preferred_element_type=F32)
    edge = (nlev + 1) * c - 1 if fwd else nlev * c
    st_new = st * e_all[edge:edge + 1] + jnp.dot(vt_bf, k_out, preferred_element_type=F32)
    return o, st_new


def _hgrn_body(qz_ref, ff_ref, fb_ref, iv_ref, gz_ref, lbf_ref, lbb_ref, nw_ref, sf_ref, sb_ref, o_ref,
               q_s, kf_s, lff_s, kb_s, lfb_s, v_s, vt_s, acc_s, *, c):
    t_len, dk = qz_ref.shape
    nch = t_len // c
    q_s[...] = _silu(qz_ref[...])

    def gate(fz_ref, lb_ref, k_s, lf_s):
        lb = lb_ref[0]
        f = lb + (1.0 - lb) * _sigmoid(fz_ref[...])
        lf_s[...] = jnp.log(f)
        k_s[...] = 1.0 - f

    gate(ff_ref, lbf_ref, kf_s, lff_s)
    gate(fb_ref, lbb_ref, kb_s, lfb_s)
    v = iv_ref[...]
    v_s[...] = v.astype(BF16)
    for ci in range(nch):
        vt_s[ci] = v[ci * c:(ci + 1) * c, :].T.astype(BF16)

    def run(k_s, lf_s, sums_ref, fwd):
        def step(i, st):
            ci = i if fwd else nch - 1 - i
            rows = pl.ds(pl.multiple_of(ci * c, c), c)
            o, st = _hgrn_chunk(q_s[rows, :], k_s[rows, :], lf_s[rows, :], v_s[rows, :], vt_s[ci],
                                st, sums_ref, fwd, c)
            if fwd:
                acc_s[rows, :] = o
            else:
                acc_s[rows, :] += o
            return st
        lax.fori_loop(0, nch, step, jnp.zeros((dk, dk), F32))

    run(kf_s, lff_s, sf_ref, True)
    run(kb_s, lfb_s, sb_ref, False)
    o = acc_s[...]
    o = o * lax.rsqrt(jnp.mean(o * o, axis=-1, keepdims=True) + EPS) * nw_ref[0]
    o_ref[...] = (o * _silu(gz_ref[...])).astype(o_ref.dtype)


def _hgrn(proj, lb_f, lb_b, norm_w, batch, t_len):
    h, dk, c = HGRN_HEADS, HEAD_DIM, HGRN_CHUNK
    nrows = len(_hgrn_levels(c)) + 2
    sums_f = jnp.asarray(_hgrn_sum_matrices(c, True), BF16)
    sums_b = jnp.asarray(_hgrn_sum_matrices(c, False), BF16)

    def col(group):
        return pl.BlockSpec((t_len, dk), lambda b, hh: (b, group * h + hh))

    def per_head():
        return pl.BlockSpec((1, 1, dk), lambda b, hh: (hh, 0, 0))

    const = pl.BlockSpec((nrows * c, c), lambda b, hh: (0, 0))
    seq = lambda dt: pltpu.VMEM((t_len, dk), dt)
    return pl.pallas_call(
        functools.partial(_hgrn_body, c=c),
        grid=(batch, h),
        in_specs=[col(0), col(1), col(2), col(3), col(4), per_head(), per_head(), per_head(), const, const],
        out_specs=pl.BlockSpec((t_len, dk), lambda b, hh: (b, hh)),
        out_shape=jax.ShapeDtypeStruct((batch * t_len, h * dk), BF16),
        scratch_shapes=[seq(F32), seq(F32), seq(F32), seq(F32), seq(F32), seq(BF16),
                        pltpu.VMEM((t_len // c, dk, c), BF16), seq(F32)],
        compiler_params=_cparams(("parallel", "parallel")),
        name="hgrn",
    )(proj, proj, proj, proj, proj, lb_f.reshape(h, 1, dk), lb_b.reshape(h, 1, dk),
      norm_w.reshape(h, 1, dk), sums_f, sums_b)


CONV_ROWS = 64
CONV_HALO = 16


def _conv_body(cv_ref, cg_ref, w_ref, b_ref, lnw_ref, lnb_ref, o_ref, u_s):
    t_len, ch = cv_ref.shape
    halo, rows = CONV_HALO, CONV_ROWS
    shift0 = halo - (CONV_LEN - 1) // 2
    win = rows + 2 * halo
    u_s[0:halo, :] = jnp.zeros((halo, ch), F32)
    u_s[halo + t_len:, :] = jnp.zeros((halo, ch), F32)
    u_s[halo:halo + t_len, :] = cv_ref[...] * _sigmoid(cg_ref[...])

    def step(i, carry):
        t0 = pl.multiple_of(i * rows, rows)
        groups = []
        for g in range(ch // LANES):
            lanes = slice(g * LANES, (g + 1) * LANES)
            window = u_s[pl.ds(t0, win), lanes]
            acc = jnp.zeros((rows, LANES), F32)
            for s in range(SUBLANES):
                taps = [j for j in range(CONV_LEN) if (j + shift0) % SUBLANES == s]
                if not taps:
                    continue
                span = max(j + shift0 for j in taps) - s + rows
                shifted = window[s:s + span]
                for j in taps:
                    off = j + shift0 - s
                    acc = acc + w_ref[j:j + 1, lanes] * shifted[off:off + rows]
            groups.append(acc + b_ref[:, lanes])
        y = jnp.concatenate(groups, axis=1)
        mu = jnp.mean(y, axis=-1, keepdims=True)
        yc = y - mu
        var = jnp.mean(yc * yc, axis=-1, keepdims=True)
        z = yc * lax.rsqrt(var + EPS) * lnw_ref[...] + lnb_ref[...]
        o_ref[pl.ds(t0, rows), :] = _silu(z).astype(o_ref.dtype)
        return carry

    lax.fori_loop(0, t_len // rows, step, 0)


def _conv(proj, dw_w, dw_b, ln_w, ln_b, batch, t_len, first_col_block):
    ch = dw_w.shape[1]
    vec = pl.BlockSpec((1, ch), lambda b: (0, 0))
    return pl.pallas_call(
        _conv_body,
        grid=(batch,),
        in_specs=[pl.BlockSpec((t_len, ch), lambda b: (b, first_col_block)),
                  pl.BlockSpec((t_len, ch), lambda b: (b, first_col_block + 1)),
                  pl.BlockSpec((CONV_LEN, ch), lambda b: (0, 0)), vec, vec, vec],
        out_specs=pl.BlockSpec((t_len, ch), lambda b: (b, 0)),
        out_shape=jax.ShapeDtypeStruct((batch * t_len, ch), BF16),
        scratch_shapes=[pltpu.VMEM((t_len + 2 * CONV_HALO, ch), F32)],
        compiler_params=_cparams(("parallel",)),
        name="conv",
    )(proj, proj, dw_w, dw_b.reshape(1, ch), ln_w.reshape(1, ch), ln_b.reshape(1, ch))


def _mix_body(a_ref, b_ref, x_ref, wa_ref, wb_ref, nw_ref, wrh_ref, wrl_ref, rb_ref, tri_ref,
              h_ref, n2_ref, eid_ref, rank_ref, gate_ref, cnt_ref):
    tm = x_ref.shape[0]
    ne = wrh_ref.shape[0]

    @pl.when(pl.program_id(0) == 0)
    def _():
        cnt_ref[...] = jnp.zeros(cnt_ref.shape, F32)

    h = x_ref[...] + jnp.dot(a_ref[...], wa_ref[...], preferred_element_type=F32) \
        + jnp.dot(b_ref[...], wb_ref[...], preferred_element_type=F32)
    h_ref[...] = h
    n2 = h * lax.rsqrt(jnp.mean(h * h, axis=-1, keepdims=True) + EPS) * nw_ref[...]
    _store_row_tiles(n2_ref, n2)
    hi = n2.astype(BF16)
    lo = (n2 - hi.astype(F32)).astype(BF16)
    logits = (lax.dot_general(wrh_ref[...], hi, _NT, preferred_element_type=F32)
              + lax.dot_general(wrh_ref[...], lo, _NT, preferred_element_type=F32)
              + lax.dot_general(wrl_ref[...], hi, _NT, preferred_element_type=F32)
              + rb_ref[...])
    e_iota = lax.broadcasted_iota(I32, (ne, tm), 0)
    work = logits
    sels, vals, ids = [], [], []
    for _ in range(TOP_K):
        mx = jnp.max(work, axis=0, keepdims=True)
        idx = jnp.min(jnp.where(work == mx, e_iota, ne), axis=0, keepdims=True)
        sel = e_iota == idx
        work = jnp.where(sel, -jnp.inf, work)
        sels.append(sel)
        vals.append(mx)
        ids.append(idx)
    exps = [jnp.exp(v - vals[0]) for v in vals]
    denom = exps[0] + exps[1] + exps[2] + exps[3]
    chosen = jnp.zeros((ne, tm), F32)
    for sel in sels:
        chosen = jnp.where(sel, 1.0, chosen)
    chosen_bf = chosen.astype(BF16)
    before = cnt_ref[...]
    prior = jnp.dot(chosen_bf, tri_ref[...], preferred_element_type=F32) \
        + jnp.concatenate([before] * (tm // LANES), axis=1)
    cnt_ref[...] = before + jnp.dot(chosen_bf, jnp.ones((tm, LANES), BF16), preferred_element_type=F32)
    for k in range(TOP_K):
        eid_ref[k:k + 1, :] = ids[k]
        rank_ref[k:k + 1, :] = jnp.sum(jnp.where(sels[k], prior, 0.0), axis=0, keepdims=True).astype(I32)
        gate_ref[k:k + 1, :] = exps[k] / denom


def _mix(a, b, x2, w_out_bf, norm_w, router_w, router_b, tm=256):
    n_tok, d = x2.shape
    wa, wb = w_out_bf[:a.shape[1]], w_out_bf[a.shape[1]:]
    ne = router_w.shape[1]
    wr_t = router_w.T
    wr_hi = wr_t.astype(BF16)
    wr_lo = (wr_t - wr_hi.astype(F32)).astype(BF16)
    tri = jnp.asarray(np.triu(np.ones((tm, tm), np.float32), k=1), BF16)
    const = lambda shape: pl.BlockSpec(shape, lambda i: tuple(0 for _ in shape))
    return pl.pallas_call(
        _mix_body,
        grid=(n_tok // tm,),
        in_specs=[pl.BlockSpec((tm, a.shape[1]), lambda i: (i, 0)),
                  pl.BlockSpec((tm, b.shape[1]), lambda i: (i, 0)),
                  pl.BlockSpec((tm, d), lambda i: (i, 0)),
                  const(wa.shape), const(wb.shape), const((1, d)),
                  const((ne, d)), const((ne, d)), const((ne, 1)), const((tm, tm))],
        out_specs=[pl.BlockSpec((tm, d), lambda i: (i, 0)),
                   pl.BlockSpec(_row_tile_shape(tm, d), lambda i: (i, 0, 0, 0)),
                   pl.BlockSpec((TOP_K, tm), lambda i: (0, i)),
                   pl.BlockSpec((TOP_K, tm), lambda i: (0, i)),
                   pl.BlockSpec((TOP_K, tm), lambda i: (0, i)),
                   pl.BlockSpec((ne, LANES), lambda i: (0, 0))],
        out_shape=[jax.ShapeDtypeStruct((n_tok, d), F32),
                   jax.ShapeDtypeStruct(_row_tile_shape(n_tok, d), F32),
                   jax.ShapeDtypeStruct((TOP_K, n_tok), I32),
                   jax.ShapeDtypeStruct((TOP_K, n_tok), I32),
                   jax.ShapeDtypeStruct((TOP_K, n_tok), F32),
                   jax.ShapeDtypeStruct((ne, LANES), F32)],
        compiler_params=_cparams(("arbitrary",)),
        name="mix_router",
    )(a, b, x2, wa, wb, norm_w.reshape(1, d), wr_hi, wr_lo, router_b.reshape(ne, 1), tri)


def _row_of(ref, r):
    return _row_slice(ref, lax.shift_right_logical(r, 3), lax.bitwise_and(r, SUBLANES - 1))


def _dispatch_body(dest_ref, n2_ref, xs_hbm, sem):
    groups = n2_ref.shape[0]
    base = pl.program_id(0) * (groups * SUBLANES * TOP_K)

    def copies(g):
        out = []
        for s in range(SUBLANES):
            for k in range(TOP_K):
                r = dest_ref[base + (g * SUBLANES + s) * TOP_K + k]
                out.append(pltpu.make_async_copy(_row_slice(n2_ref, g, s), _row_of(xs_hbm, r), sem))
        return out

    def issue(g, carry):
        for cp in copies(g):
            cp.start()
        return carry

    lax.fori_loop(0, groups, issue, 0)

    def drain(g, carry):
        for cp in copies(g):
            cp.wait()
        return carry

    lax.fori_loop(0, groups, drain, 0)


def _dispatch(dest_flat, n2_rows, n_rows, td=256):
    width = n2_rows.shape[1] * LANES
    n_tok = n2_rows.shape[0] * SUBLANES
    return pl.pallas_call(
        _dispatch_body,
        grid_spec=pltpu.PrefetchScalarGridSpec(
            num_scalar_prefetch=1,
            grid=(n_tok // td,),
            in_specs=[pl.BlockSpec(_row_tile_shape(td, width), lambda i, dest: (i, 0, 0, 0))],
            out_specs=pl.BlockSpec(memory_space=pl.ANY),
            scratch_shapes=[pltpu.SemaphoreType.DMA(())]),
        out_shape=jax.ShapeDtypeStruct(_row_tile_shape(n_rows, width), F32),
        compiler_params=_cparams(("arbitrary",)),
        name="dispatch",
    )(dest_flat, n2_rows)


def _ffn_body(be_ref, nact_ref, xs_ref, w1_ref, b1_ref, w2_ref, b2_ref, ys_ref):
    @pl.when(pl.program_id(0) < nact_ref[0])
    def _():
        x = _load_row_tiles(xs_ref, BF16)
        hdn = jnp.dot(x, w1_ref[0], preferred_element_type=F32) + b1_ref[0]
        d_ff = hdn.shape[1] // 2
        glu = jnp.minimum(hdn[:, :d_ff], SWIGLU_LIMIT)
        lin = jnp.clip(hdn[:, d_ff:], -SWIGLU_LIMIT, SWIGLU_LIMIT)
        act = glu * _sigmoid(SWIGLU_ALPHA * glu) * (lin + 1.0)
        y = jnp.dot(act.astype(BF16), w2_ref[0], preferred_element_type=F32) + b2_ref[0]
        _store_row_tiles(ys_ref, y)


def _ffn(block_expert, n_active, xs, w1_bf, b1, w2_bf, b2):
    ne, d, f2 = w1_bf.shape
    nb = xs.shape[0] * SUBLANES // MOE_BLOCK
    block = _row_tile_shape(MOE_BLOCK, d)

    def blk(i, be, nact):
        return (jnp.minimum(i, nact[0] - 1), 0, 0, 0)

    def exp(i, be, nact):
        return (be[jnp.minimum(i, nact[0] - 1)], 0, 0)

    return pl.pallas_call(
        _ffn_body,
        grid_spec=pltpu.PrefetchScalarGridSpec(
            num_scalar_prefetch=2,
            grid=(nb,),
            in_specs=[pl.BlockSpec(block, blk),
                      pl.BlockSpec((1, d, f2), exp),
                      pl.BlockSpec((1, 1, f2), exp),
                      pl.BlockSpec((1, f2 // 2, d), exp),
                      pl.BlockSpec((1, 1, d), exp)],
            out_specs=pl.BlockSpec(block, blk)),
        out_shape=jax.ShapeDtypeStruct(xs.shape, F32),
        compiler_params=_cparams(("arbitrary",)),
        name="expert_ffn",
    )(block_expert, n_active, xs, w1_bf, b1.reshape(ne, 1, f2), w2_bf, b2.reshape(ne, 1, d))


def _combine_body(dest_ref, h_ref, gate_ref, nw_ref, ys_hbm, o_ref, buf, sem):
    step = pl.program_id(0)
    groups, lane_groups = buf.shape[2], buf.shape[3]
    tc = groups * SUBLANES

    def issue(at_step, slot):
        base = at_step * (tc * TOP_K)

        def body(g, carry):
            for s in range(SUBLANES):
                for k in range(TOP_K):
                    r = dest_ref[base + (g * SUBLANES + s) * TOP_K + k]
                    pltpu.make_async_copy(_row_of(ys_hbm, r), _row_slice(buf.at[slot, k], g, s),
                                          sem.at[slot]).start()
            return carry

        lax.fori_loop(0, groups, body, 0)

    def drain(slot):
        def body(g, carry):
            for s in range(SUBLANES):
                for k in range(TOP_K):
                    pltpu.make_async_copy(_row_slice(ys_hbm, 0, 0), _row_slice(buf.at[slot, k], g, s),
                                          sem.at[slot]).wait()
            return carry

        lax.fori_loop(0, groups, body, 0)

    @pl.when(step == 0)
    def _():
        issue(0, 0)

    @pl.when(step + 1 < pl.num_programs(0))
    def _():
        issue(step + 1, (step + 1) % 2)

    slot = step % 2
    drain(slot)
    gates = gate_ref[...]
    cols = []
    for j in range(lane_groups):
        acc = gates[:, 0:1] * buf[slot, 0, :, j].reshape(tc, LANES)
        for k in range(1, TOP_K):
            acc = acc + gates[:, k:k + 1] * buf[slot, k, :, j].reshape(tc, LANES)
        cols.append(acc)
    y = h_ref[...] + jnp.concatenate(cols, axis=1)
    o_ref[...] = y * lax.rsqrt(jnp.mean(y * y, axis=-1, keepdims=True) + EPS) * nw_ref[...]


def _combine(dest_flat, h, gates_tk, norm_w, ys, tc=128):
    n_tok, d = h.shape
    return pl.pallas_call(
        _combine_body,
        grid_spec=pltpu.PrefetchScalarGridSpec(
            num_scalar_prefetch=1,
            grid=(n_tok // tc,),
            in_specs=[pl.BlockSpec((tc, d), lambda i, dest: (i, 0)),
                      pl.BlockSpec((tc, TOP_K), lambda i, dest: (i, 0)),
                      pl.BlockSpec((1, d), lambda i, dest: (0, 0)),
                      pl.BlockSpec(memory_space=pl.ANY)],
            out_specs=pl.BlockSpec((tc, d), lambda i, dest: (i, 0)),
            scratch_shapes=[pltpu.VMEM((2, TOP_K) + _row_tile_shape(tc, d), F32),
                            pltpu.SemaphoreType.DMA((2,))]),
        out_shape=jax.ShapeDtypeStruct((n_tok, d), F32),
        compiler_params=_cparams(("arbitrary",)),
        name="combine_norm",
    )(dest_flat, h, gates_tk, norm_w.reshape(1, d), ys)


def kernel(x, norm1_w, w_in, lb_logits, hgrn_norm_w, dw_w, dw_b, conv_ln_w, conv_ln_b, w_out, norm2_w,
           router_w, router_b, w1, b1, w2, b2, final_norm_w):
    batch, t_len, d = x.shape
    assert w_in.shape[0] == 1, "single-layer block"
    n_tok = batch * t_len
    hk = HGRN_HEADS * HEAD_DIM
    conv_ch = dw_w.shape[2]
    lb_table = jnp.cumsum(jax.nn.softmax(lb_logits.astype(F32), axis=1), axis=1)
    x2 = x.reshape(n_tok, d)
    proj = _in_proj(x2, norm1_w[0], w_in[0].astype(BF16))
    a = _hgrn(proj, lb_table[0, 0], lb_table[1, 0], hgrn_norm_w[0], batch, t_len)
    b = _conv(proj, dw_w[0], dw_b[0], conv_ln_w[0], conv_ln_b[0], batch, t_len, (5 * hk) // conv_ch)
    h_mid, n2_rows, eid, rank, gate, cnt = _mix(a, b, x2, w_out[0].astype(BF16), norm2_w[0],
                                                router_w[0], router_b[0])
    counts = cnt[:, 0].astype(I32)
    padded = ((counts + MOE_BLOCK - 1) // MOE_BLOCK) * MOE_BLOCK
    e_ids = jnp.arange(N_EXPERTS, dtype=I32)
    pad_end = jnp.sum(jnp.where(e_ids[None, :] <= e_ids[:, None], padded[None, :], 0), axis=1)
    pad_start = pad_end - padded
    n_blocks = -(-(n_tok * TOP_K) // MOE_BLOCK) + N_EXPERTS
    block_start = jnp.arange(n_blocks, dtype=I32) * MOE_BLOCK
    block_expert = jnp.minimum(jnp.sum((pad_end[None, :] <= block_start[:, None]).astype(I32), axis=1),
                               N_EXPERTS - 1)
    n_active = pad_end[-1:] // MOE_BLOCK
    start_of = jnp.sum(jnp.where(eid[:, :, None] == e_ids, pad_start, 0), axis=-1)
    dest_flat = (start_of + rank).T.reshape(-1)
    xs = _dispatch(dest_flat, n2_rows, n_blocks * MOE_BLOCK)
    ys = _ffn(block_expert, n_active, xs, w1[0].astype(BF16), b1[0], w2[0].astype(BF16), b2[0])
    out = _combine(dest_flat, h_mid, gate.T, final_norm_w, ys)
    return out.reshape(batch, t_len, d)
```

```python
import functools
import math

import numpy as np
import jax
import jax.numpy as jnp
from jax import lax
from jax.experimental import pallas as pl
from jax.experimental.pallas import tpu as pltpu

F32 = jnp.float32
BF16 = jnp.bfloat16
I32 = jnp.int32

EPS = 1e-5
HGRN_HEADS = 4
HEAD_DIM = 128
HGRN_CHUNK = 128
CONV_LEN = 31
N_EXPERTS = 32
TOP_K = 4
SWIGLU_LIMIT = 7.0
SWIGLU_ALPHA = 1.702
MOE_BLOCK = 256
LANES = 128
SUBLANES = 8
VMEM_LIMIT = 56 << 20

_NT = (((1,), (1,)), ((), ()))


def _sigmoid(x):
    return 0.5 * jnp.tanh(0.5 * x) + 0.5


def _silu(x):
    return x * _sigmoid(x)


def _cparams(sem):
    return pltpu.CompilerParams(dimension_semantics=sem, vmem_limit_bytes=VMEM_LIMIT)


def _row_tile_shape(rows, width):
    return (rows // SUBLANES, width // LANES, SUBLANES, LANES)


def _load_row_tiles(ref, dtype=None):
    groups, lane_groups = ref.shape[0], ref.shape[1]
    cols = [ref[:, j].reshape(groups * SUBLANES, LANES) for j in range(lane_groups)]
    if dtype is not None:
        cols = [c.astype(dtype) for c in cols]
    return jnp.concatenate(cols, axis=1)


def _store_row_tiles(ref, value):
    groups, lane_groups = ref.shape[0], ref.shape[1]
    for j in range(lane_groups):
        ref[:, j] = value[:, j * LANES:(j + 1) * LANES].reshape(groups, SUBLANES, LANES)


def _row_slice(ref, group, sub):
    return ref.at[group, :, pl.ds(sub, 1), :]


def _inproj_body(x_ref, nw_ref, w_ref, o_ref):
    x = x_ref[...]
    n = x * lax.rsqrt(jnp.mean(x * x, axis=-1, keepdims=True) + EPS) * nw_ref[...]
    o_ref[...] = jnp.dot(n.astype(BF16), w_ref[...], preferred_element_type=F32)


def _in_proj(x2, norm_w, w_bf, tm=256):
    n_tok, d = x2.shape
    cols = w_bf.shape[1]
    return pl.pallas_call(
        _inproj_body,
        grid=(n_tok // tm,),
        in_specs=[pl.BlockSpec((tm, d), lambda i: (i, 0)),
                  pl.BlockSpec((1, d), lambda i: (0, 0)),
                  pl.BlockSpec((d, cols), lambda i: (0, 0))],
        out_specs=pl.BlockSpec((tm, cols), lambda i: (i, 0)),
        out_shape=jax.ShapeDtypeStruct((n_tok, cols), F32),
        compiler_params=_cparams(("parallel",)),
        name="in_proj",
    )(x2, norm_w.reshape(1, d), w_bf)


def _hgrn_levels(c):
    return [c >> (i + 1) for i in range(int(math.log2(c)))]


def _hgrn_sum_matrices(c, fwd):
    r = np.arange(c)
    m = _HGRN_SMALL_LEVEL
    small = np.zeros((c, c), np.float32)
    for t in range(c):
        p0 = t & ~(2 * m - 1)
        upper = (t & m) != 0
        if fwd:
            if upper:
                small[t, p0 + m:t + 1] = 1.0
            else:
                small[t, t + 1:p0 + m] = 1.0
        else:
            if upper:
                small[t, p0 + m:t] = 1.0
            else:
                small[t, t:p0 + m] = 1.0
    if fwd:
        mats = [r[None, :] <= r[:, None], r[None, :] > r[:, None], small]
    else:
        mats = [r[None, :] >= r[:, None], r[None, :] < r[:, None], small]
    return np.concatenate([np.asarray(x, np.float32) for x in mats], axis=0)


_HGRN_SMALL_LEVEL = 2


def _hgrn_chunk(q, k, lf, v_bf, vt_bf, st, sums_ref, fwd, c):
    dk = q.shape[1]
    hi = lf.astype(BF16)
    lo = (lf - hi.astype(F32)).astype(BF16)
    both = jnp.dot(sums_ref[...], jnp.concatenate([hi, lo], axis=1), preferred_element_type=F32)
    sums = both[:, :dk] + both[:, dk:]
    b = sums[0:c]
    d_out = sums[c:2 * c]
    row = lax.broadcasted_iota(I32, (c, dk), 0)
    rr = lax.broadcasted_iota(I32, (c, c), 0)
    cc = lax.broadcasted_iota(I32, (c, c), 1)
    sep = rr ^ cc
    a = None
    for m in _hgrn_levels(c):
        upper = (row & m) != 0
        is_q = upper if fwd else jnp.logical_not(upper)
        if m == 1:
            d = jnp.where(is_q, lf, 0.0)
        elif m == _HGRN_SMALL_LEVEL:
            d = sums[2 * c:3 * c]
        else:
            blocks = c // (2 * m)
            at = m - 1 if fwd else m
            edge = b.reshape(blocks, 2 * m, dk)[:, at:at + 1, :]
            edge = jnp.broadcast_to(edge, (blocks, 2 * m, dk)).reshape(c, dk)
            d = jnp.where(is_q, b - edge, edge - b)
        r = (jnp.where(is_q, q, k) * jnp.exp2(d)).astype(BF16)
        a_l = lax.dot_general(r, r, _NT, preferred_element_type=F32)
        a = a_l if a is None else jnp.where(sep < 2 * m, a_l, a)
    a_diag = lax.dot_general(q.astype(BF16), k.astype(BF16), _NT, preferred_element_type=F32)
    a = jnp.where(sep < 1, a_diag, a)
    a = jnp.where(rr >= cc if fwd else rr <= cc, a, 0.0)
    e_b = jnp.exp2(b)
    q_in = (q * e_b).astype(BF16)
    k_out = (k * jnp.exp2(d_out)).astype(BF16)
    o = jnp.dot(a.astype(BF16), v_bf, preferred_element_type=F32)
    o = o + lax.dot_general(q_in, st.astype(BF16), _NT, preferred_element_type=F32)
    edge_row = c - 1 if fwd else 0
    st_new = st * e_b[edge_row:edge_row + 1] + jnp.dot(vt_bf, k_out, preferred_element_type=F32)
    return o, st_new


def _hgrn_body(qz_ref, ff_ref, fb_ref, iv_ref, gz_ref, lbf_ref, lbb_ref, nw_ref, sf_ref, sb_ref, o_ref,
               q_s, kf_s, lff_s, kb_s, lfb_s, v_s, vt_s, acc_s, *, c):
    t_len, dk = qz_ref.shape
    nch = t_len // c
    q_s[...] = _silu(qz_ref[...])

    def gate(fz_ref, lb_ref, k_s, lf_s):
        lb = lb_ref[0]
        f = lb + (1.0 - lb) * _sigmoid(fz_ref[...])
        lf_s[...] = jnp.log2(f)
        k_s[...] = 1.0 - f

    gate(ff_ref, lbf_ref, kf_s, lff_s)
    gate(fb_ref, lbb_ref, kb_s, lfb_s)
    v = iv_ref[...]
    v_s[...] = v.astype(BF16)
    for ci in range(nch):
        vt_s[ci] = v[ci * c:(ci + 1) * c, :].T.astype(BF16)

    def run(k_s, lf_s, sums_ref, fwd):
        def step(i, st):
            ci = i if fwd else nch - 1 - i
            rows = pl.ds(pl.multiple_of(ci * c, c), c)
            o, st = _hgrn_chunk(q_s[rows, :], k_s[rows, :], lf_s[rows, :], v_s[rows, :], vt_s[ci],
                                st, sums_ref, fwd, c)
            if fwd:
                acc_s[rows, :] = o
            else:
                acc_s[rows, :] += o
            return st
        lax.fori_loop(0, nch, step, jnp.zeros((dk, dk), F32), unroll=4)

    run(kf_s, lff_s, sf_ref, True)
    run(kb_s, lfb_s, sb_ref, False)
    o = acc_s[...]
    o = o * lax.rsqrt(jnp.mean(o * o, axis=-1, keepdims=True) + EPS) * nw_ref[0]
    o_ref[...] = (o * _silu(gz_ref[...])).astype(o_ref.dtype)


def _hgrn(proj, lb_f, lb_b, norm_w, batch, t_len):
    h, dk, c = HGRN_HEADS, HEAD_DIM, HGRN_CHUNK
    nrows = 3
    sums_f = jnp.asarray(_hgrn_sum_matrices(c, True), BF16)
    sums_b = jnp.asarray(_hgrn_sum_matrices(c, False), BF16)

    def col(group):
        return pl.BlockSpec((t_len, dk), lambda b, hh: (b, group * h + hh))

    def per_head():
        return pl.BlockSpec((1, 1, dk), lambda b, hh: (hh, 0, 0))

    const = pl.BlockSpec((nrows * c, c), lambda b, hh: (0, 0))
    seq = lambda dt: pltpu.VMEM((t_len, dk), dt)
    return pl.pallas_call(
        functools.partial(_hgrn_body, c=c),
        grid=(batch, h),
        in_specs=[col(0), col(1), col(2), col(3), col(4), per_head(), per_head(), per_head(), const, const],
        out_specs=pl.BlockSpec((t_len, dk), lambda b, hh: (b, hh)),
        out_shape=jax.ShapeDtypeStruct((batch * t_len, h * dk), BF16),
        scratch_shapes=[seq(F32), seq(F32), seq(F32), seq(F32), seq(F32), seq(BF16),
                        pltpu.VMEM((t_len // c, dk, c), BF16), seq(F32)],
        compiler_params=_cparams(("parallel", "parallel")),
        name="hgrn",
    )(proj, proj, proj, proj, proj, lb_f.reshape(h, 1, dk), lb_b.reshape(h, 1, dk),
      norm_w.reshape(h, 1, dk), sums_f, sums_b)


CONV_ROWS = 64
CONV_HALO = 16


def _conv_body(cv_ref, cg_ref, w_ref, b_ref, lnw_ref, lnb_ref, o_ref, u_s):
    t_len, ch = cv_ref.shape
    halo, rows = CONV_HALO, CONV_ROWS
    shift0 = halo - (CONV_LEN - 1) // 2
    win = rows + 2 * halo
    u_s[0:halo, :] = jnp.zeros((halo, ch), F32)
    u_s[halo + t_len:, :] = jnp.zeros((halo, ch), F32)
    u_s[halo:halo + t_len, :] = cv_ref[...] * _sigmoid(cg_ref[...])

    def step(i, carry):
        t0 = pl.multiple_of(i * rows, rows)
        groups = []
        for g in range(ch // LANES):
            lanes = slice(g * LANES, (g + 1) * LANES)
            window = u_s[pl.ds(t0, win), lanes]
            acc = jnp.zeros((rows, LANES), F32)
            for s in range(SUBLANES):
                shifted = window if s == 0 else pltpu.roll(window, win - s, 0)
                for j in range(CONV_LEN):
                    if (j + shift0) % SUBLANES == s:
                        off = j + shift0 - s
                        acc = acc + w_ref[j:j + 1, lanes] * shifted[off:off + rows]
            groups.append(acc + b_ref[:, lanes])
        y = jnp.concatenate(groups, axis=1)
        mu = jnp.mean(y, axis=-1, keepdims=True)
        yc = y - mu
        var = jnp.mean(yc * yc, axis=-1, keepdims=True)
        z = yc * lax.rsqrt(var + EPS) * lnw_ref[...] + lnb_ref[...]
        o_ref[pl.ds(t0, rows), :] = _silu(z).astype(o_ref.dtype)
        return carry

    lax.fori_loop(0, t_len // rows, step, 0)


def _conv(proj, dw_w, dw_b, ln_w, ln_b, batch, t_len, first_col_block):
    ch = dw_w.shape[1]
    vec = pl.BlockSpec((1, ch), lambda b: (0, 0))
    return pl.pallas_call(
        _conv_body,
        grid=(batch,),
        in_specs=[pl.BlockSpec((t_len, ch), lambda b: (b, first_col_block)),
                  pl.BlockSpec((t_len, ch), lambda b: (b, first_col_block + 1)),
                  pl.BlockSpec((CONV_LEN, ch), lambda b: (0, 0)), vec, vec, vec],
        out_specs=pl.BlockSpec((t_len, ch), lambda b: (b, 0)),
        out_shape=jax.ShapeDtypeStruct((batch * t_len, ch), BF16),
        scratch_shapes=[pltpu.VMEM((t_len + 2 * CONV_HALO, ch), F32)],
        compiler_params=_cparams(("parallel",)),
        name="conv",
    )(proj, proj, dw_w, dw_b.reshape(1, ch), ln_w.reshape(1, ch), ln_b.reshape(1, ch))


def _mix_body(a_ref, b_ref, x_ref, wa_ref, wb_ref, nw_ref, wrh_ref, wrl_ref, rb_ref, tri_ref,
              h_ref, n2_ref, eid_ref, rank_ref, gate_ref, cnt_ref):
    tm = x_ref.shape[0]
    ne = wrh_ref.shape[0]

    @pl.when(pl.program_id(0) == 0)
    def _():
        cnt_ref[...] = jnp.zeros(cnt_ref.shape, F32)

    h = x_ref[...] + jnp.dot(a_ref[...], wa_ref[...], preferred_element_type=F32) \
        + jnp.dot(b_ref[...], wb_ref[...], preferred_element_type=F32)
    h_ref[...] = h
    n2 = h * lax.rsqrt(jnp.mean(h * h, axis=-1, keepdims=True) + EPS) * nw_ref[...]
    _store_row_tiles(n2_ref, n2)
    hi = n2.astype(BF16)
    lo = (n2 - hi.astype(F32)).astype(BF16)
    logits = (lax.dot_general(wrh_ref[...], hi, _NT, preferred_element_type=F32)
              + lax.dot_general(wrh_ref[...], lo, _NT, preferred_element_type=F32)
              + lax.dot_general(wrl_ref[...], hi, _NT, preferred_element_type=F32)
              + rb_ref[...])
    e_iota = lax.broadcasted_iota(I32, (ne, tm), 0)
    work = logits
    sels, vals, ids = [], [], []
    for _ in range(TOP_K):
        mx = jnp.max(work, axis=0, keepdims=True)
        idx = jnp.min(jnp.where(work == mx, e_iota, ne), axis=0, keepdims=True)
        sel = e_iota == idx
        work = jnp.where(sel, -jnp.inf, work)
        sels.append(sel)
        vals.append(mx)
        ids.append(idx)
    exps = [jnp.exp(v - vals[0]) for v in vals]
    denom = exps[0] + exps[1] + exps[2] + exps[3]
    chosen = jnp.zeros((ne, tm), F32)
    for sel in sels:
        chosen = jnp.where(sel, 1.0, chosen)
    chosen_bf = chosen.astype(BF16)
    before = cnt_ref[...]
    prior = jnp.dot(chosen_bf, tri_ref[...], preferred_element_type=F32) \
        + jnp.concatenate([before] * (tm // LANES), axis=1)
    cnt_ref[...] = before + jnp.dot(chosen_bf, jnp.ones((tm, LANES), BF16), preferred_element_type=F32)
    for k in range(TOP_K):
        eid_ref[k:k + 1, :] = ids[k]
        rank_ref[k:k + 1, :] = jnp.sum(jnp.where(sels[k], prior, 0.0), axis=0, keepdims=True).astype(I32)
        gate_ref[k:k + 1, :] = exps[k] / denom


def _mix(a, b, x2, w_out_bf, norm_w, router_w, router_b, tm=256):
    n_tok, d = x2.shape
    wa, wb = w_out_bf[:a.shape[1]], w_out_bf[a.shape[1]:]
    ne = router_w.shape[1]
    wr_t = router_w.T
    wr_hi = wr_t.astype(BF16)
    wr_lo = (wr_t - wr_hi.astype(F32)).astype(BF16)
    tri = jnp.asarray(np.triu(np.ones((tm, tm), np.float32), k=1), BF16)
    const = lambda shape: pl.BlockSpec(shape, lambda i: tuple(0 for _ in shape))
    return pl.pallas_call(
        _mix_body,
        grid=(n_tok // tm,),
        in_specs=[pl.BlockSpec((tm, a.shape[1]), lambda i: (i, 0)),
                  pl.BlockSpec((tm, b.shape[1]), lambda i: (i, 0)),
                  pl.BlockSpec((tm, d), lambda i: (i, 0)),
                  const(wa.shape), const(wb.shape), const((1, d)),
                  const((ne, d)), const((ne, d)), const((ne, 1)), const((tm, tm))],
        out_specs=[pl.BlockSpec((tm, d), lambda i: (i, 0)),
                   pl.BlockSpec(_row_tile_shape(tm, d), lambda i: (i, 0, 0, 0)),
                   pl.BlockSpec((TOP_K, tm), lambda i: (0, i)),
                   pl.BlockSpec((TOP_K, tm), lambda i: (0, i)),
                   pl.BlockSpec((TOP_K, tm), lambda i: (0, i)),
                   pl.BlockSpec((ne, LANES), lambda i: (0, 0))],
        out_shape=[jax.ShapeDtypeStruct((n_tok, d), F32),
                   jax.ShapeDtypeStruct(_row_tile_shape(n_tok, d), F32),
                   jax.ShapeDtypeStruct((TOP_K, n_tok), I32),
                   jax.ShapeDtypeStruct((TOP_K, n_tok), I32),
                   jax.ShapeDtypeStruct((TOP_K, n_tok), F32),
                   jax.ShapeDtypeStruct((ne, LANES), F32)],
        compiler_params=_cparams(("arbitrary",)),
        name="mix_router",
    )(a, b, x2, wa, wb, norm_w.reshape(1, d), wr_hi, wr_lo, router_b.reshape(ne, 1), tri)


def _row_of(ref, r):
    return _row_slice(ref, lax.shift_right_logical(r, 3), lax.bitwise_and(r, SUBLANES - 1))


def _dispatch_body(dest_ref, n2_ref, xs_hbm, sem):
    groups = n2_ref.shape[0]
    base = pl.program_id(0) * (groups * SUBLANES * TOP_K)

    def copies(g):
        out = []
        for s in range(SUBLANES):
            for k in range(TOP_K):
                r = dest_ref[base + (g * SUBLANES + s) * TOP_K + k]
                out.append(pltpu.make_async_copy(_row_slice(n2_ref, g, s), _row_of(xs_hbm, r), sem))
        return out

    def issue(g, carry):
        for cp in copies(g):
            cp.start()
        return carry

    lax.fori_loop(0, groups, issue, 0)

    def drain(g, carry):
        for cp in copies(g):
            cp.wait()
        return carry

    lax.fori_loop(0, groups, drain, 0)


def _dispatch(dest_flat, n2_rows, n_rows, td=256):
    width = n2_rows.shape[1] * LANES
    n_tok = n2_rows.shape[0] * SUBLANES
    return pl.pallas_call(
        _dispatch_body,
        grid_spec=pltpu.PrefetchScalarGridSpec(
            num_scalar_prefetch=1,
            grid=(n_tok // td,),
            in_specs=[pl.BlockSpec(_row_tile_shape(td, width), lambda i, dest: (i, 0, 0, 0))],
            out_specs=pl.BlockSpec(memory_space=pl.ANY),
            scratch_shapes=[pltpu.SemaphoreType.DMA(())]),
        out_shape=jax.ShapeDtypeStruct(_row_tile_shape(n_rows, width), F32),
        compiler_params=_cparams(("arbitrary",)),
        name="dispatch",
    )(dest_flat, n2_rows)


FFN_CAST_ROWS = 64


def _ffn_body(be_ref, first_ref, slot_ref, next_ref, nact_ref, xs_ref, w1_hbm, b1_ref, w2_hbm, b2_ref, ys_ref,
              w1_f, w2_f, w1_b, w2_b, sem):
    i = pl.program_id(0)

    def fetch(expert, slot):
        return (pltpu.make_async_copy(w1_hbm.at[expert], w1_f.at[slot], sem.at[0, slot]),
                pltpu.make_async_copy(w2_hbm.at[expert], w2_f.at[slot], sem.at[1, slot]))

    @pl.when(i < nact_ref[0])
    def _():
        slot = slot_ref[i]

        @pl.when(first_ref[i] == 1)
        def _():
            @pl.when(i == 0)
            def _():
                for cp in fetch(be_ref[0], slot):
                    cp.start()

            for cp in fetch(be_ref[i], slot):
                cp.wait()

            @pl.when(next_ref[i] >= 0)
            def _():
                for cp in fetch(next_ref[i], 1 - slot):
                    cp.start()

            def cast(src, dst):
                def body(c, carry):
                    rows = pl.ds(pl.multiple_of(c * FFN_CAST_ROWS, FFN_CAST_ROWS), FFN_CAST_ROWS)
                    dst[rows, :] = src[slot, rows, :].astype(BF16)
                    return carry
                lax.fori_loop(0, dst.shape[0] // FFN_CAST_ROWS, body, 0)

            cast(w1_f, w1_b)
            cast(w2_f, w2_b)

        x = _load_row_tiles(xs_ref, BF16)
        hdn = jnp.dot(x, w1_b[...], preferred_element_type=F32) + b1_ref[0]
        d_ff = hdn.shape[1] // 2
        glu = jnp.minimum(hdn[:, :d_ff], SWIGLU_LIMIT)
        lin = jnp.clip(hdn[:, d_ff:], -SWIGLU_LIMIT, SWIGLU_LIMIT)
        act = glu * _sigmoid(SWIGLU_ALPHA * glu) * (lin + 1.0)
        y = jnp.dot(act.astype(BF16), w2_b[...], preferred_element_type=F32) + b2_ref[0]
        _store_row_tiles(ys_ref, y)


def _ffn(block_expert, n_active, xs, w1, b1, w2, b2):
    ne, d, f2 = w1.shape
    nb = xs.shape[0] * SUBLANES // MOE_BLOCK
    block = _row_tile_shape(MOE_BLOCK, d)
    idx = jnp.arange(nb, dtype=I32)
    active = idx < n_active[0]
    prev = jnp.concatenate([block_expert[:1] - 1, block_expert[:-1]])
    first = (active & (block_expert != prev)).astype(I32)
    run_slot = (jnp.sum(jnp.where(idx[None, :] <= idx[:, None], first[None, :], 0), axis=1) - 1) & 1
    later_first = (first[None, :] == 1) & (idx[None, :] > idx[:, None])
    next_block = jnp.min(jnp.where(later_first, idx[None, :], nb), axis=1)
    next_expert = jnp.where(next_block < nb, block_expert[jnp.minimum(next_block, nb - 1)], -1).astype(I32)

    def blk(i, be, fi, sl, nx, nact):
        return (jnp.minimum(i, nact[0] - 1), 0, 0, 0)

    def exp(i, be, fi, sl, nx, nact):
        return (be[jnp.minimum(i, nact[0] - 1)], 0, 0)

    return pl.pallas_call(
        _ffn_body,
        grid_spec=pltpu.PrefetchScalarGridSpec(
            num_scalar_prefetch=5,
            grid=(nb,),
            in_specs=[pl.BlockSpec(block, blk),
                      pl.BlockSpec(memory_space=pl.ANY),
                      pl.BlockSpec((1, 1, f2), exp),
                      pl.BlockSpec(memory_space=pl.ANY),
                      pl.BlockSpec((1, 1, d), exp)],
            out_specs=pl.BlockSpec(block, blk),
            scratch_shapes=[pltpu.VMEM((2, d, f2), F32), pltpu.VMEM((2, f2 // 2, d), F32),
                            pltpu.VMEM((d, f2), BF16), pltpu.VMEM((f2 // 2, d), BF16),
                            pltpu.SemaphoreType.DMA((2, 2))]),
        out_shape=jax.ShapeDtypeStruct(xs.shape, F32),
        compiler_params=_cparams(("arbitrary",)),
        name="expert_ffn",
    )(block_expert, first, run_slot.astype(I32), next_expert, n_active, xs, w1, b1.reshape(ne, 1, f2), w2,
      b2.reshape(ne, 1, d))


def _combine_body(dest_ref, h_ref, gate_ref, nw_ref, ys_hbm, o_ref, buf, sem):
    step = pl.program_id(0)
    groups, lane_groups = buf.shape[2], buf.shape[3]
    tc = groups * SUBLANES

    def issue(at_step, slot):
        base = at_step * (tc * TOP_K)

        def body(g, carry):
            for s in range(SUBLANES):
                for k in range(TOP_K):
                    r = dest_ref[base + (g * SUBLANES + s) * TOP_K + k]
                    pltpu.make_async_copy(_row_of(ys_hbm, r), _row_slice(buf.at[slot, k], g, s),
                                          sem.at[slot]).start()
            return carry

        lax.fori_loop(0, groups, body, 0)

    def drain(slot):
        def body(g, carry):
            for s in range(SUBLANES):
                for k in range(TOP_K):
                    pltpu.make_async_copy(_row_slice(ys_hbm, 0, 0), _row_slice(buf.at[slot, k], g, s),
                                          sem.at[slot]).wait()
            return carry

        lax.fori_loop(0, groups, body, 0)

    @pl.when(step == 0)
    def _():
        issue(0, 0)

    @pl.when(step + 1 < pl.num_programs(0))
    def _():
        issue(step + 1, (step + 1) % 2)

    slot = step % 2
    drain(slot)
    gates = gate_ref[...]
    cols = []
    for j in range(lane_groups):
        acc = gates[:, 0:1] * buf[slot, 0, :, j].reshape(tc, LANES)
        for k in range(1, TOP_K):
            acc = acc + gates[:, k:k + 1] * buf[slot, k, :, j].reshape(tc, LANES)
        cols.append(acc)
    y = h_ref[...] + jnp.concatenate(cols, axis=1)
    o_ref[...] = y * lax.rsqrt(jnp.mean(y * y, axis=-1, keepdims=True) + EPS) * nw_ref[...]


def _combine(dest_flat, h, gates_tk, norm_w, ys, tc=128):
    n_tok, d = h.shape
    return pl.pallas_call(
        _combine_body,
        grid_spec=pltpu.PrefetchScalarGridSpec(
            num_scalar_prefetch=1,
            grid=(n_tok // tc,),
            in_specs=[pl.BlockSpec((tc, d), lambda i, dest: (i, 0)),
                      pl.BlockSpec((tc, TOP_K), lambda i, dest: (i, 0)),
                      pl.BlockSpec((1, d), lambda i, dest: (0, 0)),
                      pl.BlockSpec(memory_space=pl.ANY)],
            out_specs=pl.BlockSpec((tc, d), lambda i, dest: (i, 0)),
            scratch_shapes=[pltpu.VMEM((2, TOP_K) + _row_tile_shape(tc, d), F32),
                            pltpu.SemaphoreType.DMA((2,))]),
        out_shape=jax.ShapeDtypeStruct((n_tok, d), F32),
        compiler_params=_cparams(("arbitrary",)),
        name="combine_norm",
    )(dest_flat, h, gates_tk, norm_w.reshape(1, d), ys)


def kernel(x, norm1_w, w_in, lb_logits, hgrn_norm_w, dw_w, dw_b, conv_ln_w, conv_ln_b, w_out, norm2_w,
           router_w, router_b, w1, b1, w2, b2, final_norm_w):
    batch, t_len, d = x.shape
    assert w_in.shape[0] == 1, "single-layer block"
    n_tok = batch * t_len
    hk = HGRN_HEADS * HEAD_DIM
    conv_ch = dw_w.shape[2]
    lb_table = jnp.cumsum(jax.nn.softmax(lb_logits.astype(F32), axis=1), axis=1)
    x2 = x.reshape(n_tok, d)
    proj = _in_proj(x2, norm1_w[0], w_in[0].astype(BF16))
    a = _hgrn(proj, lb_table[0, 0], lb_table[1, 0], hgrn_norm_w[0], batch, t_len)
    b = _conv(proj, dw_w[0], dw_b[0], conv_ln_w[0], conv_ln_b[0], batch, t_len, (5 * hk) // conv_ch)
    h_mid, n2_rows, eid, rank, gate, cnt = _mix(a, b, x2, w_out[0].astype(BF16), norm2_w[0],
                                                router_w[0], router_b[0])
    counts = cnt[:, 0].astype(I32)
    padded = ((counts + MOE_BLOCK - 1) // MOE_BLOCK) * MOE_BLOCK
    e_ids = jnp.arange(N_EXPERTS, dtype=I32)
    pad_end = jnp.sum(jnp.where(e_ids[None, :] <= e_ids[:, None], padded[None, :], 0), axis=1)
    pad_start = pad_end - padded
    n_blocks = -(-(n_tok * TOP_K) // MOE_BLOCK) + N_EXPERTS
    block_start = jnp.arange(n_blocks, dtype=I32) * MOE_BLOCK
    block_expert = jnp.minimum(jnp.sum((pad_end[None, :] <= block_start[:, None]).astype(I32), axis=1),
                               N_EXPERTS - 1)
    n_active = pad_end[-1:] // MOE_BLOCK
    start_of = jnp.sum(jnp.where(eid[:, :, None] == e_ids, pad_start, 0), axis=-1)
    dest_flat = (start_of + rank).T.reshape(-1)
    xs = _dispatch(dest_flat, n2_rows, n_blocks * MOE_BLOCK)
    ys = _ffn(block_expert, n_active, xs, w1[0], b1[0], w2[0], b2[0])
    out = _combine(dest_flat, h_mid, gate.T, final_norm_w, ys)
    return out.reshape(batch, t_len, d)
```

```python
import functools
import math

import numpy as np
import jax
import jax.numpy as jnp
from jax import lax
from jax.experimental import pallas as pl
from jax.experimental.pallas import tpu as pltpu

F32 = jnp.float32
BF16 = jnp.bfloat16
I32 = jnp.int32

EPS = 1e-5
HGRN_HEADS = 4
HEAD_DIM = 128
HGRN_CHUNK = 128
CONV_LEN = 31
N_EXPERTS = 32
TOP_K = 4
SWIGLU_LIMIT = 7.0
SWIGLU_ALPHA = 1.702
MOE_BLOCK = 256
LANES = 128
SUBLANES = 8
VMEM_LIMIT = 56 << 20
DMA_QUEUES = 2

_NT = (((1,), (1,)), ((), ()))


def _sigmoid(x):
    return 0.5 * jnp.tanh(0.5 * x) + 0.5


def _silu(x):
    return x * _sigmoid(x)


def _cparams(sem):
    return pltpu.CompilerParams(dimension_semantics=sem, vmem_limit_bytes=VMEM_LIMIT)


def _row_tile_shape(rows, width):
    return (rows // SUBLANES, width // LANES, SUBLANES, LANES)


def _load_row_tiles(ref, dtype=None):
    groups, lane_groups = ref.shape[0], ref.shape[1]
    cols = [ref[:, j].reshape(groups * SUBLANES, LANES) for j in range(lane_groups)]
    if dtype is not None:
        cols = [c.astype(dtype) for c in cols]
    return jnp.concatenate(cols, axis=1)


def _store_row_tiles(ref, value):
    groups, lane_groups = ref.shape[0], ref.shape[1]
    for j in range(lane_groups):
        ref[:, j] = value[:, j * LANES:(j + 1) * LANES].reshape(groups, SUBLANES, LANES)


def _row_slice(ref, group, sub):
    return ref.at[group, :, pl.ds(sub, 1), :]


def _inproj_body(x_ref, nw_ref, w_ref, o_ref):
    x = x_ref[...]
    n = x * lax.rsqrt(jnp.mean(x * x, axis=-1, keepdims=True) + EPS) * nw_ref[...]
    o_ref[...] = jnp.dot(n.astype(BF16), w_ref[...], preferred_element_type=F32)


def _in_proj(x2, norm_w, w_bf, tm=512):
    n_tok, d = x2.shape
    cols = w_bf.shape[1]
    return pl.pallas_call(
        _inproj_body,
        grid=(n_tok // tm,),
        in_specs=[pl.BlockSpec((tm, d), lambda i: (i, 0)),
                  pl.BlockSpec((1, d), lambda i: (0, 0)),
                  pl.BlockSpec((d, cols), lambda i: (0, 0))],
        out_specs=pl.BlockSpec((tm, cols), lambda i: (i, 0)),
        out_shape=jax.ShapeDtypeStruct((n_tok, cols), F32),
        compiler_params=_cparams(("parallel",)),
        name="in_proj",
    )(x2, norm_w.reshape(1, d), w_bf)


def _hgrn_levels(c):
    return [c >> (i + 1) for i in range(int(math.log2(c)))]


def _hgrn_sum_matrices(c, fwd):
    r = np.arange(c)
    m = _HGRN_SMALL_LEVEL
    small = np.zeros((c, c), np.float32)
    for t in range(c):
        p0 = t & ~(2 * m - 1)
        upper = (t & m) != 0
        if fwd:
            if upper:
                small[t, p0 + m:t + 1] = 1.0
            else:
                small[t, t + 1:p0 + m] = 1.0
        else:
            if upper:
                small[t, p0 + m:t] = 1.0
            else:
                small[t, t:p0 + m] = 1.0
    if fwd:
        mats = [r[None, :] <= r[:, None], r[None, :] > r[:, None], small]
    else:
        mats = [r[None, :] >= r[:, None], r[None, :] < r[:, None], small]
    return np.concatenate([np.asarray(x, np.float32) for x in mats], axis=0)


_HGRN_SMALL_LEVEL = 2


def _hgrn_chunk(q, k, lf, v_bf, vt_bf, st, sums_ref, fwd, c):
    dk = q.shape[1]
    hi = lf.astype(BF16)
    lo = (lf - hi.astype(F32)).astype(BF16)
    both = jnp.dot(sums_ref[...], jnp.concatenate([hi, lo], axis=1), preferred_element_type=F32)
    sums = both[:, :dk] + both[:, dk:]
    b = sums[0:c]
    d_out = sums[c:2 * c]
    row = lax.broadcasted_iota(I32, (c, dk), 0)
    rr = lax.broadcasted_iota(I32, (c, c), 0)
    cc = lax.broadcasted_iota(I32, (c, c), 1)
    sep = rr ^ cc
    a = None
    for m in _hgrn_levels(c):
        upper = (row & m) != 0
        is_q = upper if fwd else jnp.logical_not(upper)
        if m == 1:
            d = jnp.where(is_q, lf, 0.0)
        elif m == _HGRN_SMALL_LEVEL:
            d = sums[2 * c:3 * c]
        else:
            blocks = c // (2 * m)
            at = m - 1 if fwd else m
            edge = b.reshape(blocks, 2 * m, dk)[:, at:at + 1, :]
            edge = jnp.broadcast_to(edge, (blocks, 2 * m, dk)).reshape(c, dk)
            d = jnp.where(is_q, b - edge, edge - b)
        r = (jnp.where(is_q, q, k) * jnp.exp2(d)).astype(BF16)
        a_l = lax.dot_general(r, r, _NT, preferred_element_type=F32)
        a = a_l if a is None else jnp.where(sep < 2 * m, a_l, a)
    a_diag = lax.dot_general(q.astype(BF16), k.astype(BF16), _NT, preferred_element_type=F32)
    a = jnp.where(sep < 1, a_diag, a)
    a = jnp.where(rr >= cc if fwd else rr <= cc, a, 0.0)
    e_b = jnp.exp2(b)
    q_in = (q * e_b).astype(BF16)
    k_out = (k * jnp.exp2(d_out)).astype(BF16)
    o = jnp.dot(a.astype(BF16), v_bf, preferred_element_type=F32)
    o = o + lax.dot_general(q_in, st.astype(BF16), _NT, preferred_element_type=F32)
    edge_row = c - 1 if fwd else 0
    st_new = st * e_b[edge_row:edge_row + 1] + jnp.dot(vt_bf, k_out, preferred_element_type=F32)
    return o, st_new


def _hgrn_body(qz_ref, ff_ref, fb_ref, iv_ref, gz_ref, lbf_ref, lbb_ref, nw_ref, sf_ref, sb_ref, o_ref,
               q_s, kf_s, lff_s, kb_s, lfb_s, v_s, vt_s, acc_s, *, c):
    t_len, dk = qz_ref.shape
    nch = t_len // c
    q_s[...] = _silu(qz_ref[...])

    def gate(fz_ref, lb_ref, k_s, lf_s):
        lb = lb_ref[0]
        f = lb + (1.0 - lb) * _sigmoid(fz_ref[...])
        lf_s[...] = jnp.log2(f)
        k_s[...] = 1.0 - f

    gate(ff_ref, lbf_ref, kf_s, lff_s)
    gate(fb_ref, lbb_ref, kb_s, lfb_s)
    v = iv_ref[...]
    v_s[...] = v.astype(BF16)
    for ci in range(nch):
        vt_s[ci] = v[ci * c:(ci + 1) * c, :].T.astype(BF16)

    def run(k_s, lf_s, sums_ref, fwd):
        def step(i, st):
            ci = i if fwd else nch - 1 - i
            rows = pl.ds(pl.multiple_of(ci * c, c), c)
            o, st = _hgrn_chunk(q_s[rows, :], k_s[rows, :], lf_s[rows, :], v_s[rows, :], vt_s[ci],
                                st, sums_ref, fwd, c)
            if fwd:
                acc_s[rows, :] = o
            else:
                acc_s[rows, :] += o
            return st
        lax.fori_loop(0, nch, step, jnp.zeros((dk, dk), F32), unroll=4)

    run(kf_s, lff_s, sf_ref, True)
    run(kb_s, lfb_s, sb_ref, False)
    o = acc_s[...]
    o = o * lax.rsqrt(jnp.mean(o * o, axis=-1, keepdims=True) + EPS) * nw_ref[0]
    o_ref[...] = (o * _silu(gz_ref[...])).astype(o_ref.dtype)


def _hgrn(proj, lb_f, lb_b, norm_w, batch, t_len):
    h, dk, c = HGRN_HEADS, HEAD_DIM, HGRN_CHUNK
    nrows = 3
    sums_f = jnp.asarray(_hgrn_sum_matrices(c, True), BF16)
    sums_b = jnp.asarray(_hgrn_sum_matrices(c, False), BF16)

    def col(group):
        return pl.BlockSpec((t_len, dk), lambda b, hh: (b, group * h + hh))

    def per_head():
        return pl.BlockSpec((1, 1, dk), lambda b, hh: (hh, 0, 0))

    const = pl.BlockSpec((nrows * c, c), lambda b, hh: (0, 0))
    seq = lambda dt: pltpu.VMEM((t_len, dk), dt)
    return pl.pallas_call(
        functools.partial(_hgrn_body, c=c),
        grid=(batch, h),
        in_specs=[col(0), col(1), col(2), col(3), col(4), per_head(), per_head(), per_head(), const, const],
        out_specs=pl.BlockSpec((t_len, dk), lambda b, hh: (b, hh)),
        out_shape=jax.ShapeDtypeStruct((batch * t_len, h * dk), BF16),
        scratch_shapes=[seq(F32), seq(F32), seq(F32), seq(F32), seq(F32), seq(BF16),
                        pltpu.VMEM((t_len // c, dk, c), BF16), seq(F32)],
        compiler_params=_cparams(("parallel", "parallel")),
        name="hgrn",
    )(proj, proj, proj, proj, proj, lb_f.reshape(h, 1, dk), lb_b.reshape(h, 1, dk),
      norm_w.reshape(h, 1, dk), sums_f, sums_b)


CONV_ROWS = 64
CONV_HALO = 16


def _conv_body(cv_ref, cg_ref, w_ref, b_ref, lnw_ref, lnb_ref, o_ref, u_s):
    t_len, ch = cv_ref.shape
    halo, rows = CONV_HALO, CONV_ROWS
    shift0 = halo - (CONV_LEN - 1) // 2
    win = rows + 2 * halo
    u_s[0:halo, :] = jnp.zeros((halo, ch), F32)
    u_s[halo + t_len:, :] = jnp.zeros((halo, ch), F32)
    u_s[halo:halo + t_len, :] = cv_ref[...] * _sigmoid(cg_ref[...])

    def step(i, carry):
        t0 = pl.multiple_of(i * rows, rows)
        groups = []
        for g in range(ch // LANES):
            lanes = slice(g * LANES, (g + 1) * LANES)
            window = u_s[pl.ds(t0, win), lanes]
            acc = jnp.zeros((rows, LANES), F32)
            for s in range(SUBLANES):
                shifted = window if s == 0 else pltpu.roll(window, win - s, 0)
                for j in range(CONV_LEN):
                    if (j + shift0) % SUBLANES == s:
                        off = j + shift0 - s
                        acc = acc + w_ref[j:j + 1, lanes] * shifted[off:off + rows]
            groups.append(acc + b_ref[:, lanes])
        y = jnp.concatenate(groups, axis=1)
        mu = jnp.mean(y, axis=-1, keepdims=True)
        yc = y - mu
        var = jnp.mean(yc * yc, axis=-1, keepdims=True)
        z = yc * lax.rsqrt(var + EPS) * lnw_ref[...] + lnb_ref[...]
        o_ref[pl.ds(t0, rows), :] = _silu(z).astype(o_ref.dtype)
        return carry

    lax.fori_loop(0, t_len // rows, step, 0)


def _conv(proj, dw_w, dw_b, ln_w, ln_b, batch, t_len, first_col_block):
    ch = dw_w.shape[1]
    vec = pl.BlockSpec((1, ch), lambda b: (0, 0))
    return pl.pallas_call(
        _conv_body,
        grid=(batch,),
        in_specs=[pl.BlockSpec((t_len, ch), lambda b: (b, first_col_block)),
                  pl.BlockSpec((t_len, ch), lambda b: (b, first_col_block + 1)),
                  pl.BlockSpec((CONV_LEN, ch), lambda b: (0, 0)), vec, vec, vec],
        out_specs=pl.BlockSpec((t_len, ch), lambda b: (b, 0)),
        out_shape=jax.ShapeDtypeStruct((batch * t_len, ch), BF16),
        scratch_shapes=[pltpu.VMEM((t_len + 2 * CONV_HALO, ch), F32)],
        compiler_params=_cparams(("parallel",)),
        name="conv",
    )(proj, proj, dw_w, dw_b.reshape(1, ch), ln_w.reshape(1, ch), ln_b.reshape(1, ch))


def _mix_body(a_ref, b_ref, x_ref, wa_ref, wb_ref, nw_ref, wrh_ref, wrl_ref, rb_ref, tri_ref,
              h_ref, n2_ref, eid_ref, rank_ref, gate_ref, cnt_ref):
    tm = x_ref.shape[0]
    ne = wrh_ref.shape[0]

    @pl.when(pl.program_id(0) == 0)
    def _():
        cnt_ref[...] = jnp.zeros(cnt_ref.shape, F32)

    h = x_ref[...] + jnp.dot(a_ref[...], wa_ref[...], preferred_element_type=F32) \
        + jnp.dot(b_ref[...], wb_ref[...], preferred_element_type=F32)
    h_ref[...] = h
    n2 = h * lax.rsqrt(jnp.mean(h * h, axis=-1, keepdims=True) + EPS) * nw_ref[...]
    _store_row_tiles(n2_ref, n2)
    hi = n2.astype(BF16)
    lo = (n2 - hi.astype(F32)).astype(BF16)
    logits = (lax.dot_general(wrh_ref[...], hi, _NT, preferred_element_type=F32)
              + lax.dot_general(wrh_ref[...], lo, _NT, preferred_element_type=F32)
              + lax.dot_general(wrl_ref[...], hi, _NT, preferred_element_type=F32)
              + rb_ref[...])
    e_iota = lax.broadcasted_iota(I32, (ne, tm), 0)
    work = logits
    sels, vals, ids = [], [], []
    for _ in range(TOP_K):
        mx = jnp.max(work, axis=0, keepdims=True)
        idx = jnp.min(jnp.where(work == mx, e_iota, ne), axis=0, keepdims=True)
        sel = e_iota == idx
        work = jnp.where(sel, -jnp.inf, work)
        sels.append(sel)
        vals.append(mx)
        ids.append(idx)
    exps = [jnp.exp(v - vals[0]) for v in vals]
    denom = exps[0] + exps[1] + exps[2] + exps[3]
    chosen = jnp.zeros((ne, tm), F32)
    for sel in sels:
        chosen = jnp.where(sel, 1.0, chosen)
    chosen_bf = chosen.astype(BF16)
    before = cnt_ref[...]
    prior = jnp.dot(chosen_bf, tri_ref[...], preferred_element_type=F32) \
        + jnp.concatenate([before] * (tm // LANES), axis=1)
    cnt_ref[...] = before + jnp.dot(chosen_bf, jnp.ones((tm, LANES), BF16), preferred_element_type=F32)
    for k in range(TOP_K):
        eid_ref[k:k + 1, :] = ids[k]
        rank_ref[k:k + 1, :] = jnp.sum(jnp.where(sels[k], prior, 0.0), axis=0, keepdims=True).astype(I32)
        gate_ref[k:k + 1, :] = exps[k] / denom


def _mix(a, b, x2, w_out_bf, norm_w, router_w, router_b, tm=512):
    n_tok, d = x2.shape
    wa, wb = w_out_bf[:a.shape[1]], w_out_bf[a.shape[1]:]
    ne = router_w.shape[1]
    wr_t = router_w.T
    wr_hi = wr_t.astype(BF16)
    wr_lo = (wr_t - wr_hi.astype(F32)).astype(BF16)
    tri = jnp.asarray(np.triu(np.ones((tm, tm), np.float32), k=1), BF16)
    const = lambda shape: pl.BlockSpec(shape, lambda i: tuple(0 for _ in shape))
    return pl.pallas_call(
        _mix_body,
        grid=(n_tok // tm,),
        in_specs=[pl.BlockSpec((tm, a.shape[1]), lambda i: (i, 0)),
                  pl.BlockSpec((tm, b.shape[1]), lambda i: (i, 0)),
                  pl.BlockSpec((tm, d), lambda i: (i, 0)),
                  const(wa.shape), const(wb.shape), const((1, d)),
                  const((ne, d)), const((ne, d)), const((ne, 1)), const((tm, tm))],
        out_specs=[pl.BlockSpec((tm, d), lambda i: (i, 0)),
                   pl.BlockSpec(_row_tile_shape(tm, d), lambda i: (i, 0, 0, 0)),
                   pl.BlockSpec((TOP_K, tm), lambda i: (0, i)),
                   pl.BlockSpec((TOP_K, tm), lambda i: (0, i)),
                   pl.BlockSpec((TOP_K, tm), lambda i: (0, i)),
                   pl.BlockSpec((ne, LANES), lambda i: (0, 0))],
        out_shape=[jax.ShapeDtypeStruct((n_tok, d), F32),
                   jax.ShapeDtypeStruct(_row_tile_shape(n_tok, d), F32),
                   jax.ShapeDtypeStruct((TOP_K, n_tok), I32),
                   jax.ShapeDtypeStruct((TOP_K, n_tok), I32),
                   jax.ShapeDtypeStruct((TOP_K, n_tok), F32),
                   jax.ShapeDtypeStruct((ne, LANES), F32)],
        compiler_params=_cparams(("arbitrary",)),
        name="mix_router",
    )(a, b, x2, wa, wb, norm_w.reshape(1, d), wr_hi, wr_lo, router_b.reshape(ne, 1), tri)


def _row_of(ref, r):
    return _row_slice(ref, lax.shift_right_logical(r, 3), lax.bitwise_and(r, SUBLANES - 1))


def _dispatch_body(dest_ref, n2_ref, xs_hbm, sem):
    groups = n2_ref.shape[0]
    base = pl.program_id(0) * (groups * SUBLANES * TOP_K)

    def copies(g):
        out = []
        for s in range(SUBLANES):
            for k in range(TOP_K):
                r = dest_ref[base + (g * SUBLANES + s) * TOP_K + k]
                out.append(pltpu.make_async_copy(_row_slice(n2_ref, g, s), _row_of(xs_hbm, r), sem))
        return out

    def issue(g, carry):
        for n, cp in enumerate(copies(g)):
            cp.start(priority=n % DMA_QUEUES)
        return carry

    lax.fori_loop(0, groups, issue, 0)

    def drain(g, carry):
        for cp in copies(g):
            cp.wait()
        return carry

    lax.fori_loop(0, groups, drain, 0)


def _dispatch(dest_flat, n2_rows, n_rows, td=256):
    width = n2_rows.shape[1] * LANES
    n_tok = n2_rows.shape[0] * SUBLANES
    return pl.pallas_call(
        _dispatch_body,
        grid_spec=pltpu.PrefetchScalarGridSpec(
            num_scalar_prefetch=1,
            grid=(n_tok // td,),
            in_specs=[pl.BlockSpec(_row_tile_shape(td, width), lambda i, dest: (i, 0, 0, 0))],
            out_specs=pl.BlockSpec(memory_space=pl.ANY),
            scratch_shapes=[pltpu.SemaphoreType.DMA(())]),
        out_shape=jax.ShapeDtypeStruct(_row_tile_shape(n_rows, width), F32),
        compiler_params=_cparams(("arbitrary",)),
        name="dispatch",
    )(dest_flat, n2_rows)


FFN_CAST_ROWS = 64


def _ffn_body(be_ref, first_ref, slot_ref, next_ref, nact_ref, xs_ref, w1_hbm, b1_ref, w2_hbm, b2_ref, ys_ref,
              w1_f, w2_f, w1_b, w2_b, sem):
    i = pl.program_id(0)

    def fetch(expert, slot):
        return (pltpu.make_async_copy(w1_hbm.at[expert], w1_f.at[slot], sem.at[0, slot]),
                pltpu.make_async_copy(w2_hbm.at[expert], w2_f.at[slot], sem.at[1, slot]))

    @pl.when(i < nact_ref[0])
    def _():
        slot = slot_ref[i]

        @pl.when(first_ref[i] == 1)
        def _():
            @pl.when(i == 0)
            def _():
                for cp in fetch(be_ref[0], slot):
                    cp.start()

            for cp in fetch(be_ref[i], slot):
                cp.wait()

            @pl.when(next_ref[i] >= 0)
            def _():
                for cp in fetch(next_ref[i], 1 - slot):
                    cp.start()

            def cast(src, dst):
                def body(c, carry):
                    rows = pl.ds(pl.multiple_of(c * FFN_CAST_ROWS, FFN_CAST_ROWS), FFN_CAST_ROWS)
                    dst[rows, :] = src[slot, rows, :].astype(BF16)
                    return carry
                lax.fori_loop(0, dst.shape[0] // FFN_CAST_ROWS, body, 0)

            cast(w1_f, w1_b)
            cast(w2_f, w2_b)

        x = _load_row_tiles(xs_ref, BF16)
        hdn = jnp.dot(x, w1_b[...], preferred_element_type=F32) + b1_ref[0]
        d_ff = hdn.shape[1] // 2
        glu = jnp.minimum(hdn[:, :d_ff], SWIGLU_LIMIT)
        lin = jnp.clip(hdn[:, d_ff:], -SWIGLU_LIMIT, SWIGLU_LIMIT)
        act = glu * _sigmoid(SWIGLU_ALPHA * glu) * (lin + 1.0)
        y = jnp.dot(act.astype(BF16), w2_b[...], preferred_element_type=F32) + b2_ref[0]
        _store_row_tiles(ys_ref, y)


def _ffn(block_expert, n_active, xs, w1, b1, w2, b2):
    ne, d, f2 = w1.shape
    nb = xs.shape[0] * SUBLANES // MOE_BLOCK
    block = _row_tile_shape(MOE_BLOCK, d)
    idx = jnp.arange(nb, dtype=I32)
    active = idx < n_active[0]
    prev = jnp.concatenate([block_expert[:1] - 1, block_expert[:-1]])
    first = (active & (block_expert != prev)).astype(I32)
    run_slot = (jnp.sum(jnp.where(idx[None, :] <= idx[:, None], first[None, :], 0), axis=1) - 1) & 1
    later_first = (first[None, :] == 1) & (idx[None, :] > idx[:, None])
    next_block = jnp.min(jnp.where(later_first, idx[None, :], nb), axis=1)
    next_expert = jnp.where(next_block < nb, block_expert[jnp.minimum(next_block, nb - 1)], -1).astype(I32)

    def blk(i, be, fi, sl, nx, nact):
        return (jnp.minimum(i, nact[0] - 1), 0, 0, 0)

    def exp(i, be, fi, sl, nx, nact):
        return (be[jnp.minimum(i, nact[0] - 1)], 0, 0)

    return pl.pallas_call(
        _ffn_body,
        grid_spec=pltpu.PrefetchScalarGridSpec(
            num_scalar_prefetch=5,
            grid=(nb,),
            in_specs=[pl.BlockSpec(block, blk),
                      pl.BlockSpec(memory_space=pl.ANY),
                      pl.BlockSpec((1, 1, f2), exp),
                      pl.BlockSpec(memory_space=pl.ANY),
                      pl.BlockSpec((1, 1, d), exp)],
            out_specs=pl.BlockSpec(block, blk),
            scratch_shapes=[pltpu.VMEM((2, d, f2), F32), pltpu.VMEM((2, f2 // 2, d), F32),
                            pltpu.VMEM((d, f2), BF16), pltpu.VMEM((f2 // 2, d), BF16),
                            pltpu.SemaphoreType.DMA((2, 2))]),
        out_shape=jax.ShapeDtypeStruct(xs.shape, F32),
        compiler_params=_cparams(("arbitrary",)),
        name="expert_ffn",
    )(block_expert, first, run_slot.astype(I32), next_expert, n_active, xs, w1, b1.reshape(ne, 1, f2), w2,
      b2.reshape(ne, 1, d))


def _combine_body(dest_ref, h_ref, gate_ref, nw_ref, ys_hbm, o_ref, buf, sem):
    step = pl.program_id(0)
    groups, lane_groups = buf.shape[2], buf.shape[3]
    tc = groups * SUBLANES

    def issue(at_step, slot):
        base = at_step * (tc * TOP_K)

        def body(g, carry):
            for s in range(SUBLANES):
                for k in range(TOP_K):
                    r = dest_ref[base + (g * SUBLANES + s) * TOP_K + k]
                    pltpu.make_async_copy(_row_of(ys_hbm, r), _row_slice(buf.at[slot, k], g, s),
                                          sem.at[slot]).start(priority=k % DMA_QUEUES)
            return carry

        lax.fori_loop(0, groups, body, 0)

    def drain(slot):
        def body(g, carry):
            for s in range(SUBLANES):
                for k in range(TOP_K):
                    pltpu.make_async_copy(_row_slice(ys_hbm, 0, 0), _row_slice(buf.at[slot, k], g, s),
                                          sem.at[slot]).wait()
            return carry

        lax.fori_loop(0, groups, body, 0)

    @pl.when(step == 0)
    def _():
        issue(0, 0)

    @pl.when(step + 1 < pl.num_programs(0))
    def _():
        issue(step + 1, (step + 1) % 2)

    slot = step % 2
    drain(slot)
    gates = gate_ref[...]
    cols = []
    for j in range(lane_groups):
        acc = gates[:, 0:1] * buf[slot, 0, :, j].reshape(tc, LANES)
        for k in range(1, TOP_K):
            acc = acc + gates[:, k:k + 1] * buf[slot, k, :, j].reshape(tc, LANES)
        cols.append(acc)
    y = h_ref[...] + jnp.concatenate(cols, axis=1)
    o_ref[...] = y * lax.rsqrt(jnp.mean(y * y, axis=-1, keepdims=True) + EPS) * nw_ref[...]


def _combine(dest_flat, h, gates_tk, norm_w, ys, tc=128):
    n_tok, d = h.shape
    return pl.pallas_call(
        _combine_body,
        grid_spec=pltpu.PrefetchScalarGridSpec(
            num_scalar_prefetch=1,
            grid=(n_tok // tc,),
            in_specs=[pl.BlockSpec((tc, d), lambda i, dest: (i, 0)),
                      pl.BlockSpec((tc, TOP_K), lambda i, dest: (i, 0)),
                      pl.BlockSpec((1, d), lambda i, dest: (0, 0)),
                      pl.BlockSpec(memory_space=pl.ANY)],
            out_specs=pl.BlockSpec((tc, d), lambda i, dest: (i, 0)),
            scratch_shapes=[pltpu.VMEM((2, TOP_K) + _row_tile_shape(tc, d), F32),
                            pltpu.SemaphoreType.DMA((2,))]),
        out_shape=jax.ShapeDtypeStruct((n_tok, d), F32),
        compiler_params=_cparams(("arbitrary",)),
        name="combine_norm",
    )(dest_flat, h, gates_tk, norm_w.reshape(1, d), ys)


def kernel(x, norm1_w, w_in, lb_logits, hgrn_norm_w, dw_w, dw_b, conv_ln_w, conv_ln_b, w_out, norm2_w,
           router_w, router_b, w1, b1, w2, b2, final_norm_w):
    batch, t_len, d = x.shape
    assert w_in.shape[0] == 1, "single-layer block"
    n_tok = batch * t_len
    hk = HGRN_HEADS * HEAD_DIM
    conv_ch = dw_w.shape[2]
    lb_table = jnp.cumsum(jax.nn.softmax(lb_logits.astype(F32), axis=1), axis=1)
    x2 = x.reshape(n_tok, d)
    proj = _in_proj(x2, norm1_w[0], w_in[0].astype(BF16))
    a = _hgrn(proj, lb_table[0, 0], lb_table[1, 0], hgrn_norm_w[0], batch, t_len)
    b = _conv(proj, dw_w[0], dw_b[0], conv_ln_w[0], conv_ln_b[0], batch, t_len, (5 * hk) // conv_ch)
    h_mid, n2_rows, eid, rank, gate, cnt = _mix(a, b, x2, w_out[0].astype(BF16), norm2_w[0],
                                                router_w[0], router_b[0])
    counts = cnt[:, 0].astype(I32)
    padded = ((counts + MOE_BLOCK - 1) // MOE_BLOCK) * MOE_BLOCK
    e_ids = jnp.arange(N_EXPERTS, dtype=I32)
    pad_end = jnp.sum(jnp.where(e_ids[None, :] <= e_ids[:, None], padded[None, :], 0), axis=1)
    pad_start = pad_end - padded
    n_blocks = -(-(n_tok * TOP_K) // MOE_BLOCK) + N_EXPERTS
    block_start = jnp.arange(n_blocks, dtype=I32) * MOE_BLOCK
    block_expert = jnp.minimum(jnp.sum((pad_end[None, :] <= block_start[:, None]).astype(I32), axis=1),
                               N_EXPERTS - 1)
    n_active = pad_end[-1:] // MOE_BLOCK
    start_of = jnp.sum(jnp.where(eid[:, :, None] == e_ids, pad_start, 0), axis=-1)
    dest_flat = (start_of + rank).T.reshape(-1)
    xs = _dispatch(dest_flat, n2_rows, n_blocks * MOE_BLOCK)
    ys = _ffn(block_expert, n_active, xs, w1[0], b1[0], w2[0], b2[0])
    out = _combine(dest_flat, h_mid, gate.T, final_norm_w, ys)
    return out.reshape(batch, t_len, d)
```

```python
import functools
import math

import numpy as np
import jax
import jax.numpy as jnp
from jax import lax
from jax.experimental import pallas as pl
from jax.experimental.pallas import tpu as pltpu

F32 = jnp.float32
BF16 = jnp.bfloat16
I32 = jnp.int32

EPS = 1e-5
HGRN_HEADS = 4
HEAD_DIM = 128
HGRN_CHUNK = 128
CONV_LEN = 31
N_EXPERTS = 32
TOP_K = 4
SWIGLU_LIMIT = 7.0
SWIGLU_ALPHA = 1.702
MOE_BLOCK = 256
LANES = 128
SUBLANES = 8
VMEM_LIMIT = 56 << 20
DMA_QUEUES = 2

_NT = (((1,), (1,)), ((), ()))


def _sigmoid(x):
    return 0.5 * jnp.tanh(0.5 * x) + 0.5


def _silu(x):
    return x * _sigmoid(x)


def _cparams(sem):
    return pltpu.CompilerParams(dimension_semantics=sem, vmem_limit_bytes=VMEM_LIMIT)


def _row_tile_shape(rows, width):
    return (rows // SUBLANES, width // LANES, SUBLANES, LANES)


def _load_row_tiles(ref, dtype=None):
    groups, lane_groups = ref.shape[0], ref.shape[1]
    cols = [ref[:, j].reshape(groups * SUBLANES, LANES) for j in range(lane_groups)]
    if dtype is not None:
        cols = [c.astype(dtype) for c in cols]
    return jnp.concatenate(cols, axis=1)


def _store_row_tiles(ref, value):
    groups, lane_groups = ref.shape[0], ref.shape[1]
    for j in range(lane_groups):
        ref[:, j] = value[:, j * LANES:(j + 1) * LANES].reshape(groups, SUBLANES, LANES)


def _row_slice(ref, group, sub):
    return ref.at[group, :, pl.ds(sub, 1), :]


def _inproj_body(x_ref, nw_ref, w_ref, o_ref):
    x = x_ref[...]
    n = x * lax.rsqrt(jnp.mean(x * x, axis=-1, keepdims=True) + EPS) * nw_ref[...]
    o_ref[...] = jnp.dot(n.astype(BF16), w_ref[...], preferred_element_type=F32)


def _in_proj(x2, norm_w, w_bf, tm=512):
    n_tok, d = x2.shape
    cols = w_bf.shape[1]
    return pl.pallas_call(
        _inproj_body,
        grid=(n_tok // tm,),
        in_specs=[pl.BlockSpec((tm, d), lambda i: (i, 0)),
                  pl.BlockSpec((1, d), lambda i: (0, 0)),
                  pl.BlockSpec((d, cols), lambda i: (0, 0))],
        out_specs=pl.BlockSpec((tm, cols), lambda i: (i, 0)),
        out_shape=jax.ShapeDtypeStruct((n_tok, cols), F32),
        compiler_params=_cparams(("parallel",)),
        name="in_proj",
    )(x2, norm_w.reshape(1, d), w_bf)


def _hgrn_levels(c):
    return [c >> (i + 1) for i in range(int(math.log2(c)))]


def _hgrn_sum_matrices(c, fwd):
    r = np.arange(c)
    m = _HGRN_SMALL_LEVEL
    small = np.zeros((c, c), np.float32)
    for t in range(c):
        p0 = t & ~(2 * m - 1)
        upper = (t & m) != 0
        if fwd:
            if upper:
                small[t, p0 + m:t + 1] = 1.0
            else:
                small[t, t + 1:p0 + m] = 1.0
        else:
            if upper:
                small[t, p0 + m:t] = 1.0
            else:
                small[t, t:p0 + m] = 1.0
    if fwd:
        mats = [r[None, :] <= r[:, None], r[None, :] > r[:, None], small]
    else:
        mats = [r[None, :] >= r[:, None], r[None, :] < r[:, None], small]
    return np.concatenate([np.asarray(x, np.float32) for x in mats], axis=0)


_HGRN_SMALL_LEVEL = 2


def _hgrn_chunk(q, k, lf, v_bf, vt_bf, st, sums_ref, fwd, c):
    dk = q.shape[1]
    hi = lf.astype(BF16)
    lo = (lf - hi.astype(F32)).astype(BF16)
    both = jnp.dot(sums_ref[...], jnp.concatenate([hi, lo], axis=1), preferred_element_type=F32)
    sums = both[:, :dk] + both[:, dk:]
    b = sums[0:c]
    d_out = sums[c:2 * c]
    row = lax.broadcasted_iota(I32, (c, dk), 0)
    rr = lax.broadcasted_iota(I32, (c, c), 0)
    cc = lax.broadcasted_iota(I32, (c, c), 1)
    sep = rr ^ cc
    a = None
    for m in _hgrn_levels(c):
        upper = (row & m) != 0
        is_q = upper if fwd else jnp.logical_not(upper)
        if m == 1:
            d = jnp.where(is_q, lf, 0.0)
        elif m == _HGRN_SMALL_LEVEL:
            d = sums[2 * c:3 * c]
        else:
            blocks = c // (2 * m)
            at = m - 1 if fwd else m
            edge = b.reshape(blocks, 2 * m, dk)[:, at:at + 1, :]
            edge = jnp.broadcast_to(edge, (blocks, 2 * m, dk)).reshape(c, dk)
            d = jnp.where(is_q, b - edge, edge - b)
        r = (jnp.where(is_q, q, k) * jnp.exp2(d)).astype(BF16)
        a_l = lax.dot_general(r, r, _NT, preferred_element_type=F32)
        a = a_l if a is None else jnp.where(sep < 2 * m, a_l, a)
    a_diag = lax.dot_general(q.astype(BF16), k.astype(BF16), _NT, preferred_element_type=F32)
    a = jnp.where(sep < 1, a_diag, a)
    a = jnp.where(rr >= cc if fwd else rr <= cc, a, 0.0)
    e_b = jnp.exp2(b)
    q_in = (q * e_b).astype(BF16)
    k_out = (k * jnp.exp2(d_out)).astype(BF16)
    o = jnp.dot(a.astype(BF16), v_bf, preferred_element_type=F32)
    o = o + lax.dot_general(q_in, st.astype(BF16), _NT, preferred_element_type=F32)
    edge_row = c - 1 if fwd else 0
    st_new = st * e_b[edge_row:edge_row + 1] + jnp.dot(vt_bf, k_out, preferred_element_type=F32)
    return o, st_new


def _hgrn_body(qz_ref, ff_ref, fb_ref, iv_ref, gz_ref, lbf_ref, lbb_ref, nw_ref, sf_ref, sb_ref, o_ref,
               q_s, kf_s, lff_s, kb_s, lfb_s, v_s, vt_s, acc_s, *, c):
    t_len, dk = qz_ref.shape
    nch = t_len // c
    q_s[...] = _silu(qz_ref[...])

    def gate(fz_ref, lb_ref, k_s, lf_s):
        lb = lb_ref[0]
        f = lb + (1.0 - lb) * _sigmoid(fz_ref[...])
        lf_s[...] = jnp.log2(f)
        k_s[...] = 1.0 - f

    gate(ff_ref, lbf_ref, kf_s, lff_s)
    gate(fb_ref, lbb_ref, kb_s, lfb_s)
    v = iv_ref[...]
    v_s[...] = v.astype(BF16)
    for ci in range(nch):
        vt_s[ci] = v[ci * c:(ci + 1) * c, :].T.astype(BF16)

    def run(k_s, lf_s, sums_ref, fwd):
        def step(i, st):
            ci = i if fwd else nch - 1 - i
            rows = pl.ds(pl.multiple_of(ci * c, c), c)
            o, st = _hgrn_chunk(q_s[rows, :], k_s[rows, :], lf_s[rows, :], v_s[rows, :], vt_s[ci],
                                st, sums_ref, fwd, c)
            if fwd:
                acc_s[rows, :] = o
            else:
                acc_s[rows, :] += o
            return st
        lax.fori_loop(0, nch, step, jnp.zeros((dk, dk), F32), unroll=8)

    run(kf_s, lff_s, sf_ref, True)
    run(kb_s, lfb_s, sb_ref, False)
    o = acc_s[...]
    o = o * lax.rsqrt(jnp.mean(o * o, axis=-1, keepdims=True) + EPS) * nw_ref[0]
    o_ref[...] = (o * _silu(gz_ref[...])).astype(o_ref.dtype)


def _hgrn(proj, lb_f, lb_b, norm_w, batch, t_len):
    h, dk, c = HGRN_HEADS, HEAD_DIM, HGRN_CHUNK
    nrows = 3
    sums_f = jnp.asarray(_hgrn_sum_matrices(c, True), BF16)
    sums_b = jnp.asarray(_hgrn_sum_matrices(c, False), BF16)

    def col(group):
        return pl.BlockSpec((t_len, dk), lambda b, hh: (b, group * h + hh))

    def per_head():
        return pl.BlockSpec((1, 1, dk), lambda b, hh: (hh, 0, 0))

    const = pl.BlockSpec((nrows * c, c), lambda b, hh: (0, 0))
    seq = lambda dt: pltpu.VMEM((t_len, dk), dt)
    return pl.pallas_call(
        functools.partial(_hgrn_body, c=c),
        grid=(batch, h),
        in_specs=[col(0), col(1), col(2), col(3), col(4), per_head(), per_head(), per_head(), const, const],
        out_specs=pl.BlockSpec((t_len, dk), lambda b, hh: (b, hh)),
        out_shape=jax.ShapeDtypeStruct((batch * t_len, h * dk), BF16),
        scratch_shapes=[seq(F32), seq(F32), seq(F32), seq(F32), seq(F32), seq(BF16),
                        pltpu.VMEM((t_len // c, dk, c), BF16), seq(F32)],
        compiler_params=_cparams(("parallel", "parallel")),
        name="hgrn",
    )(proj, proj, proj, proj, proj, lb_f.reshape(h, 1, dk), lb_b.reshape(h, 1, dk),
      norm_w.reshape(h, 1, dk), sums_f, sums_b)


CONV_ROWS = 64
CONV_HALO = 16


def _conv_body(cv_ref, cg_ref, w_ref, b_ref, lnw_ref, lnb_ref, o_ref, u_s, y_s):
    t_len, ch = cv_ref.shape
    halo, rows = CONV_HALO, CONV_ROWS
    shift0 = halo - (CONV_LEN - 1) // 2
    win = rows + 2 * halo
    for g in range(ch // LANES):
        lanes = slice(g * LANES, (g + 1) * LANES)
        u_s[g, 0:halo, :] = jnp.zeros((halo, LANES), F32)
        u_s[g, halo + t_len:, :] = jnp.zeros((halo, LANES), F32)
        u_s[g, halo:halo + t_len, :] = cv_ref[:, lanes] * _sigmoid(cg_ref[:, lanes])

    def conv_step(i, carry):
        t0 = pl.multiple_of(i * rows, rows)
        for g in range(ch // LANES):
            lanes = slice(g * LANES, (g + 1) * LANES)
            window = u_s.at[g, pl.ds(t0, win), :]
            acc = jnp.zeros((rows, LANES), F32)
            for j in range(CONV_LEN):
                off = j + shift0
                acc = acc + w_ref[j:j + 1, lanes] * window[off:off + rows, :]
            y_s[pl.ds(t0, rows), lanes] = acc + b_ref[:, lanes]
        return carry

    lax.fori_loop(0, t_len // rows, conv_step, 0)

    def step(i, carry):
        t0 = pl.multiple_of(i * rows, rows)
        y = y_s[pl.ds(t0, rows), :]
        mu = jnp.mean(y, axis=-1, keepdims=True)
        yc = y - mu
        var = jnp.mean(yc * yc, axis=-1, keepdims=True)
        z = yc * lax.rsqrt(var + EPS) * lnw_ref[...] + lnb_ref[...]
        o_ref[pl.ds(t0, rows), :] = _silu(z).astype(o_ref.dtype)
        return carry

    lax.fori_loop(0, t_len // rows, step, 0, unroll=4)


def _conv(proj, dw_w, dw_b, ln_w, ln_b, batch, t_len, first_col_block):
    ch = dw_w.shape[1]
    vec = pl.BlockSpec((1, ch), lambda b: (0, 0))
    return pl.pallas_call(
        _conv_body,
        grid=(batch,),
        in_specs=[pl.BlockSpec((t_len, ch), lambda b: (b, first_col_block)),
                  pl.BlockSpec((t_len, ch), lambda b: (b, first_col_block + 1)),
                  pl.BlockSpec((CONV_LEN, ch), lambda b: (0, 0)), vec, vec, vec],
        out_specs=pl.BlockSpec((t_len, ch), lambda b: (b, 0)),
        out_shape=jax.ShapeDtypeStruct((batch * t_len, ch), BF16),
        scratch_shapes=[pltpu.VMEM((ch // LANES, t_len + 2 * CONV_HALO, LANES), F32),
                        pltpu.VMEM((t_len, ch), F32)],
        compiler_params=_cparams(("parallel",)),
        name="conv",
    )(proj, proj, dw_w, dw_b.reshape(1, ch), ln_w.reshape(1, ch), ln_b.reshape(1, ch))


def _mix_body(a_ref, b_ref, x_ref, wa_ref, wb_ref, nw_ref, wrh_ref, wrl_ref, rb_ref, tri_ref,
              h_ref, n2_ref, eid_ref, rank_ref, gate_ref, cnt_ref):
    tm = x_ref.shape[0]
    ne = wrh_ref.shape[0]

    @pl.when(pl.program_id(0) == 0)
    def _():
        cnt_ref[...] = jnp.zeros(cnt_ref.shape, F32)

    h = x_ref[...] + jnp.dot(a_ref[...], wa_ref[...], preferred_element_type=F32) \
        + jnp.dot(b_ref[...], wb_ref[...], preferred_element_type=F32)
    h_ref[...] = h
    n2 = h * lax.rsqrt(jnp.mean(h * h, axis=-1, keepdims=True) + EPS) * nw_ref[...]
    _store_row_tiles(n2_ref, n2)
    hi = n2.astype(BF16)
    lo = (n2 - hi.astype(F32)).astype(BF16)
    logits = (lax.dot_general(wrh_ref[...], hi, _NT, preferred_element_type=F32)
              + lax.dot_general(wrh_ref[...], lo, _NT, preferred_element_type=F32)
              + lax.dot_general(wrl_ref[...], hi, _NT, preferred_element_type=F32)
              + rb_ref[...])
    e_iota = lax.broadcasted_iota(I32, (ne, tm), 0)
    work = logits
    sels, vals, ids = [], [], []
    for _ in range(TOP_K):
        mx = jnp.max(work, axis=0, keepdims=True)
        idx = jnp.min(jnp.where(work == mx, e_iota, ne), axis=0, keepdims=True)
        sel = e_iota == idx
        work = jnp.where(sel, -jnp.inf, work)
        sels.append(sel)
        vals.append(mx)
        ids.append(idx)
    exps = [jnp.exp(v - vals[0]) for v in vals]
    denom = exps[0] + exps[1] + exps[2] + exps[3]
    chosen = jnp.zeros((ne, tm), F32)
    for sel in sels:
        chosen = jnp.where(sel, 1.0, chosen)
    chosen_bf = chosen.astype(BF16)
    before = cnt_ref[...]
    prior = jnp.dot(chosen_bf, tri_ref[...], preferred_element_type=F32) \
        + jnp.concatenate([before] * (tm // LANES), axis=1)
    cnt_ref[...] = before + jnp.dot(chosen_bf, jnp.ones((tm, LANES), BF16), preferred_element_type=F32)
    for k in range(TOP_K):
        eid_ref[k:k + 1, :] = ids[k]
        rank_ref[k:k + 1, :] = jnp.sum(jnp.where(sels[k], prior, 0.0), axis=0, keepdims=True).astype(I32)
        gate_ref[k:k + 1, :] = exps[k] / denom


def _mix(a, b, x2, w_out_bf, norm_w, router_w, router_b, tm=512):
    n_tok, d = x2.shape
    wa, wb = w_out_bf[:a.shape[1]], w_out_bf[a.shape[1]:]
    ne = router_w.shape[1]
    wr_t = router_w.T
    wr_hi = wr_t.astype(BF16)
    wr_lo = (wr_t - wr_hi.astype(F32)).astype(BF16)
    tri = jnp.asarray(np.triu(np.ones((tm, tm), np.float32), k=1), BF16)
    const = lambda shape: pl.BlockSpec(shape, lambda i: tuple(0 for _ in shape))
    return pl.pallas_call(
        _mix_body,
        grid=(n_tok // tm,),
        in_specs=[pl.BlockSpec((tm, a.shape[1]), lambda i: (i, 0)),
                  pl.BlockSpec((tm, b.shape[1]), lambda i: (i, 0)),
                  pl.BlockSpec((tm, d), lambda i: (i, 0)),
                  const(wa.shape), const(wb.shape), const((1, d)),
                  const((ne, d)), const((ne, d)), const((ne, 1)), const((tm, tm))],
        out_specs=[pl.BlockSpec((tm, d), lambda i: (i, 0)),
                   pl.BlockSpec(_row_tile_shape(tm, d), lambda i: (i, 0, 0, 0)),
                   pl.BlockSpec((TOP_K, tm), lambda i: (0, i)),
                   pl.BlockSpec((TOP_K, tm), lambda i: (0, i)),
                   pl.BlockSpec((TOP_K, tm), lambda i: (0, i)),
                   pl.BlockSpec((ne, LANES), lambda i: (0, 0))],
        out_shape=[jax.ShapeDtypeStruct((n_tok, d), F32),
                   jax.ShapeDtypeStruct(_row_tile_shape(n_tok, d), F32),
                   jax.ShapeDtypeStruct((TOP_K, n_tok), I32),
                   jax.ShapeDtypeStruct((TOP_K, n_tok), I32),
                   jax.ShapeDtypeStruct((TOP_K, n_tok), F32),
                   jax.ShapeDtypeStruct((ne, LANES), F32)],
        compiler_params=_cparams(("arbitrary",)),
        name="mix_router",
    )(a, b, x2, wa, wb, norm_w.reshape(1, d), wr_hi, wr_lo, router_b.reshape(ne, 1), tri)


def _row_of(ref, r):
    return _row_slice(ref, lax.shift_right_logical(r, 3), lax.bitwise_and(r, SUBLANES - 1))


def _dispatch_body(dest_ref, n2_ref, xs_hbm, sem):
    groups = n2_ref.shape[0]
    base = pl.program_id(0) * (groups * SUBLANES * TOP_K)

    def copies(g):
        out = []
        for s in range(SUBLANES):
            for k in range(TOP_K):
                r = dest_ref[base + (g * SUBLANES + s) * TOP_K + k]
                out.append(pltpu.make_async_copy(_row_slice(n2_ref, g, s), _row_of(xs_hbm, r), sem))
        return out

    def issue(g, carry):
        for n, cp in enumerate(copies(g)):
            cp.start(priority=n % DMA_QUEUES)
        return carry

    lax.fori_loop(0, groups, issue, 0)

    def drain(g, carry):
        for cp in copies(g):
            cp.wait()
        return carry

    lax.fori_loop(0, groups, drain, 0)


def _dispatch(dest_flat, n2_rows, n_rows, td=256):
    width = n2_rows.shape[1] * LANES
    n_tok = n2_rows.shape[0] * SUBLANES
    return pl.pallas_call(
        _dispatch_body,
        grid_spec=pltpu.PrefetchScalarGridSpec(
            num_scalar_prefetch=1,
            grid=(n_tok // td,),
            in_specs=[pl.BlockSpec(_row_tile_shape(td, width), lambda i, dest: (i, 0, 0, 0))],
            out_specs=pl.BlockSpec(memory_space=pl.ANY),
            scratch_shapes=[pltpu.SemaphoreType.DMA(())]),
        out_shape=jax.ShapeDtypeStruct(_row_tile_shape(n_rows, width), F32),
        compiler_params=_cparams(("arbitrary",)),
        name="dispatch",
    )(dest_flat, n2_rows)


FFN_CAST_ROWS = 64


def _ffn_body(be_ref, first_ref, slot_ref, next_ref, nact_ref, xs_ref, w1_hbm, b1_ref, w2_hbm, b2_ref, ys_ref,
              w1_f, w2_f, w1_b, w2_b, sem):
    i = pl.program_id(0)

    def fetch(expert, slot):
        return (pltpu.make_async_copy(w1_hbm.at[expert], w1_f.at[slot], sem.at[0, slot]),
                pltpu.make_async_copy(w2_hbm.at[expert], w2_f.at[slot], sem.at[1, slot]))

    @pl.when(i < nact_ref[0])
    def _():
        slot = slot_ref[i]

        @pl.when(first_ref[i] == 1)
        def _():
            @pl.when(i == 0)
            def _():
                for cp in fetch(be_ref[0], slot):
                    cp.start()

            for cp in fetch(be_ref[i], slot):
                cp.wait()

            @pl.when(next_ref[i] >= 0)
            def _():
                for cp in fetch(next_ref[i], 1 - slot):
                    cp.start()

            def cast(src, dst):
                def body(c, carry):
                    rows = pl.ds(pl.multiple_of(c * FFN_CAST_ROWS, FFN_CAST_ROWS), FFN_CAST_ROWS)
                    dst[rows, :] = src[slot, rows, :].astype(BF16)
                    return carry
                lax.fori_loop(0, dst.shape[0] // FFN_CAST_ROWS, body, 0)

            cast(w1_f, w1_b)
            cast(w2_f, w2_b)

        x = _load_row_tiles(xs_ref, BF16)
        hdn = jnp.dot(x, w1_b[...], preferred_element_type=F32) + b1_ref[0]
        d_ff = hdn.shape[1] // 2
        glu = jnp.minimum(hdn[:, :d_ff], SWIGLU_LIMIT)
        lin = jnp.clip(hdn[:, d_ff:], -SWIGLU_LIMIT, SWIGLU_LIMIT)
        act = glu * _sigmoid(SWIGLU_ALPHA * glu) * (lin + 1.0)
        y = jnp.dot(act.astype(BF16), w2_b[...], preferred_element_type=F32) + b2_ref[0]
        _store_row_tiles(ys_ref, y)


def _ffn(block_expert, n_active, xs, w1, b1, w2, b2):
    ne, d, f2 = w1.shape
    nb = xs.shape[0] * SUBLANES // MOE_BLOCK
    block = _row_tile_shape(MOE_BLOCK, d)
    idx = jnp.arange(nb, dtype=I32)
    active = idx < n_active[0]
    prev = jnp.concatenate([block_expert[:1] - 1, block_expert[:-1]])
    first = (active & (block_expert != prev)).astype(I32)
    run_slot = (jnp.sum(jnp.where(idx[None, :] <= idx[:, None], first[None, :], 0), axis=1) - 1) & 1
    later_first = (first[None, :] == 1) & (idx[None, :] > idx[:, None])
    next_block = jnp.min(jnp.where(later_first, idx[None, :], nb), axis=1)
    next_expert = jnp.where(next_block < nb, block_expert[jnp.minimum(next_block, nb - 1)], -1).astype(I32)

    def blk(i, be, fi, sl, nx, nact):
        return (jnp.minimum(i, nact[0] - 1), 0, 0, 0)

    def exp(i, be, fi, sl, nx, nact):
        return (be[jnp.minimum(i, nact[0] - 1)], 0, 0)

    return pl.pallas_call(
        _ffn_body,
        grid_spec=pltpu.PrefetchScalarGridSpec(
            num_scalar_prefetch=5,
            grid=(nb,),
            in_specs=[pl.BlockSpec(block, blk),
                      pl.BlockSpec(memory_space=pl.ANY),
                      pl.BlockSpec((1, 1, f2), exp),
                      pl.BlockSpec(memory_space=pl.ANY),
                      pl.BlockSpec((1, 1, d), exp)],
            out_specs=pl.BlockSpec(block, blk),
            scratch_shapes=[pltpu.VMEM((2, d, f2), F32), pltpu.VMEM((2, f2 // 2, d), F32),
                            pltpu.VMEM((d, f2), BF16), pltpu.VMEM((f2 // 2, d), BF16),
                            pltpu.SemaphoreType.DMA((2, 2))]),
        out_shape=jax.ShapeDtypeStruct(xs.shape, F32),
        compiler_params=_cparams(("arbitrary",)),
        name="expert_ffn",
    )(block_expert, first, run_slot.astype(I32), next_expert, n_active, xs, w1, b1.reshape(ne, 1, f2), w2,
      b2.reshape(ne, 1, d))


def _combine_body(dest_ref, h_ref, gate_ref, nw_ref, ys_hbm, o_ref, buf, sem):
    step = pl.program_id(0)
    groups, lane_groups = buf.shape[2], buf.shape[3]
    tc = groups * SUBLANES

    def issue(at_step, slot):
        base = at_step * (tc * TOP_K)

        def body(g, carry):
            for s in range(SUBLANES):
                for k in range(TOP_K):
                    r = dest_ref[base + (g * SUBLANES + s) * TOP_K + k]
                    pltpu.make_async_copy(_row_of(ys_hbm, r), _row_slice(buf.at[slot, k], g, s),
                                          sem.at[slot]).start(priority=k % DMA_QUEUES)
            return carry

        lax.fori_loop(0, groups, body, 0)

    def drain(slot):
        def body(g, carry):
            for s in range(SUBLANES):
                for k in range(TOP_K):
                    pltpu.make_async_copy(_row_slice(ys_hbm, 0, 0), _row_slice(buf.at[slot, k], g, s),
                                          sem.at[slot]).wait()
            return carry

        lax.fori_loop(0, groups, body, 0)

    @pl.when(step == 0)
    def _():
        issue(0, 0)

    @pl.when(step + 1 < pl.num_programs(0))
    def _():
        issue(step + 1, (step + 1) % 2)

    slot = step % 2
    drain(slot)
    gates = gate_ref[...]
    cols = []
    for j in range(lane_groups):
        acc = gates[:, 0:1] * buf[slot, 0, :, j].reshape(tc, LANES)
        for k in range(1, TOP_K):
            acc = acc + gates[:, k:k + 1] * buf[slot, k, :, j].reshape(tc, LANES)
        cols.append(acc)
    y = h_ref[...] + jnp.concatenate(cols, axis=1)
    o_ref[...] = y * lax.rsqrt(jnp.mean(y * y, axis=-1, keepdims=True) + EPS) * nw_ref[...]


def _combine(dest_flat, h, gates_tk, norm_w, ys, tc=128):
    n_tok, d = h.shape
    return pl.pallas_call(
        _combine_body,
        grid_spec=pltpu.PrefetchScalarGridSpec(
            num_scalar_prefetch=1,
            grid=(n_tok // tc,),
            in_specs=[pl.BlockSpec((tc, d), lambda i, dest: (i, 0)),
                      pl.BlockSpec((tc, TOP_K), lambda i, dest: (i, 0)),
                      pl.BlockSpec((1, d), lambda i, dest: (0, 0)),
                      pl.BlockSpec(memory_space=pl.ANY)],
            out_specs=pl.BlockSpec((tc, d), lambda i, dest: (i, 0)),
            scratch_shapes=[pltpu.VMEM((2, TOP_K) + _row_tile_shape(tc, d), F32),
                            pltpu.SemaphoreType.DMA((2,))]),
        out_shape=jax.ShapeDtypeStruct((n_tok, d), F32),
        compiler_params=_cparams(("arbitrary",)),
        name="combine_norm",
    )(dest_flat, h, gates_tk, norm_w.reshape(1, d), ys)


def kernel(x, norm1_w, w_in, lb_logits, hgrn_norm_w, dw_w, dw_b, conv_ln_w, conv_ln_b, w_out, norm2_w,
           router_w, router_b, w1, b1, w2, b2, final_norm_w):
    batch, t_len, d = x.shape
    assert w_in.shape[0] == 1, "single-layer block"
    n_tok = batch * t_len
    hk = HGRN_HEADS * HEAD_DIM
    conv_ch = dw_w.shape[2]
    lb_table = jnp.cumsum(jax.nn.softmax(lb_logits.astype(F32), axis=1), axis=1)
    x2 = x.reshape(n_tok, d)
    proj = _in_proj(x2, norm1_w[0], w_in[0].astype(BF16))
    a = _hgrn(proj, lb_table[0, 0], lb_table[1, 0], hgrn_norm_w[0], batch, t_len)
    b = _conv(proj, dw_w[0], dw_b[0], conv_ln_w[0], conv_ln_b[0], batch, t_len, (5 * hk) // conv_ch)
    h_mid, n2_rows, eid, rank, gate, cnt = _mix(a, b, x2, w_out[0].astype(BF16), norm2_w[0],
                                                router_w[0], router_b[0])
    counts = cnt[:, 0].astype(I32)
    padded = ((counts + MOE_BLOCK - 1) // MOE_BLOCK) * MOE_BLOCK
    e_ids = jnp.arange(N_EXPERTS, dtype=I32)
    pad_end = jnp.sum(jnp.where(e_ids[None, :] <= e_ids[:, None], padded[None, :], 0), axis=1)
    pad_start = pad_end - padded
    n_blocks = -(-(n_tok * TOP_K) // MOE_BLOCK) + N_EXPERTS
    block_start = jnp.arange(n_blocks, dtype=I32) * MOE_BLOCK
    block_expert = jnp.minimum(jnp.sum((pad_end[None, :] <= block_start[:, None]).astype(I32), axis=1),
                               N_EXPERTS - 1)
    n_active = pad_end[-1:] // MOE_BLOCK
    start_of = jnp.sum(jnp.where(eid[:, :, None] == e_ids, pad_start, 0), axis=-1)
    dest_flat = (start_of + rank).T.reshape(-1)
    xs = _dispatch(dest_flat, n2_rows, n_blocks * MOE_BLOCK)
    ys = _ffn(block_expert, n_active, xs, w1[0], b1[0], w2[0], b2[0])
    out = _combine(dest_flat, h_mid, gate.T, final_norm_w, ys)
    return out.reshape(batch, t_len, d)
```

```python
import functools
import math

import numpy as np
import jax
import jax.numpy as jnp
from jax import lax
from jax.experimental import pallas as pl
from jax.experimental.pallas import tpu as pltpu

F32 = jnp.float32
BF16 = jnp.bfloat16
I32 = jnp.int32

EPS = 1e-5
HGRN_HEADS = 4
HEAD_DIM = 128
HGRN_CHUNK = 128
CONV_LEN = 31
N_EXPERTS = 32
TOP_K = 4
SWIGLU_LIMIT = 7.0
SWIGLU_ALPHA = 1.702
MOE_BLOCK = 256
ROUTE_TILE = 512
LANES = 128
SUBLANES = 8
VMEM_LIMIT = 56 << 20
DMA_QUEUES = 2

_NT = (((1,), (1,)), ((), ()))


def _sigmoid(x):
    return 0.5 * jnp.tanh(0.5 * x) + 0.5


def _silu(x):
    return x * _sigmoid(x)


def _cparams(sem):
    return pltpu.CompilerParams(dimension_semantics=sem, vmem_limit_bytes=VMEM_LIMIT)


def _row_tile_shape(rows, width):
    return (rows // SUBLANES, width // LANES, SUBLANES, LANES)


def _load_row_tiles(ref, dtype=None):
    groups, lane_groups = ref.shape[0], ref.shape[1]
    cols = [ref[:, j].reshape(groups * SUBLANES, LANES) for j in range(lane_groups)]
    if dtype is not None:
        cols = [c.astype(dtype) for c in cols]
    return jnp.concatenate(cols, axis=1)


def _store_row_tiles(ref, value):
    groups, lane_groups = ref.shape[0], ref.shape[1]
    for j in range(lane_groups):
        ref[:, j] = value[:, j * LANES:(j + 1) * LANES].reshape(groups, SUBLANES, LANES)


def _row_slice(ref, group, sub):
    return ref.at[group, :, pl.ds(sub, 1), :]


def _inproj_body(x_ref, nw_ref, w_ref, o_ref):
    x = x_ref[...]
    n = x * lax.rsqrt(jnp.mean(x * x, axis=-1, keepdims=True) + EPS) * nw_ref[...]
    o_ref[...] = jnp.dot(n.astype(BF16), w_ref[...], preferred_element_type=F32)


def _in_proj(x2, norm_w, w_bf, tm=512):
    n_tok, d = x2.shape
    cols = w_bf.shape[1]
    return pl.pallas_call(
        _inproj_body,
        grid=(n_tok // tm,),
        in_specs=[pl.BlockSpec((tm, d), lambda i: (i, 0)),
                  pl.BlockSpec((1, d), lambda i: (0, 0)),
                  pl.BlockSpec((d, cols), lambda i: (0, 0))],
        out_specs=pl.BlockSpec((tm, cols), lambda i: (i, 0)),
        out_shape=jax.ShapeDtypeStruct((n_tok, cols), F32),
        compiler_params=_cparams(("parallel",)),
        name="in_proj",
    )(x2, norm_w.reshape(1, d), w_bf)


def _hgrn_levels(c):
    return [c >> (i + 1) for i in range(int(math.log2(c)))]


def _hgrn_sum_matrices(c, fwd):
    r = np.arange(c)
    m = _HGRN_SMALL_LEVEL
    small = np.zeros((c, c), np.float32)
    for t in range(c):
        p0 = t & ~(2 * m - 1)
        upper = (t & m) != 0
        if fwd:
            if upper:
                small[t, p0 + m:t + 1] = 1.0
            else:
                small[t, t + 1:p0 + m] = 1.0
        else:
            if upper:
                small[t, p0 + m:t] = 1.0
            else:
                small[t, t:p0 + m] = 1.0
    if fwd:
        mats = [r[None, :] <= r[:, None], r[None, :] > r[:, None], small]
    else:
        mats = [r[None, :] >= r[:, None], r[None, :] < r[:, None], small]
    return np.concatenate([np.asarray(x, np.float32) for x in mats], axis=0)


_HGRN_SMALL_LEVEL = 2


def _hgrn_chunk(q, k, lf, v_bf, vt_bf, st, sums_ref, fwd, c):
    dk = q.shape[1]
    hi = lf.astype(BF16)
    lo = (lf - hi.astype(F32)).astype(BF16)
    both = jnp.dot(sums_ref[...], jnp.concatenate([hi, lo], axis=1), preferred_element_type=F32)
    sums = both[:, :dk] + both[:, dk:]
    b = sums[0:c]
    d_out = sums[c:2 * c]
    row = lax.broadcasted_iota(I32, (c, dk), 0)
    rr = lax.broadcasted_iota(I32, (c, c), 0)
    cc = lax.broadcasted_iota(I32, (c, c), 1)
    sep = rr ^ cc
    a = None
    for m in _hgrn_levels(c):
        upper = (row & m) != 0
        is_q = upper if fwd else jnp.logical_not(upper)
        if m == 1:
            d = jnp.where(is_q, lf, 0.0)
        elif m == _HGRN_SMALL_LEVEL:
            d = sums[2 * c:3 * c]
        else:
            blocks = c // (2 * m)
            at = m - 1 if fwd else m
            edge = b.reshape(blocks, 2 * m, dk)[:, at:at + 1, :]
            edge = jnp.broadcast_to(edge, (blocks, 2 * m, dk)).reshape(c, dk)
            d = jnp.where(is_q, b - edge, edge - b)
        r = (jnp.where(is_q, q, k) * jnp.exp2(d)).astype(BF16)
        a_l = lax.dot_general(r, r, _NT, preferred_element_type=F32)
        a = a_l if a is None else jnp.where(sep < 2 * m, a_l, a)
    a_diag = lax.dot_general(q.astype(BF16), k.astype(BF16), _NT, preferred_element_type=F32)
    a = jnp.where(sep < 1, a_diag, a)
    a = jnp.where(rr >= cc if fwd else rr <= cc, a, 0.0)
    e_b = jnp.exp2(b)
    q_in = (q * e_b).astype(BF16)
    k_out = (k * jnp.exp2(d_out)).astype(BF16)
    o = jnp.dot(a.astype(BF16), v_bf, preferred_element_type=F32)
    o = o + lax.dot_general(q_in, st.astype(BF16), _NT, preferred_element_type=F32)
    edge_row = c - 1 if fwd else 0
    st_new = st * e_b[edge_row:edge_row + 1] + jnp.dot(vt_bf, k_out, preferred_element_type=F32)
    return o, st_new


def _hgrn_body(qz_ref, ff_ref, fb_ref, iv_ref, gz_ref, lbf_ref, lbb_ref, nw_ref, sf_ref, sb_ref, o_ref,
               q_s, kf_s, lff_s, kb_s, lfb_s, v_s, vt_s, acc_s, *, c):
    t_len, dk = qz_ref.shape
    nch = t_len // c
    q_s[...] = _silu(qz_ref[...])

    def gate(fz_ref, lb_ref, k_s, lf_s):
        lb = lb_ref[0]
        f = lb + (1.0 - lb) * _sigmoid(fz_ref[...])
        lf_s[...] = jnp.log2(f)
        k_s[...] = 1.0 - f

    gate(ff_ref, lbf_ref, kf_s, lff_s)
    gate(fb_ref, lbb_ref, kb_s, lfb_s)
    v = iv_ref[...]
    v_s[...] = v.astype(BF16)
    for ci in range(nch):
        vt_s[ci] = v[ci * c:(ci + 1) * c, :].T.astype(BF16)

    def run(k_s, lf_s, sums_ref, fwd):
        def step(i, st):
            ci = i if fwd else nch - 1 - i
            rows = pl.ds(pl.multiple_of(ci * c, c), c)
            o, st = _hgrn_chunk(q_s[rows, :], k_s[rows, :], lf_s[rows, :], v_s[rows, :], vt_s[ci],
                                st, sums_ref, fwd, c)
            if fwd:
                acc_s[rows, :] = o
            else:
                acc_s[rows, :] += o
            return st
        lax.fori_loop(0, nch, step, jnp.zeros((dk, dk), F32), unroll=8)

    run(kf_s, lff_s, sf_ref, True)
    run(kb_s, lfb_s, sb_ref, False)
    o = acc_s[...]
    o = o * lax.rsqrt(jnp.mean(o * o, axis=-1, keepdims=True) + EPS) * nw_ref[0]
    o_ref[...] = (o * _silu(gz_ref[...])).astype(o_ref.dtype)


def _hgrn(proj, lb_f, lb_b, norm_w, batch, t_len):
    h, dk, c = HGRN_HEADS, HEAD_DIM, HGRN_CHUNK
    nrows = 3
    sums_f = jnp.asarray(_hgrn_sum_matrices(c, True), BF16)
    sums_b = jnp.asarray(_hgrn_sum_matrices(c, False), BF16)

    def col(group):
        return pl.BlockSpec((t_len, dk), lambda b, hh: (b, group * h + hh))

    def per_head():
        return pl.BlockSpec((1, 1, dk), lambda b, hh: (hh, 0, 0))

    const = pl.BlockSpec((nrows * c, c), lambda b, hh: (0, 0))
    seq = lambda dt: pltpu.VMEM((t_len, dk), dt)
    return pl.pallas_call(
        functools.partial(_hgrn_body, c=c),
        grid=(batch, h),
        in_specs=[col(0), col(1), col(2), col(3), col(4), per_head(), per_head(), per_head(), const, const],
        out_specs=pl.BlockSpec((t_len, dk), lambda b, hh: (b, hh)),
        out_shape=jax.ShapeDtypeStruct((batch * t_len, h * dk), BF16),
        scratch_shapes=[seq(F32), seq(F32), seq(F32), seq(F32), seq(F32), seq(BF16),
                        pltpu.VMEM((t_len // c, dk, c), BF16), seq(F32)],
        compiler_params=_cparams(("parallel", "parallel")),
        name="hgrn",
    )(proj, proj, proj, proj, proj, lb_f.reshape(h, 1, dk), lb_b.reshape(h, 1, dk),
      norm_w.reshape(h, 1, dk), sums_f, sums_b)


CONV_ROWS = 64
CONV_HALO = 16


def _conv_body(cv_ref, cg_ref, w_ref, b_ref, lnw_ref, lnb_ref, o_ref, u_s, y_s):
    t_len, ch = cv_ref.shape
    halo, rows = CONV_HALO, CONV_ROWS
    shift0 = halo - (CONV_LEN - 1) // 2
    win = rows + 2 * halo
    for g in range(ch // LANES):
        lanes = slice(g * LANES, (g + 1) * LANES)
        u_s[g, 0:halo, :] = jnp.zeros((halo, LANES), F32)
        u_s[g, halo + t_len:, :] = jnp.zeros((halo, LANES), F32)
        u_s[g, halo:halo + t_len, :] = cv_ref[:, lanes] * _sigmoid(cg_ref[:, lanes])

    def conv_step(i, carry):
        t0 = pl.multiple_of(i * rows, rows)
        for g in range(ch // LANES):
            lanes = slice(g * LANES, (g + 1) * LANES)
            window = u_s.at[g, pl.ds(t0, win), :]
            acc = jnp.zeros((rows, LANES), F32)
            for j in range(CONV_LEN):
                off = j + shift0
                acc = acc + w_ref[j:j + 1, lanes] * window[off:off + rows, :]
            y_s[pl.ds(t0, rows), lanes] = acc + b_ref[:, lanes]
        return carry

    lax.fori_loop(0, t_len // rows, conv_step, 0)

    def step(i, carry):
        t0 = pl.multiple_of(i * rows, rows)
        y = y_s[pl.ds(t0, rows), :]
        mu = jnp.mean(y, axis=-1, keepdims=True)
        yc = y - mu
        var = jnp.mean(yc * yc, axis=-1, keepdims=True)
        z = yc * lax.rsqrt(var + EPS) * lnw_ref[...] + lnb_ref[...]
        o_ref[pl.ds(t0, rows), :] = _silu(z).astype(o_ref.dtype)
        return carry

    lax.fori_loop(0, t_len // rows, step, 0, unroll=4)


def _conv(proj, dw_w, dw_b, ln_w, ln_b, batch, t_len, first_col_block):
    ch = dw_w.shape[1]
    vec = pl.BlockSpec((1, ch), lambda b: (0, 0))
    return pl.pallas_call(
        _conv_body,
        grid=(batch,),
        in_specs=[pl.BlockSpec((t_len, ch), lambda b: (b, first_col_block)),
                  pl.BlockSpec((t_len, ch), lambda b: (b, first_col_block + 1)),
                  pl.BlockSpec((CONV_LEN, ch), lambda b: (0, 0)), vec, vec, vec],
        out_specs=pl.BlockSpec((t_len, ch), lambda b: (b, 0)),
        out_shape=jax.ShapeDtypeStruct((batch * t_len, ch), BF16),
        scratch_shapes=[pltpu.VMEM((ch // LANES, t_len + 2 * CONV_HALO, LANES), F32),
                        pltpu.VMEM((t_len, ch), F32)],
        compiler_params=_cparams(("parallel",)),
        name="conv",
    )(proj, proj, dw_w, dw_b.reshape(1, ch), ln_w.reshape(1, ch), ln_b.reshape(1, ch))


def _mix_body(a_ref, b_ref, x_ref, wa_ref, wb_ref, nw_ref, wrh_ref, wrl_ref, rb_ref, tri_ref,
              h_ref, n2_ref, eid_ref, rank_ref, gate_ref, cnt_ref):
    tm = x_ref.shape[0]
    ne = wrh_ref.shape[0]

    h = x_ref[...] + jnp.dot(a_ref[...], wa_ref[...], preferred_element_type=F32) \
        + jnp.dot(b_ref[...], wb_ref[...], preferred_element_type=F32)
    h_ref[...] = h
    n2 = h * lax.rsqrt(jnp.mean(h * h, axis=-1, keepdims=True) + EPS) * nw_ref[...]
    hi = n2.astype(BF16)
    n2_ref[...] = hi
    lo = (n2 - hi.astype(F32)).astype(BF16)
    logits = (lax.dot_general(wrh_ref[...], hi, _NT, preferred_element_type=F32)
              + lax.dot_general(wrh_ref[...], lo, _NT, preferred_element_type=F32)
              + lax.dot_general(wrl_ref[...], hi, _NT, preferred_element_type=F32)
              + rb_ref[...])
    e_iota = lax.broadcasted_iota(I32, (ne, tm), 0)
    work = logits
    sels, vals, ids = [], [], []
    for _ in range(TOP_K):
        mx = jnp.max(work, axis=0, keepdims=True)
        idx = jnp.min(jnp.where(work == mx, e_iota, ne), axis=0, keepdims=True)
        sel = e_iota == idx
        work = jnp.where(sel, -jnp.inf, work)
        sels.append(sel)
        vals.append(mx)
        ids.append(idx)
    exps = [jnp.exp(v - vals[0]) for v in vals]
    denom = exps[0] + exps[1] + exps[2] + exps[3]
    chosen = jnp.zeros((ne, tm), F32)
    for sel in sels:
        chosen = jnp.where(sel, 1.0, chosen)
    chosen_bf = chosen.astype(BF16)
    prior = jnp.dot(chosen_bf, tri_ref[...], preferred_element_type=F32)
    cnt_ref[0] = jnp.dot(chosen_bf, jnp.ones((tm, LANES), BF16), preferred_element_type=F32)
    for k in range(TOP_K):
        eid_ref[k:k + 1, :] = ids[k]
        rank_ref[k:k + 1, :] = jnp.sum(jnp.where(sels[k], prior, 0.0), axis=0, keepdims=True).astype(I32)
        gate_ref[k:k + 1, :] = exps[k] / denom


def _mix(a, b, x2, w_out_bf, norm_w, router_w, router_b, tm=ROUTE_TILE):
    n_tok, d = x2.shape
    wa, wb = w_out_bf[:a.shape[1]], w_out_bf[a.shape[1]:]
    ne = router_w.shape[1]
    wr_t = router_w.T
    wr_hi = wr_t.astype(BF16)
    wr_lo = (wr_t - wr_hi.astype(F32)).astype(BF16)
    tri = jnp.asarray(np.triu(np.ones((tm, tm), np.float32), k=1), BF16)
    const = lambda shape: pl.BlockSpec(shape, lambda i: tuple(0 for _ in shape))
    return pl.pallas_call(
        _mix_body,
        grid=(n_tok // tm,),
        in_specs=[pl.BlockSpec((tm, a.shape[1]), lambda i: (i, 0)),
                  pl.BlockSpec((tm, b.shape[1]), lambda i: (i, 0)),
                  pl.BlockSpec((tm, d), lambda i: (i, 0)),
                  const(wa.shape), const(wb.shape), const((1, d)),
                  const((ne, d)), const((ne, d)), const((ne, 1)), const((tm, tm))],
        out_specs=[pl.BlockSpec((tm, d), lambda i: (i, 0)),
                   pl.BlockSpec((tm, d), lambda i: (i, 0)),
                   pl.BlockSpec((TOP_K, tm), lambda i: (0, i)),
                   pl.BlockSpec((TOP_K, tm), lambda i: (0, i)),
                   pl.BlockSpec((TOP_K, tm), lambda i: (0, i)),
                   pl.BlockSpec((1, ne, LANES), lambda i: (i, 0, 0))],
        out_shape=[jax.ShapeDtypeStruct((n_tok, d), F32),
                   jax.ShapeDtypeStruct((n_tok, d), BF16),
                   jax.ShapeDtypeStruct((TOP_K, n_tok), I32),
                   jax.ShapeDtypeStruct((TOP_K, n_tok), I32),
                   jax.ShapeDtypeStruct((TOP_K, n_tok), F32),
                   jax.ShapeDtypeStruct((n_tok // tm, ne, LANES), F32)],
        compiler_params=_cparams(("parallel",)),
        name="mix_router",
    )(a, b, x2, wa, wb, norm_w.reshape(1, d), wr_hi, wr_lo, router_b.reshape(ne, 1), tri)


def _row_of(ref, r):
    return _row_slice(ref, lax.shift_right_logical(r, 3), lax.bitwise_and(r, SUBLANES - 1))


SLAB_ROWS = TOP_K * ROUTE_TILE + N_EXPERTS * SUBLANES
SLAB_CHUNK = 256
SLAB_PIECES = (64, 32, 16, 8, 4, 2, 1)


def _for_each_piece(ngroups, fn):
    for size in SLAB_PIECES:
        first = lax.bitwise_and(ngroups, ~(2 * size - 1))

        @pl.when(lax.bitwise_and(ngroups, size) != 0)
        def _():
            fn(first, size)


def _dispatch_body(src_ref, dst_ref, ngrp_ref, n2_ref, dloc_ref, xs_hbm, stage, sem):
    step = pl.program_id(0)
    tm = n2_ref.shape[0]
    slot = step % 2

    def shares(at_step, at_slot, act):
        def per_expert(e, carry):
            idx = at_step * N_EXPERTS + e
            src, dst = src_ref[idx], dst_ref[idx]

            def piece(first, size):
                act(pltpu.make_async_copy(stage.at[at_slot, pl.ds(src + first, size)],
                                          xs_hbm.at[pl.ds(dst + first, size)], sem.at[at_slot]))

            _for_each_piece(ngrp_ref[idx], piece)
            return carry

        lax.fori_loop(0, N_EXPERTS, per_expert, 0)

    @pl.when(step >= 2)
    def _():
        shares(step - 2, slot, lambda cp: cp.wait())

    n2 = n2_ref[...]
    dloc = dloc_ref[...]
    for c in range(SLAB_ROWS // SLAB_CHUNK):
        rows = lax.broadcasted_iota(I32, (SLAB_CHUNK, tm), 0) + c * SLAB_CHUNK
        pick = jnp.zeros((SLAB_CHUNK, tm), F32)
        for k in range(TOP_K):
            pick = pick + jnp.where(dloc[k:k + 1, :] == rows, 1.0, 0.0)
        chunk = jnp.dot(pick.astype(BF16), n2, preferred_element_type=F32)
        _store_row_tiles(stage.at[slot, pl.ds(c * (SLAB_CHUNK // SUBLANES), SLAB_CHUNK // SUBLANES)], chunk)
    shares(step, slot, lambda cp: cp.start())

    @pl.when(step == pl.num_programs(0) - 1)
    def _():
        shares(step, slot, lambda cp: cp.wait())

        @pl.when(step >= 1)
        def _():
            shares(step - 1, 1 - slot, lambda cp: cp.wait())


def _dispatch(src_grp, dst_grp, ngrp, n2_bf, dloc, n_rows):
    n_tok, d = n2_bf.shape
    tm = ROUTE_TILE
    return pl.pallas_call(
        _dispatch_body,
        grid_spec=pltpu.PrefetchScalarGridSpec(
            num_scalar_prefetch=3,
            grid=(n_tok // tm,),
            in_specs=[pl.BlockSpec((tm, d), lambda i, *_: (i, 0)),
                      pl.BlockSpec((TOP_K, tm), lambda i, *_: (0, i))],
            out_specs=pl.BlockSpec(memory_space=pl.ANY),
            scratch_shapes=[pltpu.VMEM((2,) + _row_tile_shape(SLAB_ROWS, d), F32),
                            pltpu.SemaphoreType.DMA((2,))]),
        out_shape=jax.ShapeDtypeStruct(_row_tile_shape(n_rows, d), F32),
        compiler_params=_cparams(("arbitrary",)),
        name="dispatch",
    )(src_grp, dst_grp, ngrp, n2_bf, dloc)


FFN_CAST_ROWS = 64


def _ffn_body(be_ref, first_ref, slot_ref, next_ref, nact_ref, xs_ref, w1_hbm, b1_ref, w2_hbm, b2_ref, ys_ref,
              w1_f, w2_f, w1_b, w2_b, sem):
    i = pl.program_id(0)

    def fetch(expert, slot):
        return (pltpu.make_async_copy(w1_hbm.at[expert], w1_f.at[slot], sem.at[0, slot]),
                pltpu.make_async_copy(w2_hbm.at[expert], w2_f.at[slot], sem.at[1, slot]))

    @pl.when(i < nact_ref[0])
    def _():
        slot = slot_ref[i]

        @pl.when(first_ref[i] == 1)
        def _():
            @pl.when(i == 0)
            def _():
                for cp in fetch(be_ref[0], slot):
                    cp.start()

            for cp in fetch(be_ref[i], slot):
                cp.wait()

            @pl.when(next_ref[i] >= 0)
            def _():
                for cp in fetch(next_ref[i], 1 - slot):
                    cp.start()

            def cast(src, dst):
                def body(c, carry):
                    rows = pl.ds(pl.multiple_of(c * FFN_CAST_ROWS, FFN_CAST_ROWS), FFN_CAST_ROWS)
                    dst[rows, :] = src[slot, rows, :].astype(BF16)
                    return carry
                lax.fori_loop(0, dst.shape[0] // FFN_CAST_ROWS, body, 0)

            cast(w1_f, w1_b)
            cast(w2_f, w2_b)

        x = _load_row_tiles(xs_ref, BF16)
        hdn = jnp.dot(x, w1_b[...], preferred_element_type=F32) + b1_ref[0]
        d_ff = hdn.shape[1] // 2
        glu = jnp.minimum(hdn[:, :d_ff], SWIGLU_LIMIT)
        lin = jnp.clip(hdn[:, d_ff:], -SWIGLU_LIMIT, SWIGLU_LIMIT)
        act = glu * _sigmoid(SWIGLU_ALPHA * glu) * (lin + 1.0)
        y = jnp.dot(act.astype(BF16), w2_b[...], preferred_element_type=F32) + b2_ref[0]
        _store_row_tiles(ys_ref, y)


def _ffn(block_expert, n_active, xs, w1, b1, w2, b2):
    ne, d, f2 = w1.shape
    nb = xs.shape[0] * SUBLANES // MOE_BLOCK
    block = _row_tile_shape(MOE_BLOCK, d)
    idx = jnp.arange(nb, dtype=I32)
    active = idx < n_active[0]
    prev = jnp.concatenate([block_expert[:1] - 1, block_expert[:-1]])
    first = (active & (block_expert != prev)).astype(I32)
    run_slot = (jnp.sum(jnp.where(idx[None, :] <= idx[:, None], first[None, :], 0), axis=1) - 1) & 1
    later_first = (first[None, :] == 1) & (idx[None, :] > idx[:, None])
    next_block = jnp.min(jnp.where(later_first, idx[None, :], nb), axis=1)
    next_expert = jnp.where(next_block < nb, block_expert[jnp.minimum(next_block, nb - 1)], -1).astype(I32)

    def blk(i, be, fi, sl, nx, nact):
        return (jnp.minimum(i, nact[0] - 1), 0, 0, 0)

    def exp(i, be, fi, sl, nx, nact):
        return (be[jnp.minimum(i, nact[0] - 1)], 0, 0)

    return pl.pallas_call(
        _ffn_body,
        grid_spec=pltpu.PrefetchScalarGridSpec(
            num_scalar_prefetch=5,
            grid=(nb,),
            in_specs=[pl.BlockSpec(block, blk),
                      pl.BlockSpec(memory_space=pl.ANY),
                      pl.BlockSpec((1, 1, f2), exp),
                      pl.BlockSpec(memory_space=pl.ANY),
                      pl.BlockSpec((1, 1, d), exp)],
            out_specs=pl.BlockSpec(block, blk),
            scratch_shapes=[pltpu.VMEM((2, d, f2), F32), pltpu.VMEM((2, f2 // 2, d), F32),
                            pltpu.VMEM((d, f2), BF16), pltpu.VMEM((f2 // 2, d), BF16),
                            pltpu.SemaphoreType.DMA((2, 2))]),
        out_shape=jax.ShapeDtypeStruct(xs.shape, F32),
        compiler_params=_cparams(("arbitrary",)),
        name="expert_ffn",
    )(block_expert, first, run_slot.astype(I32), next_expert, n_active, xs, w1, b1.reshape(ne, 1, f2), w2,
      b2.reshape(ne, 1, d))


def _combine_body(dest_ref, h_ref, gate_ref, nw_ref, ys_hbm, o_ref, buf, sem):
    step = pl.program_id(0)
    groups, lane_groups = buf.shape[2], buf.shape[3]
    tc = groups * SUBLANES

    def issue(at_step, slot):
        base = at_step * (tc * TOP_K)

        def body(g, carry):
            for s in range(SUBLANES):
                for k in range(TOP_K):
                    r = dest_ref[base + (g * SUBLANES + s) * TOP_K + k]
                    pltpu.make_async_copy(_row_of(ys_hbm, r), _row_slice(buf.at[slot, k], g, s),
                                          sem.at[slot]).start(priority=k % DMA_QUEUES)
            return carry

        lax.fori_loop(0, groups, body, 0)

    def drain(slot):
        def body(g, carry):
            for s in range(SUBLANES):
                for k in range(TOP_K):
                    pltpu.make_async_copy(_row_slice(ys_hbm, 0, 0), _row_slice(buf.at[slot, k], g, s),
                                          sem.at[slot]).wait()
            return carry

        lax.fori_loop(0, groups, body, 0)

    @pl.when(step == 0)
    def _():
        issue(0, 0)

    @pl.when(step + 1 < pl.num_programs(0))
    def _():
        issue(step + 1, (step + 1) % 2)

    slot = step % 2
    drain(slot)
    gates = gate_ref[...]
    cols = []
    for j in range(lane_groups):
        acc = gates[:, 0:1] * buf[slot, 0, :, j].reshape(tc, LANES)
        for k in range(1, TOP_K):
            acc = acc + gates[:, k:k + 1] * buf[slot, k, :, j].reshape(tc, LANES)
        cols.append(acc)
    y = h_ref[...] + jnp.concatenate(cols, axis=1)
    o_ref[...] = y * lax.rsqrt(jnp.mean(y * y, axis=-1, keepdims=True) + EPS) * nw_ref[...]


def _combine(dest_flat, h, gates_tk, norm_w, ys, tc=128):
    n_tok, d = h.shape
    return pl.pallas_call(
        _combine_body,
        grid_spec=pltpu.PrefetchScalarGridSpec(
            num_scalar_prefetch=1,
            grid=(n_tok // tc,),
            in_specs=[pl.BlockSpec((tc, d), lambda i, dest: (i, 0)),
                      pl.BlockSpec((tc, TOP_K), lambda i, dest: (i, 0)),
                      pl.BlockSpec((1, d), lambda i, dest: (0, 0)),
                      pl.BlockSpec(memory_space=pl.ANY)],
            out_specs=pl.BlockSpec((tc, d), lambda i, dest: (i, 0)),
            scratch_shapes=[pltpu.VMEM((2, TOP_K) + _row_tile_shape(tc, d), F32),
                            pltpu.SemaphoreType.DMA((2,))]),
        out_shape=jax.ShapeDtypeStruct((n_tok, d), F32),
        compiler_params=_cparams(("arbitrary",)),
        name="combine_norm",
    )(dest_flat, h, gates_tk, norm_w.reshape(1, d), ys)


def kernel(x, norm1_w, w_in, lb_logits, hgrn_norm_w, dw_w, dw_b, conv_ln_w, conv_ln_b, w_out, norm2_w,
           router_w, router_b, w1, b1, w2, b2, final_norm_w):
    batch, t_len, d = x.shape
    assert w_in.shape[0] == 1, "single-layer block"
    n_tok = batch * t_len
    hk = HGRN_HEADS * HEAD_DIM
    conv_ch = dw_w.shape[2]
    lb_table = jnp.cumsum(jax.nn.softmax(lb_logits.astype(F32), axis=1), axis=1)
    x2 = x.reshape(n_tok, d)
    proj = _in_proj(x2, norm1_w[0], w_in[0].astype(BF16))
    a = _hgrn(proj, lb_table[0, 0], lb_table[1, 0], hgrn_norm_w[0], batch, t_len)
    b = _conv(proj, dw_w[0], dw_b[0], conv_ln_w[0], conv_ln_b[0], batch, t_len, (5 * hk) // conv_ch)
    h_mid, n2_bf, eid, lrank, gate, cnt_tiles = _mix(a, b, x2, w_out[0].astype(BF16), norm2_w[0],
                                                     router_w[0], router_b[0])
    nt = n_tok // ROUTE_TILE
    e_ids = jnp.arange(N_EXPERTS, dtype=I32)
    t_ids = jnp.arange(nt, dtype=I32)
    share = (cnt_tiles[:, :, 0].astype(I32) + SUBLANES - 1) // SUBLANES * SUBLANES
    in_slab = jnp.sum(jnp.where(e_ids[None, None, :] < e_ids[None, :, None], share[:, None, :], 0), axis=2)
    in_expert = jnp.sum(jnp.where((t_ids[None, :] < t_ids[:, None])[:, :, None], share[None, :, :], 0), axis=1)
    padded = (jnp.sum(share, axis=0) + MOE_BLOCK - 1) // MOE_BLOCK * MOE_BLOCK
    pad_end = jnp.sum(jnp.where(e_ids[None, :] <= e_ids[:, None], padded[None, :], 0), axis=1)
    in_xs = (pad_end - padded)[None, :] + in_expert
    n_blocks = -(-(n_tok * TOP_K + nt * N_EXPERTS * (SUBLANES - 1)) // MOE_BLOCK) + N_EXPERTS
    block_start = jnp.arange(n_blocks, dtype=I32) * MOE_BLOCK
    block_expert = jnp.minimum(jnp.sum((pad_end[None, :] <= block_start[:, None]).astype(I32), axis=1),
                               N_EXPERTS - 1)
    n_active = pad_end[-1:] // MOE_BLOCK
    chose = eid.reshape(TOP_K, nt, ROUTE_TILE)[..., None] == e_ids
    lrank = lrank.reshape(TOP_K, nt, ROUTE_TILE)
    dloc = (jnp.sum(jnp.where(chose, in_slab[None, :, None, :], 0), axis=-1) + lrank).reshape(TOP_K, n_tok)
    dest = (jnp.sum(jnp.where(chose, in_xs[None, :, None, :], 0), axis=-1) + lrank).reshape(TOP_K, n_tok)
    dest_flat = dest.T.reshape(-1)
    xs = _dispatch((in_slab // SUBLANES).reshape(-1), (in_xs // SUBLANES).reshape(-1),
                   (share // SUBLANES).reshape(-1), n2_bf, dloc, n_blocks * MOE_BLOCK)
    ys = _ffn(block_expert, n_active, xs, w1[0], b1[0], w2[0], b2[0])
    out = _combine(dest_flat, h_mid, gate.T, final_norm_w, ys)
    return out.reshape(batch, t_len, d)
```

```python
import functools
import math

import numpy as np
import jax
import jax.numpy as jnp
from jax import lax
from jax.experimental import pallas as pl
from jax.experimental.pallas import tpu as pltpu

F32 = jnp.float32
BF16 = jnp.bfloat16
I32 = jnp.int32

EPS = 1e-5
HGRN_HEADS = 4
HEAD_DIM = 128
HGRN_CHUNK = 128
CONV_LEN = 31
N_EXPERTS = 32
TOP_K = 4
SWIGLU_LIMIT = 7.0
SWIGLU_ALPHA = 1.702
MOE_BLOCK = 256
ROUTE_TILE = 512
LANES = 128
SUBLANES = 8
VMEM_LIMIT = 56 << 20

_NT = (((1,), (1,)), ((), ()))


def _sigmoid(x):
    return 0.5 * jnp.tanh(0.5 * x) + 0.5


def _silu(x):
    return x * _sigmoid(x)


def _cparams(sem):
    return pltpu.CompilerParams(dimension_semantics=sem, vmem_limit_bytes=VMEM_LIMIT)


def _inproj_body(x_ref, nw_ref, w_ref, o_ref):
    x = x_ref[...]
    n = x * lax.rsqrt(jnp.mean(x * x, axis=-1, keepdims=True) + EPS) * nw_ref[...]
    o_ref[...] = jnp.dot(n.astype(BF16), w_ref[...], preferred_element_type=F32)


def _in_proj(x2, norm_w, w_bf, tm=512):
    n_tok, d = x2.shape
    cols = w_bf.shape[1]
    return pl.pallas_call(
        _inproj_body,
        grid=(n_tok // tm,),
        in_specs=[pl.BlockSpec((tm, d), lambda i: (i, 0)),
                  pl.BlockSpec((1, d), lambda i: (0, 0)),
                  pl.BlockSpec((d, cols), lambda i: (0, 0))],
        out_specs=pl.BlockSpec((tm, cols), lambda i: (i, 0)),
        out_shape=jax.ShapeDtypeStruct((n_tok, cols), F32),
        compiler_params=_cparams(("parallel",)),
        name="in_proj",
    )(x2, norm_w.reshape(1, d), w_bf)


def _hgrn_levels(c):
    return [c >> (i + 1) for i in range(int(math.log2(c)))]


def _hgrn_sum_matrices(c, fwd):
    r = np.arange(c)
    m = _HGRN_SMALL_LEVEL
    small = np.zeros((c, c), np.float32)
    for t in range(c):
        p0 = t & ~(2 * m - 1)
        upper = (t & m) != 0
        if fwd:
            if upper:
                small[t, p0 + m:t + 1] = 1.0
            else:
                small[t, t + 1:p0 + m] = 1.0
        else:
            if upper:
                small[t, p0 + m:t] = 1.0
            else:
                small[t, t:p0 + m] = 1.0
    if fwd:
        mats = [r[None, :] <= r[:, None], r[None, :] > r[:, None], small]
    else:
        mats = [r[None, :] >= r[:, None], r[None, :] < r[:, None], small]
    return np.concatenate([np.asarray(x, np.float32) for x in mats], axis=0)


_HGRN_SMALL_LEVEL = 2


def _hgrn_chunk(q, k, lf, v_bf, vt_bf, st, sums_ref, fwd, c):
    dk = q.shape[1]
    hi = lf.astype(BF16)
    lo = (lf - hi.astype(F32)).astype(BF16)
    both = jnp.dot(sums_ref[...], jnp.concatenate([hi, lo], axis=1), preferred_element_type=F32)
    sums = both[:, :dk] + both[:, dk:]
    b = sums[0:c]
    d_out = sums[c:2 * c]
    row = lax.broadcasted_iota(I32, (c, dk), 0)
    rr = lax.broadcasted_iota(I32, (c, c), 0)
    cc = lax.broadcasted_iota(I32, (c, c), 1)
    sep = rr ^ cc
    a = None
    for m in _hgrn_levels(c):
        upper = (row & m) != 0
        is_q = upper if fwd else jnp.logical_not(upper)
        if m == 1:
            d = jnp.where(is_q, lf, 0.0)
        elif m == _HGRN_SMALL_LEVEL:
            d = sums[2 * c:3 * c]
        else:
            blocks = c // (2 * m)
            at = m - 1 if fwd else m
            edge = b.reshape(blocks, 2 * m, dk)[:, at:at + 1, :]
            edge = jnp.broadcast_to(edge, (blocks, 2 * m, dk)).reshape(c, dk)
            d = jnp.where(is_q, b - edge, edge - b)
        r = (jnp.where(is_q, q, k) * jnp.exp2(d)).astype(BF16)
        a_l = lax.dot_general(r, r, _NT, preferred_element_type=F32)
        a = a_l if a is None else jnp.where(sep < 2 * m, a_l, a)
    a_diag = lax.dot_general(q.astype(BF16), k.astype(BF16), _NT, preferred_element_type=F32)
    a = jnp.where(sep < 1, a_diag, a)
    a = jnp.where(rr >= cc if fwd else rr <= cc, a, 0.0)
    e_b = jnp.exp2(b)
    q_in = (q * e_b).astype(BF16)
    k_out = (k * jnp.exp2(d_out)).astype(BF16)
    o = jnp.dot(a.astype(BF16), v_bf, preferred_element_type=F32)
    o = o + lax.dot_general(q_in, st.astype(BF16), _NT, preferred_element_type=F32)
    edge_row = c - 1 if fwd else 0
    st_new = st * e_b[edge_row:edge_row + 1] + jnp.dot(vt_bf, k_out, preferred_element_type=F32)
    return o, st_new


def _hgrn_body(qz_ref, ff_ref, fb_ref, iv_ref, gz_ref, lbf_ref, lbb_ref, nw_ref, sf_ref, sb_ref, o_ref,
               q_s, kf_s, lff_s, kb_s, lfb_s, v_s, vt_s, acc_s, *, c):
    t_len, dk = qz_ref.shape
    nch = t_len // c
    q_s[...] = _silu(qz_ref[...])

    def gate(fz_ref, lb_ref, k_s, lf_s):
        lb = lb_ref[0]
        f = lb + (1.0 - lb) * _sigmoid(fz_ref[...])
        lf_s[...] = jnp.log2(f)
        k_s[...] = 1.0 - f

    gate(ff_ref, lbf_ref, kf_s, lff_s)
    gate(fb_ref, lbb_ref, kb_s, lfb_s)
    v = iv_ref[...]
    v_s[...] = v.astype(BF16)
    for ci in range(nch):
        vt_s[ci] = v[ci * c:(ci + 1) * c, :].T.astype(BF16)

    def run(k_s, lf_s, sums_ref, fwd):
        def step(i, st):
            ci = i if fwd else nch - 1 - i
            rows = pl.ds(pl.multiple_of(ci * c, c), c)
            o, st = _hgrn_chunk(q_s[rows, :], k_s[rows, :], lf_s[rows, :], v_s[rows, :], vt_s[ci],
                                st, sums_ref, fwd, c)
            if fwd:
                acc_s[rows, :] = o
            else:
                acc_s[rows, :] += o
            return st
        lax.fori_loop(0, nch, step, jnp.zeros((dk, dk), F32), unroll=8)

    run(kf_s, lff_s, sf_ref, True)
    run(kb_s, lfb_s, sb_ref, False)
    o = acc_s[...]
    o = o * lax.rsqrt(jnp.mean(o * o, axis=-1, keepdims=True) + EPS) * nw_ref[0]
    o_ref[...] = (o * _silu(gz_ref[...])).astype(o_ref.dtype)


def _hgrn(proj, lb_f, lb_b, norm_w, batch, t_len):
    h, dk, c = HGRN_HEADS, HEAD_DIM, HGRN_CHUNK
    nrows = 3
    sums_f = jnp.asarray(_hgrn_sum_matrices(c, True), BF16)
    sums_b = jnp.asarray(_hgrn_sum_matrices(c, False), BF16)

    def col(group):
        return pl.BlockSpec((t_len, dk), lambda b, hh: (b, group * h + hh))

    def per_head():
        return pl.BlockSpec((1, 1, dk), lambda b, hh: (hh, 0, 0))

    const = pl.BlockSpec((nrows * c, c), lambda b, hh: (0, 0))
    seq = lambda dt: pltpu.VMEM((t_len, dk), dt)
    return pl.pallas_call(
        functools.partial(_hgrn_body, c=c),
        grid=(batch, h),
        in_specs=[col(0), col(1), col(2), col(3), col(4), per_head(), per_head(), per_head(), const, const],
        out_specs=pl.BlockSpec((t_len, dk), lambda b, hh: (b, hh)),
        out_shape=jax.ShapeDtypeStruct((batch * t_len, h * dk), BF16),
        scratch_shapes=[seq(F32), seq(F32), seq(F32), seq(F32), seq(F32), seq(BF16),
                        pltpu.VMEM((t_len // c, dk, c), BF16), seq(F32)],
        compiler_params=_cparams(("parallel", "parallel")),
        name="hgrn",
    )(proj, proj, proj, proj, proj, lb_f.reshape(h, 1, dk), lb_b.reshape(h, 1, dk),
      norm_w.reshape(h, 1, dk), sums_f, sums_b)


CONV_ROWS = 64
CONV_HALO = 16


def _conv_body(cv_ref, cg_ref, w_ref, b_ref, lnw_ref, lnb_ref, o_ref, u_s, y_s):
    t_len, ch = cv_ref.shape
    halo, rows = CONV_HALO, CONV_ROWS
    shift0 = halo - (CONV_LEN - 1) // 2
    win = rows + 2 * halo
    for g in range(ch // LANES):
        lanes = slice(g * LANES, (g + 1) * LANES)
        u_s[g, 0:halo, :] = jnp.zeros((halo, LANES), F32)
        u_s[g, halo + t_len:, :] = jnp.zeros((halo, LANES), F32)
        u_s[g, halo:halo + t_len, :] = cv_ref[:, lanes] * _sigmoid(cg_ref[:, lanes])

    def conv_step(i, carry):
        t0 = pl.multiple_of(i * rows, rows)
        for g in range(ch // LANES):
            lanes = slice(g * LANES, (g + 1) * LANES)
            window = u_s.at[g, pl.ds(t0, win), :]
            acc = jnp.zeros((rows, LANES), F32)
            for j in range(CONV_LEN):
                off = j + shift0
                acc = acc + w_ref[j:j + 1, lanes] * window[off:off + rows, :]
            y_s[pl.ds(t0, rows), lanes] = acc + b_ref[:, lanes]
        return carry

    lax.fori_loop(0, t_len // rows, conv_step, 0)

    def step(i, carry):
        t0 = pl.multiple_of(i * rows, rows)
        y = y_s[pl.ds(t0, rows), :]
        mu = jnp.mean(y, axis=-1, keepdims=True)
        yc = y - mu
        var = jnp.mean(yc * yc, axis=-1, keepdims=True)
        z = yc * lax.rsqrt(var + EPS) * lnw_ref[...] + lnb_ref[...]
        o_ref[pl.ds(t0, rows), :] = _silu(z).astype(o_ref.dtype)
        return carry

    lax.fori_loop(0, t_len // rows, step, 0, unroll=4)


def _conv(proj, dw_w, dw_b, ln_w, ln_b, batch, t_len, first_col_block):
    ch = dw_w.shape[1]
    vec = pl.BlockSpec((1, ch), lambda b: (0, 0))
    return pl.pallas_call(
        _conv_body,
        grid=(batch,),
        in_specs=[pl.BlockSpec((t_len, ch), lambda b: (b, first_col_block)),
                  pl.BlockSpec((t_len, ch), lambda b: (b, first_col_block + 1)),
                  pl.BlockSpec((CONV_LEN, ch), lambda b: (0, 0)), vec, vec, vec],
        out_specs=pl.BlockSpec((t_len, ch), lambda b: (b, 0)),
        out_shape=jax.ShapeDtypeStruct((batch * t_len, ch), BF16),
        scratch_shapes=[pltpu.VMEM((ch // LANES, t_len + 2 * CONV_HALO, LANES), F32),
                        pltpu.VMEM((t_len, ch), F32)],
        compiler_params=_cparams(("parallel",)),
        name="conv",
    )(proj, proj, dw_w, dw_b.reshape(1, ch), ln_w.reshape(1, ch), ln_b.reshape(1, ch))


def _mix_body(a_ref, b_ref, x_ref, wa_ref, wb_ref, nw_ref, wrh_ref, wrl_ref, rb_ref, tri_ref,
              h_ref, n2_ref, eid_ref, rank_ref, gate_ref, cnt_ref):
    tm = x_ref.shape[0]
    ne = wrh_ref.shape[0]

    h = x_ref[...] + jnp.dot(a_ref[...], wa_ref[...], preferred_element_type=F32) \
        + jnp.dot(b_ref[...], wb_ref[...], preferred_element_type=F32)
    h_ref[...] = h
    n2 = h * lax.rsqrt(jnp.mean(h * h, axis=-1, keepdims=True) + EPS) * nw_ref[...]
    hi = n2.astype(BF16)
    n2_ref[...] = hi
    lo = (n2 - hi.astype(F32)).astype(BF16)
    logits = (lax.dot_general(wrh_ref[...], hi, _NT, preferred_element_type=F32)
              + lax.dot_general(wrh_ref[...], lo, _NT, preferred_element_type=F32)
              + lax.dot_general(wrl_ref[...], hi, _NT, preferred_element_type=F32)
              + rb_ref[...])
    e_iota = lax.broadcasted_iota(I32, (ne, tm), 0)
    work = logits
    sels, vals, ids = [], [], []
    for _ in range(TOP_K):
        mx = jnp.max(work, axis=0, keepdims=True)
        idx = jnp.min(jnp.where(work == mx, e_iota, ne), axis=0, keepdims=True)
        sel = e_iota == idx
        work = jnp.where(sel, -jnp.inf, work)
        sels.append(sel)
        vals.append(mx)
        ids.append(idx)
    exps = [jnp.exp(v - vals[0]) for v in vals]
    denom = exps[0] + exps[1] + exps[2] + exps[3]
    chosen = jnp.zeros((ne, tm), F32)
    for sel in sels:
        chosen = jnp.where(sel, 1.0, chosen)
    chosen_bf = chosen.astype(BF16)
    prior = jnp.dot(chosen_bf, tri_ref[...], preferred_element_type=F32)
    cnt_ref[0] = jnp.dot(chosen_bf, jnp.ones((tm, LANES), BF16), preferred_element_type=F32)
    for k in range(TOP_K):
        eid_ref[k:k + 1, :] = ids[k]
        rank_ref[k:k + 1, :] = jnp.sum(jnp.where(sels[k], prior, 0.0), axis=0, keepdims=True).astype(I32)
        gate_ref[k:k + 1, :] = exps[k] / denom


def _mix(a, b, x2, w_out_bf, norm_w, router_w, router_b, tm=ROUTE_TILE):
    n_tok, d = x2.shape
    wa, wb = w_out_bf[:a.shape[1]], w_out_bf[a.shape[1]:]
    ne = router_w.shape[1]
    wr_t = router_w.T
    wr_hi = wr_t.astype(BF16)
    wr_lo = (wr_t - wr_hi.astype(F32)).astype(BF16)
    tri = jnp.asarray(np.triu(np.ones((tm, tm), np.float32), k=1), BF16)
    const = lambda shape: pl.BlockSpec(shape, lambda i: tuple(0 for _ in shape))
    return pl.pallas_call(
        _mix_body,
        grid=(n_tok // tm,),
        in_specs=[pl.BlockSpec((tm, a.shape[1]), lambda i: (i, 0)),
                  pl.BlockSpec((tm, b.shape[1]), lambda i: (i, 0)),
                  pl.BlockSpec((tm, d), lambda i: (i, 0)),
                  const(wa.shape), const(wb.shape), const((1, d)),
                  const((ne, d)), const((ne, d)), const((ne, 1)), const((tm, tm))],
        out_specs=[pl.BlockSpec((tm, d), lambda i: (i, 0)),
                   pl.BlockSpec((tm, d), lambda i: (i, 0)),
                   pl.BlockSpec((TOP_K, tm), lambda i: (0, i)),
                   pl.BlockSpec((TOP_K, tm), lambda i: (0, i)),
                   pl.BlockSpec((TOP_K, tm), lambda i: (0, i)),
                   pl.BlockSpec((1, ne, LANES), lambda i: (i, 0, 0))],
        out_shape=[jax.ShapeDtypeStruct((n_tok, d), F32),
                   jax.ShapeDtypeStruct((n_tok, d), BF16),
                   jax.ShapeDtypeStruct((TOP_K, n_tok), I32),
                   jax.ShapeDtypeStruct((TOP_K, n_tok), I32),
                   jax.ShapeDtypeStruct((TOP_K, n_tok), F32),
                   jax.ShapeDtypeStruct((n_tok // tm, ne, LANES), F32)],
        compiler_params=_cparams(("parallel",)),
        name="mix_router",
    )(a, b, x2, wa, wb, norm_w.reshape(1, d), wr_hi, wr_lo, router_b.reshape(ne, 1), tri)


SLAB_ROWS = TOP_K * ROUTE_TILE + N_EXPERTS * SUBLANES
SLAB_CHUNK = 256
SLAB_PIECES = (64, 32, 16, 8, 4, 2, 1)


def _for_each_piece(ngroups, fn):
    for size in SLAB_PIECES:
        first = lax.bitwise_and(ngroups, ~(2 * size - 1))

        @pl.when(lax.bitwise_and(ngroups, size) != 0)
        def _():
            fn(first, size)


def _group_rows(first_group, groups):
    return pl.ds(pl.multiple_of(first_group * SUBLANES, SUBLANES), groups * SUBLANES)


def _dispatch_body(src_ref, dst_ref, ngrp_ref, n2_ref, dloc_ref, xs_hbm, stage, sem):
    step = pl.program_id(0)
    tm = n2_ref.shape[0]
    slot = step % 2

    def shares(at_step, at_slot, act):
        def per_expert(e, carry):
            idx = at_step * N_EXPERTS + e
            src, dst = src_ref[idx], dst_ref[idx]

            def piece(first, size):
                act(pltpu.make_async_copy(stage.at[at_slot, _group_rows(src + first, size)],
                                          xs_hbm.at[_group_rows(dst + first, size)], sem.at[at_slot]))

            _for_each_piece(ngrp_ref[idx], piece)
            return carry

        lax.fori_loop(0, N_EXPERTS, per_expert, 0)

    @pl.when(step >= 2)
    def _():
        shares(step - 2, slot, lambda cp: cp.wait())

    n2 = n2_ref[...]
    dloc = dloc_ref[...]
    for c in range(SLAB_ROWS // SLAB_CHUNK):
        rows = lax.broadcasted_iota(I32, (SLAB_CHUNK, tm), 0) + c * SLAB_CHUNK
        pick = jnp.zeros((SLAB_CHUNK, tm), F32)
        for k in range(TOP_K):
            pick = pick + jnp.where(dloc[k:k + 1, :] == rows, 1.0, 0.0)
        stage[slot, c * SLAB_CHUNK:(c + 1) * SLAB_CHUNK, :] = jnp.dot(pick.astype(BF16), n2,
                                                                       preferred_element_type=F32)
    shares(step, slot, lambda cp: cp.start())

    @pl.when(step == pl.num_programs(0) - 1)
    def _():
        shares(step, slot, lambda cp: cp.wait())

        @pl.when(step >= 1)
        def _():
            shares(step - 1, 1 - slot, lambda cp: cp.wait())


def _dispatch(src_grp, dst_grp, ngrp, n2_bf, dloc, n_rows):
    n_tok, d = n2_bf.shape
    tm = ROUTE_TILE
    return pl.pallas_call(
        _dispatch_body,
        grid_spec=pltpu.PrefetchScalarGridSpec(
            num_scalar_prefetch=3,
            grid=(n_tok // tm,),
            in_specs=[pl.BlockSpec((tm, d), lambda i, *_: (i, 0)),
                      pl.BlockSpec((TOP_K, tm), lambda i, *_: (0, i))],
            out_specs=pl.BlockSpec(memory_space=pl.ANY),
            scratch_shapes=[pltpu.VMEM((2, SLAB_ROWS, d), F32), pltpu.SemaphoreType.DMA((2,))]),
        out_shape=jax.ShapeDtypeStruct((n_rows, d), F32),
        compiler_params=_cparams(("arbitrary",)),
        name="dispatch",
    )(src_grp, dst_grp, ngrp, n2_bf, dloc)


FFN_CAST_ROWS = 64


def _ffn_body(be_ref, first_ref, slot_ref, next_ref, nact_ref, xs_ref, w1_hbm, b1_ref, w2_hbm, b2_ref, ys_ref,
              w1_f, w2_f, w1_b, w2_b, sem):
    i = pl.program_id(0)

    def fetch(expert, slot):
        return (pltpu.make_async_copy(w1_hbm.at[expert], w1_f.at[slot], sem.at[0, slot]),
                pltpu.make_async_copy(w2_hbm.at[expert], w2_f.at[slot], sem.at[1, slot]))

    @pl.when(i < nact_ref[0])
    def _():
        slot = slot_ref[i]

        @pl.when(first_ref[i] == 1)
        def _():
            @pl.when(i == 0)
            def _():
                for cp in fetch(be_ref[0], slot):
                    cp.start()

            for cp in fetch(be_ref[i], slot):
                cp.wait()

            @pl.when(next_ref[i] >= 0)
            def _():
                for cp in fetch(next_ref[i], 1 - slot):
                    cp.start()

            def cast(src, dst):
                def body(c, carry):
                    rows = pl.ds(pl.multiple_of(c * FFN_CAST_ROWS, FFN_CAST_ROWS), FFN_CAST_ROWS)
                    dst[rows, :] = src[slot, rows, :].astype(BF16)
                    return carry
                lax.fori_loop(0, dst.shape[0] // FFN_CAST_ROWS, body, 0)

            cast(w1_f, w1_b)
            cast(w2_f, w2_b)

        x = xs_ref[...].astype(BF16)
        hdn = jnp.dot(x, w1_b[...], preferred_element_type=F32) + b1_ref[0]
        d_ff = hdn.shape[1] // 2
        glu = jnp.minimum(hdn[:, :d_ff], SWIGLU_LIMIT)
        lin = jnp.clip(hdn[:, d_ff:], -SWIGLU_LIMIT, SWIGLU_LIMIT)
        act = glu * _sigmoid(SWIGLU_ALPHA * glu) * (lin + 1.0)
        y = jnp.dot(act.astype(BF16), w2_b[...], preferred_element_type=F32) + b2_ref[0]
        ys_ref[...] = y


def _ffn(block_expert, n_active, xs, w1, b1, w2, b2):
    ne, d, f2 = w1.shape
    nb = xs.shape[0] // MOE_BLOCK
    block = (MOE_BLOCK, d)
    idx = jnp.arange(nb, dtype=I32)
    active = idx < n_active[0]
    prev = jnp.concatenate([block_expert[:1] - 1, block_expert[:-1]])
    first = (active & (block_expert != prev)).astype(I32)
    run_slot = (jnp.sum(jnp.where(idx[None, :] <= idx[:, None], first[None, :], 0), axis=1) - 1) & 1
    later_first = (first[None, :] == 1) & (idx[None, :] > idx[:, None])
    next_block = jnp.min(jnp.where(later_first, idx[None, :], nb), axis=1)
    next_expert = jnp.where(next_block < nb, block_expert[jnp.minimum(next_block, nb - 1)], -1).astype(I32)

    def blk(i, be, fi, sl, nx, nact):
        return (jnp.minimum(i, nact[0] - 1), 0)

    def exp(i, be, fi, sl, nx, nact):
        return (be[jnp.minimum(i, nact[0] - 1)], 0, 0)

    return pl.pallas_call(
        _ffn_body,
        grid_spec=pltpu.PrefetchScalarGridSpec(
            num_scalar_prefetch=5,
            grid=(nb,),
            in_specs=[pl.BlockSpec(block, blk),
                      pl.BlockSpec(memory_space=pl.ANY),
                      pl.BlockSpec((1, 1, f2), exp),
                      pl.BlockSpec(memory_space=pl.ANY),
                      pl.BlockSpec((1, 1, d), exp)],
            out_specs=pl.BlockSpec(block, blk),
            scratch_shapes=[pltpu.VMEM((2, d, f2), F32), pltpu.VMEM((2, f2 // 2, d), F32),
                            pltpu.VMEM((d, f2), BF16), pltpu.VMEM((f2 // 2, d), BF16),
                            pltpu.SemaphoreType.DMA((2, 2))]),
        out_shape=jax.ShapeDtypeStruct(xs.shape, F32),
        compiler_params=_cparams(("arbitrary",)),
        name="expert_ffn",
    )(block_expert, first, run_slot.astype(I32), next_expert, n_active, xs, w1, b1.reshape(ne, 1, f2), w2,
      b2.reshape(ne, 1, d))


def _combine_body(src_ref, dst_ref, ngrp_ref, h_ref, dloc_ref, gate_ref, nw_ref, ys_hbm, o_ref, stage, sem):
    step = pl.program_id(0)
    tm = h_ref.shape[0]
    slot = step % 2

    def shares(at_step, at_slot, act):
        def per_expert(e, carry):
            idx = at_step * N_EXPERTS + e
            src, dst = src_ref[idx], dst_ref[idx]

            def piece(first, size):
                act(pltpu.make_async_copy(ys_hbm.at[_group_rows(dst + first, size)],
                                          stage.at[at_slot, _group_rows(src + first, size)], sem.at[at_slot]))

            _for_each_piece(ngrp_ref[idx], piece)
            return carry

        lax.fori_loop(0, N_EXPERTS, per_expert, 0)

    @pl.when(step == 0)
    def _():
        stage[...] = jnp.zeros(stage.shape, F32)
        shares(0, 0, lambda cp: cp.start())

    @pl.when(step + 1 < pl.num_programs(0))
    def _():
        shares(step + 1, 1 - slot, lambda cp: cp.start())

    shares(step, slot, lambda cp: cp.wait())
    dloc = dloc_ref[...]
    gates = gate_ref[...]
    moe = jnp.zeros(o_ref.shape, F32)
    for c in range(SLAB_ROWS // SLAB_CHUNK):
        cols = lax.broadcasted_iota(I32, (tm, SLAB_CHUNK), 1) + c * SLAB_CHUNK
        weight = jnp.zeros((tm, SLAB_CHUNK), F32)
        for k in range(TOP_K):
            weight = weight + jnp.where(dloc[:, k:k + 1] == cols, gates[:, k:k + 1], 0.0)
        rows = stage[slot, c * SLAB_CHUNK:(c + 1) * SLAB_CHUNK, :].astype(BF16)
        moe = moe + jnp.dot(weight.astype(BF16), rows, preferred_element_type=F32)
    y = h_ref[...] + moe
    o_ref[...] = y * lax.rsqrt(jnp.mean(y * y, axis=-1, keepdims=True) + EPS) * nw_ref[...]


def _combine(src_grp, dst_grp, ngrp, h, dloc_tk, gates_tk, norm_w, ys):
    n_tok, d = h.shape
    tm = ROUTE_TILE
    return pl.pallas_call(
        _combine_body,
        grid_spec=pltpu.PrefetchScalarGridSpec(
            num_scalar_prefetch=3,
            grid=(n_tok // tm,),
            in_specs=[pl.BlockSpec((tm, d), lambda i, *_: (i, 0)),
                      pl.BlockSpec((tm, TOP_K), lambda i, *_: (i, 0)),
                      pl.BlockSpec((tm, TOP_K), lambda i, *_: (i, 0)),
                      pl.BlockSpec((1, d), lambda i, *_: (0, 0)),
                      pl.BlockSpec(memory_space=pl.ANY)],
            out_specs=pl.BlockSpec((tm, d), lambda i, *_: (i, 0)),
            scratch_shapes=[pltpu.VMEM((2, SLAB_ROWS, d), F32), pltpu.SemaphoreType.DMA((2,))]),
        out_shape=jax.ShapeDtypeStruct((n_tok, d), F32),
        compiler_params=_cparams(("arbitrary",)),
        name="combine_norm",
    )(src_grp, dst_grp, ngrp, h, dloc_tk, gates_tk, norm_w.reshape(1, d), ys)


def kernel(x, norm1_w, w_in, lb_logits, hgrn_norm_w, dw_w, dw_b, conv_ln_w, conv_ln_b, w_out, norm2_w,
           router_w, router_b, w1, b1, w2, b2, final_norm_w):
    batch, t_len, d = x.shape
    assert w_in.shape[0] == 1, "single-layer block"
    n_tok = batch * t_len
    hk = HGRN_HEADS * HEAD_DIM
    conv_ch = dw_w.shape[2]
    lb_table = jnp.cumsum(jax.nn.softmax(lb_logits.astype(F32), axis=1), axis=1)
    x2 = x.reshape(n_tok, d)
    proj = _in_proj(x2, norm1_w[0], w_in[0].astype(BF16))
    a = _hgrn(proj, lb_table[0, 0], lb_table[1, 0], hgrn_norm_w[0], batch, t_len)
    b = _conv(proj, dw_w[0], dw_b[0], conv_ln_w[0], conv_ln_b[0], batch, t_len, (5 * hk) // conv_ch)
    h_mid, n2_bf, eid, lrank, gate, cnt_tiles = _mix(a, b, x2, w_out[0].astype(BF16), norm2_w[0],
                                                     router_w[0], router_b[0])
    nt = n_tok // ROUTE_TILE
    e_ids = jnp.arange(N_EXPERTS, dtype=I32)
    t_ids = jnp.arange(nt, dtype=I32)
    share = (cnt_tiles[:, :, 0].astype(I32) + SUBLANES - 1) // SUBLANES * SUBLANES
    in_slab = jnp.sum(jnp.where(e_ids[None, None, :] < e_ids[None, :, None], share[:, None, :], 0), axis=2)
    in_expert = jnp.sum(jnp.where((t_ids[None, :] < t_ids[:, None])[:, :, None], share[None, :, :], 0), axis=1)
    padded = (jnp.sum(share, axis=0) + MOE_BLOCK - 1) // MOE_BLOCK * MOE_BLOCK
    pad_end = jnp.sum(jnp.where(e_ids[None, :] <= e_ids[:, None], padded[None, :], 0), axis=1)
    in_xs = (pad_end - padded)[None, :] + in_expert
    n_blocks = -(-(n_tok * TOP_K + nt * N_EXPERTS * (SUBLANES - 1)) // MOE_BLOCK) + N_EXPERTS
    block_start = jnp.arange(n_blocks, dtype=I32) * MOE_BLOCK
    block_expert = jnp.minimum(jnp.sum((pad_end[None, :] <= block_start[:, None]).astype(I32), axis=1),
                               N_EXPERTS - 1)
    n_active = pad_end[-1:] // MOE_BLOCK
    chose = eid.reshape(TOP_K, nt, ROUTE_TILE)[..., None] == e_ids
    lrank = lrank.reshape(TOP_K, nt, ROUTE_TILE)
    dloc = (jnp.sum(jnp.where(chose, in_slab[None, :, None, :], 0), axis=-1) + lrank).reshape(TOP_K, n_tok)
    slab_grp, xs_grp, n_grp = [(v // SUBLANES).reshape(-1) for v in (in_slab, in_xs, share)]
    xs = _dispatch(slab_grp, xs_grp, n_grp, n2_bf, dloc, n_blocks * MOE_BLOCK)
    ys = _ffn(block_expert, n_active, xs, w1[0], b1[0], w2[0], b2[0])
    out = _combine(slab_grp, xs_grp, n_grp, h_mid, dloc.T, gate.T, final_norm_w, ys)
    return out.reshape(batch, t_len, d)
```

```python
import functools
import math

import numpy as np
import jax
import jax.numpy as jnp
from jax import lax
from jax.experimental import pallas as pl
from jax.experimental.pallas import tpu as pltpu

F32 = jnp.float32
BF16 = jnp.bfloat16
I32 = jnp.int32

EPS = 1e-5
HGRN_HEADS = 4
HEAD_DIM = 128
HGRN_CHUNK = 128
CONV_LEN = 31
N_EXPERTS = 32
TOP_K = 4
SWIGLU_LIMIT = 7.0
SWIGLU_ALPHA = 1.702
MOE_BLOCK = 256
ROUTE_TILE = 512
LANES = 128
SUBLANES = 8
VMEM_LIMIT = 56 << 20

_NT = (((1,), (1,)), ((), ()))


def _sigmoid(x):
    return 0.5 * jnp.tanh(0.5 * x) + 0.5


def _silu(x):
    return x * _sigmoid(x)


def _cparams(sem):
    return pltpu.CompilerParams(dimension_semantics=sem, vmem_limit_bytes=VMEM_LIMIT)


def _inproj_body(x_ref, nw_ref, w_ref, o_ref):
    x = x_ref[...]
    n = x * lax.rsqrt(jnp.mean(x * x, axis=-1, keepdims=True) + EPS) * nw_ref[...]
    o_ref[...] = jnp.dot(n.astype(BF16), w_ref[...], preferred_element_type=F32)


def _in_proj(x2, norm_w, w_bf, tm=512):
    n_tok, d = x2.shape
    cols = w_bf.shape[1]
    return pl.pallas_call(
        _inproj_body,
        grid=(n_tok // tm,),
        in_specs=[pl.BlockSpec((tm, d), lambda i: (i, 0)),
                  pl.BlockSpec((1, d), lambda i: (0, 0)),
                  pl.BlockSpec((d, cols), lambda i: (0, 0))],
        out_specs=pl.BlockSpec((tm, cols), lambda i: (i, 0)),
        out_shape=jax.ShapeDtypeStruct((n_tok, cols), F32),
        compiler_params=_cparams(("parallel",)),
        name="in_proj",
    )(x2, norm_w.reshape(1, d), w_bf)


def _hgrn_levels(c):
    return [c >> (i + 1) for i in range(int(math.log2(c)))]


def _hgrn_sum_matrices(c, fwd):
    r = np.arange(c)
    m = _HGRN_SMALL_LEVEL
    small = np.zeros((c, c), np.float32)
    for t in range(c):
        p0 = t & ~(2 * m - 1)
        upper = (t & m) != 0
        if fwd:
            if upper:
                small[t, p0 + m:t + 1] = 1.0
            else:
                small[t, t + 1:p0 + m] = 1.0
        else:
            if upper:
                small[t, p0 + m:t] = 1.0
            else:
                small[t, t:p0 + m] = 1.0
    if fwd:
        mats = [r[None, :] <= r[:, None], r[None, :] > r[:, None], small]
    else:
        mats = [r[None, :] >= r[:, None], r[None, :] < r[:, None], small]
    return np.concatenate([np.asarray(x, np.float32) for x in mats], axis=0)


_HGRN_SMALL_LEVEL = 2


def _hgrn_chunk(q, k, lf, v_bf, vt_bf, st, sums_ref, fwd, c):
    dk = q.shape[1]
    hi = lf.astype(BF16)
    lo = (lf - hi.astype(F32)).astype(BF16)
    both = jnp.dot(sums_ref[...], jnp.concatenate([hi, lo], axis=1), preferred_element_type=F32)
    sums = both[:, :dk] + both[:, dk:]
    b = sums[0:c]
    d_out = sums[c:2 * c]
    row = lax.broadcasted_iota(I32, (c, dk), 0)
    rr = lax.broadcasted_iota(I32, (c, c), 0)
    cc = lax.broadcasted_iota(I32, (c, c), 1)
    sep = rr ^ cc
    a = None
    for m in _hgrn_levels(c):
        upper = (row & m) != 0
        is_q = upper if fwd else jnp.logical_not(upper)
        if m == 1:
            d = jnp.where(is_q, lf, 0.0)
        elif m == _HGRN_SMALL_LEVEL:
            d = sums[2 * c:3 * c]
        else:
            blocks = c // (2 * m)
            at = m - 1 if fwd else m
            edge = b.reshape(blocks, 2 * m, dk)[:, at:at + 1, :]
            edge = jnp.broadcast_to(edge, (blocks, 2 * m, dk)).reshape(c, dk)
            d = jnp.where(is_q, b - edge, edge - b)
        r = (jnp.where(is_q, q, k) * jnp.exp2(d)).astype(BF16)
        a_l = lax.dot_general(r, r, _NT, preferred_element_type=F32)
        a = a_l if a is None else jnp.where(sep < 2 * m, a_l, a)
    a_diag = lax.dot_general(q.astype(BF16), k.astype(BF16), _NT, preferred_element_type=F32)
    a = jnp.where(sep < 1, a_diag, a)
    a = jnp.where(rr >= cc if fwd else rr <= cc, a, 0.0)
    e_b = jnp.exp2(b)
    q_in = (q * e_b).astype(BF16)
    k_out = (k * jnp.exp2(d_out)).astype(BF16)
    o = jnp.dot(a.astype(BF16), v_bf, preferred_element_type=F32)
    o = o + lax.dot_general(q_in, st.astype(BF16), _NT, preferred_element_type=F32)
    edge_row = c - 1 if fwd else 0
    st_new = st * e_b[edge_row:edge_row + 1] + jnp.dot(vt_bf, k_out, preferred_element_type=F32)
    return o, st_new


def _hgrn_body(qz_ref, ff_ref, fb_ref, iv_ref, gz_ref, lbf_ref, lbb_ref, nw_ref, sf_ref, sb_ref, o_ref,
               q_s, kf_s, lff_s, kb_s, lfb_s, v_s, vt_s, acc_s, *, c):
    t_len, dk = qz_ref.shape
    nch = t_len // c
    q_s[...] = _silu(qz_ref[...])

    def gate(fz_ref, lb_ref, k_s, lf_s):
        lb = lb_ref[0]
        f = lb + (1.0 - lb) * _sigmoid(fz_ref[...])
        lf_s[...] = jnp.log2(f)
        k_s[...] = 1.0 - f

    gate(ff_ref, lbf_ref, kf_s, lff_s)
    gate(fb_ref, lbb_ref, kb_s, lfb_s)
    v = iv_ref[...]
    v_s[...] = v.astype(BF16)
    for ci in range(nch):
        vt_s[ci] = v[ci * c:(ci + 1) * c, :].T.astype(BF16)

    def run(k_s, lf_s, sums_ref, fwd):
        def step(i, st):
            ci = i if fwd else nch - 1 - i
            rows = pl.ds(pl.multiple_of(ci * c, c), c)
            o, st = _hgrn_chunk(q_s[rows, :], k_s[rows, :], lf_s[rows, :], v_s[rows, :], vt_s[ci],
                                st, sums_ref, fwd, c)
            if fwd:
                acc_s[rows, :] = o
            else:
                acc_s[rows, :] += o
            return st
        lax.fori_loop(0, nch, step, jnp.zeros((dk, dk), F32), unroll=8)

    run(kf_s, lff_s, sf_ref, True)
    run(kb_s, lfb_s, sb_ref, False)
    o = acc_s[...]
    o = o * lax.rsqrt(jnp.mean(o * o, axis=-1, keepdims=True) + EPS) * nw_ref[0]
    o_ref[...] = (o * _silu(gz_ref[...])).astype(o_ref.dtype)


def _hgrn(proj, lb_f, lb_b, norm_w, batch, t_len):
    h, dk, c = HGRN_HEADS, HEAD_DIM, HGRN_CHUNK
    nrows = 3
    sums_f = jnp.asarray(_hgrn_sum_matrices(c, True), BF16)
    sums_b = jnp.asarray(_hgrn_sum_matrices(c, False), BF16)

    def col(group):
        return pl.BlockSpec((t_len, dk), lambda b, hh: (b, group * h + hh))

    def per_head():
        return pl.BlockSpec((1, 1, dk), lambda b, hh: (hh, 0, 0))

    const = pl.BlockSpec((nrows * c, c), lambda b, hh: (0, 0))
    seq = lambda dt: pltpu.VMEM((t_len, dk), dt)
    return pl.pallas_call(
        functools.partial(_hgrn_body, c=c),
        grid=(batch, h),
        in_specs=[col(0), col(1), col(2), col(3), col(4), per_head(), per_head(), per_head(), const, const],
        out_specs=pl.BlockSpec((t_len, dk), lambda b, hh: (b, hh)),
        out_shape=jax.ShapeDtypeStruct((batch * t_len, h * dk), BF16),
        scratch_shapes=[seq(F32), seq(F32), seq(F32), seq(F32), seq(F32), seq(BF16),
                        pltpu.VMEM((t_len // c, dk, c), BF16), seq(F32)],
        compiler_params=_cparams(("parallel", "parallel")),
        name="hgrn",
    )(proj, proj, proj, proj, proj, lb_f.reshape(h, 1, dk), lb_b.reshape(h, 1, dk),
      norm_w.reshape(h, 1, dk), sums_f, sums_b)


CONV_ROWS = 64
CONV_HALO = 16


def _conv_body(cv_ref, cg_ref, w_ref, b_ref, lnw_ref, lnb_ref, o_ref, u_s, y_s):
    t_len, ch = cv_ref.shape
    halo, rows = CONV_HALO, CONV_ROWS
    shift0 = halo - (CONV_LEN - 1) // 2
    win = rows + 2 * halo
    for g in range(ch // LANES):
        lanes = slice(g * LANES, (g + 1) * LANES)
        u_s[g, 0:halo, :] = jnp.zeros((halo, LANES), F32)
        u_s[g, halo + t_len:, :] = jnp.zeros((halo, LANES), F32)
        u_s[g, halo:halo + t_len, :] = cv_ref[:, lanes] * _sigmoid(cg_ref[:, lanes])

    def conv_step(i, carry):
        t0 = pl.multiple_of(i * rows, rows)
        for g in range(ch // LANES):
            lanes = slice(g * LANES, (g + 1) * LANES)
            window = u_s.at[g, pl.ds(t0, win), :]
            acc = jnp.zeros((rows, LANES), F32)
            for j in range(CONV_LEN):
                off = j + shift0
                acc = acc + w_ref[j:j + 1, lanes] * window[off:off + rows, :]
            y_s[pl.ds(t0, rows), lanes] = acc + b_ref[:, lanes]
        return carry

    lax.fori_loop(0, t_len // rows, conv_step, 0)

    def step(i, carry):
        t0 = pl.multiple_of(i * rows, rows)
        y = y_s[pl.ds(t0, rows), :]
        mu = jnp.mean(y, axis=-1, keepdims=True)
        yc = y - mu
        var = jnp.mean(yc * yc, axis=-1, keepdims=True)
        z = yc * lax.rsqrt(var + EPS) * lnw_ref[...] + lnb_ref[...]
        o_ref[pl.ds(t0, rows), :] = _silu(z).astype(o_ref.dtype)
        return carry

    lax.fori_loop(0, t_len // rows, step, 0, unroll=4)


def _conv(proj, dw_w, dw_b, ln_w, ln_b, batch, t_len, first_col_block):
    ch = dw_w.shape[1]
    vec = pl.BlockSpec((1, ch), lambda b: (0, 0))
    return pl.pallas_call(
        _conv_body,
        grid=(batch,),
        in_specs=[pl.BlockSpec((t_len, ch), lambda b: (b, first_col_block)),
                  pl.BlockSpec((t_len, ch), lambda b: (b, first_col_block + 1)),
                  pl.BlockSpec((CONV_LEN, ch), lambda b: (0, 0)), vec, vec, vec],
        out_specs=pl.BlockSpec((t_len, ch), lambda b: (b, 0)),
        out_shape=jax.ShapeDtypeStruct((batch * t_len, ch), BF16),
        scratch_shapes=[pltpu.VMEM((ch // LANES, t_len + 2 * CONV_HALO, LANES), F32),
                        pltpu.VMEM((t_len, ch), F32)],
        compiler_params=_cparams(("parallel",)),
        name="conv",
    )(proj, proj, dw_w, dw_b.reshape(1, ch), ln_w.reshape(1, ch), ln_b.reshape(1, ch))


def _mix_body(a_ref, b_ref, x_ref, wa_ref, wb_ref, nw_ref, wrh_ref, wrl_ref, rb_ref, tri_ref, elow_ref,
              h_ref, n2_ref, dloc_ref, gate_ref, cnt_ref):
    tm = x_ref.shape[0]
    ne = wrh_ref.shape[0]

    h = x_ref[...] + jnp.dot(a_ref[...], wa_ref[...], preferred_element_type=F32) \
        + jnp.dot(b_ref[...], wb_ref[...], preferred_element_type=F32)
    h_ref[...] = h
    n2 = h * lax.rsqrt(jnp.mean(h * h, axis=-1, keepdims=True) + EPS) * nw_ref[...]
    hi = n2.astype(BF16)
    n2_ref[...] = hi
    lo = (n2 - hi.astype(F32)).astype(BF16)
    logits = (lax.dot_general(wrh_ref[...], hi, _NT, preferred_element_type=F32)
              + lax.dot_general(wrh_ref[...], lo, _NT, preferred_element_type=F32)
              + lax.dot_general(wrl_ref[...], hi, _NT, preferred_element_type=F32)
              + rb_ref[...])
    e_iota = lax.broadcasted_iota(I32, (ne, tm), 0)
    work = logits
    sels, vals = [], []
    for _ in range(TOP_K):
        mx = jnp.max(work, axis=0, keepdims=True)
        idx = jnp.min(jnp.where(work == mx, e_iota, ne), axis=0, keepdims=True)
        sel = e_iota == idx
        work = jnp.where(sel, -jnp.inf, work)
        sels.append(sel)
        vals.append(mx)
    exps = [jnp.exp(v - vals[0]) for v in vals]
    denom = exps[0] + exps[1] + exps[2] + exps[3]
    chosen = jnp.zeros((ne, tm), F32)
    for sel in sels:
        chosen = jnp.where(sel, 1.0, chosen)
    chosen_bf = chosen.astype(BF16)
    prior = jnp.dot(chosen_bf, tri_ref[...], preferred_element_type=F32)
    cnt = jnp.dot(chosen_bf, jnp.ones((tm, LANES), BF16), preferred_element_type=F32)
    cnt_ref[0] = cnt
    share = jnp.floor((cnt + (SUBLANES - 1)) * (1.0 / SUBLANES)) * SUBLANES
    lower = jnp.dot(elow_ref[...], share.astype(BF16), preferred_element_type=F32)
    slab_row = prior + jnp.concatenate([lower] * (tm // LANES), axis=1)
    for k in range(TOP_K):
        dloc_ref[k:k + 1, :] = jnp.sum(jnp.where(sels[k], slab_row, 0.0), axis=0, keepdims=True).astype(I32)
        gate_ref[k:k + 1, :] = exps[k] / denom


def _mix(a, b, x2, w_out_bf, norm_w, router_w, router_b, tm=ROUTE_TILE):
    n_tok, d = x2.shape
    wa, wb = w_out_bf[:a.shape[1]], w_out_bf[a.shape[1]:]
    ne = router_w.shape[1]
    wr_t = router_w.T
    wr_hi = wr_t.astype(BF16)
    wr_lo = (wr_t - wr_hi.astype(F32)).astype(BF16)
    tri = jnp.asarray(np.triu(np.ones((tm, tm), np.float32), k=1), BF16)
    e_lower = jnp.asarray(np.tril(np.ones((ne, ne), np.float32), k=-1), BF16)
    const = lambda shape: pl.BlockSpec(shape, lambda i: tuple(0 for _ in shape))
    return pl.pallas_call(
        _mix_body,
        grid=(n_tok // tm,),
        in_specs=[pl.BlockSpec((tm, a.shape[1]), lambda i: (i, 0)),
                  pl.BlockSpec((tm, b.shape[1]), lambda i: (i, 0)),
                  pl.BlockSpec((tm, d), lambda i: (i, 0)),
                  const(wa.shape), const(wb.shape), const((1, d)),
                  const((ne, d)), const((ne, d)), const((ne, 1)), const((tm, tm)), const((ne, ne))],
        out_specs=[pl.BlockSpec((tm, d), lambda i: (i, 0)),
                   pl.BlockSpec((tm, d), lambda i: (i, 0)),
                   pl.BlockSpec((TOP_K, tm), lambda i: (0, i)),
                   pl.BlockSpec((TOP_K, tm), lambda i: (0, i)),
                   pl.BlockSpec((1, ne, LANES), lambda i: (i, 0, 0))],
        out_shape=[jax.ShapeDtypeStruct((n_tok, d), F32),
                   jax.ShapeDtypeStruct((n_tok, d), BF16),
                   jax.ShapeDtypeStruct((TOP_K, n_tok), I32),
                   jax.ShapeDtypeStruct((TOP_K, n_tok), F32),
                   jax.ShapeDtypeStruct((n_tok // tm, ne, LANES), F32)],
        compiler_params=_cparams(("parallel",)),
        name="mix_router",
    )(a, b, x2, wa, wb, norm_w.reshape(1, d), wr_hi, wr_lo, router_b.reshape(ne, 1), tri, e_lower)


SLAB_ROWS = TOP_K * ROUTE_TILE + N_EXPERTS * SUBLANES
SLAB_CHUNK = 256
SLAB_PIECES = (64, 32, 16, 8, 4, 2, 1)


def _for_each_piece(ngroups, fn):
    for size in SLAB_PIECES:
        first = lax.bitwise_and(ngroups, ~(2 * size - 1))

        @pl.when(lax.bitwise_and(ngroups, size) != 0)
        def _():
            fn(first, size)


def _group_rows(first_group, groups):
    return pl.ds(pl.multiple_of(first_group * SUBLANES, SUBLANES), groups * SUBLANES)


def _wait_groups(ngroups, stage_slot, hbm, sem):
    for size in (256, 128) + SLAB_PIECES:
        @pl.when(lax.bitwise_and(ngroups, size) != 0)
        def _():
            pltpu.make_async_copy(hbm.at[_group_rows(0, size)], stage_slot.at[_group_rows(0, size)], sem).wait()


def _dispatch_body(src_ref, dst_ref, ngrp_ref, tot_ref, n2_ref, dloc_ref, xs_hbm, stage, sem):
    step = pl.program_id(0)
    tm = n2_ref.shape[0]
    slot = step % 2

    def shares(at_step, at_slot, act):
        def per_expert(e, carry):
            idx = at_step * N_EXPERTS + e
            src, dst = src_ref[idx], dst_ref[idx]

            def piece(first, size):
                act(pltpu.make_async_copy(stage.at[at_slot, _group_rows(src + first, size)],
                                          xs_hbm.at[_group_rows(dst + first, size)], sem.at[at_slot]))

            _for_each_piece(ngrp_ref[idx], piece)
            return carry

        lax.fori_loop(0, N_EXPERTS, per_expert, 0)

    def wait_slab(at_step, at_slot):
        _wait_groups(tot_ref[at_step], stage.at[at_slot], xs_hbm, sem.at[at_slot])

    @pl.when(step >= 2)
    def _():
        wait_slab(step - 2, slot)

    n2 = n2_ref[...]
    dloc = dloc_ref[...]
    for c in range(SLAB_ROWS // SLAB_CHUNK):
        rows = lax.broadcasted_iota(I32, (SLAB_CHUNK, tm), 0) + c * SLAB_CHUNK
        pick = jnp.zeros((SLAB_CHUNK, tm), F32)
        for k in range(TOP_K):
            pick = pick + jnp.where(dloc[k:k + 1, :] == rows, 1.0, 0.0)
        stage[slot, c * SLAB_CHUNK:(c + 1) * SLAB_CHUNK, :] = jnp.dot(pick.astype(BF16), n2,
                                                                       preferred_element_type=F32)
    shares(step, slot, lambda cp: cp.start())

    @pl.when(step == pl.num_programs(0) - 1)
    def _():
        wait_slab(step, slot)

        @pl.when(step >= 1)
        def _():
            wait_slab(step - 1, 1 - slot)


def _dispatch(shares, n2_bf, dloc, n_rows):
    n_tok, d = n2_bf.shape
    tm = ROUTE_TILE
    return pl.pallas_call(
        _dispatch_body,
        grid_spec=pltpu.PrefetchScalarGridSpec(
            num_scalar_prefetch=4,
            grid=(n_tok // tm,),
            in_specs=[pl.BlockSpec((tm, d), lambda i, *_: (i, 0)),
                      pl.BlockSpec((TOP_K, tm), lambda i, *_: (0, i))],
            out_specs=pl.BlockSpec(memory_space=pl.ANY),
            scratch_shapes=[pltpu.VMEM((2, SLAB_ROWS, d), F32), pltpu.SemaphoreType.DMA((2,))]),
        out_shape=jax.ShapeDtypeStruct((n_rows, d), F32),
        compiler_params=_cparams(("arbitrary",)),
        name="dispatch",
    )(*shares, n2_bf, dloc)


FFN_CAST_ROWS = 64


def _ffn_body(be_ref, first_ref, slot_ref, next_ref, nact_ref, xs_ref, w1_hbm, b1_ref, w2_hbm, b2_ref, ys_ref,
              w1_f, w2_f, w1_b, w2_b, sem):
    i = pl.program_id(0)

    def fetch(expert, slot):
        return (pltpu.make_async_copy(w1_hbm.at[expert], w1_f.at[slot], sem.at[0, slot]),
                pltpu.make_async_copy(w2_hbm.at[expert], w2_f.at[slot], sem.at[1, slot]))

    @pl.when(i < nact_ref[0])
    def _():
        slot = slot_ref[i]

        @pl.when(first_ref[i] == 1)
        def _():
            @pl.when(i == 0)
            def _():
                for cp in fetch(be_ref[0], slot):
                    cp.start()

            for cp in fetch(be_ref[i], slot):
                cp.wait()

            @pl.when(next_ref[i] >= 0)
            def _():
                for cp in fetch(next_ref[i], 1 - slot):
                    cp.start()

            def cast(src, dst):
                def body(c, carry):
                    rows = pl.ds(pl.multiple_of(c * FFN_CAST_ROWS, FFN_CAST_ROWS), FFN_CAST_ROWS)
                    dst[rows, :] = src[slot, rows, :].astype(BF16)
                    return carry
                lax.fori_loop(0, dst.shape[0] // FFN_CAST_ROWS, body, 0)

            cast(w1_f, w1_b)
            cast(w2_f, w2_b)

        x = xs_ref[...].astype(BF16)
        hdn = jnp.dot(x, w1_b[...], preferred_element_type=F32) + b1_ref[0]
        d_ff = hdn.shape[1] // 2
        glu = jnp.minimum(hdn[:, :d_ff], SWIGLU_LIMIT)
        lin = jnp.clip(hdn[:, d_ff:], -SWIGLU_LIMIT, SWIGLU_LIMIT)
        act = glu * _sigmoid(SWIGLU_ALPHA * glu) * (lin + 1.0)
        y = jnp.dot(act.astype(BF16), w2_b[...], preferred_element_type=F32) + b2_ref[0]
        ys_ref[...] = y


def _ffn(block_expert, n_active, xs, w1, b1, w2, b2):
    ne, d, f2 = w1.shape
    nb = xs.shape[0] // MOE_BLOCK
    block = (MOE_BLOCK, d)
    idx = jnp.arange(nb, dtype=I32)
    active = idx < n_active[0]
    prev = jnp.concatenate([block_expert[:1] - 1, block_expert[:-1]])
    first = (active & (block_expert != prev)).astype(I32)
    run_slot = (jnp.sum(jnp.where(idx[None, :] <= idx[:, None], first[None, :], 0), axis=1) - 1) & 1
    later_first = (first[None, :] == 1) & (idx[None, :] > idx[:, None])
    next_block = jnp.min(jnp.where(later_first, idx[None, :], nb), axis=1)
    next_expert = jnp.where(next_block < nb, block_expert[jnp.minimum(next_block, nb - 1)], -1).astype(I32)

    def blk(i, be, fi, sl, nx, nact):
        return (jnp.minimum(i, nact[0] - 1), 0)

    def exp(i, be, fi, sl, nx, nact):
        return (be[jnp.minimum(i, nact[0] - 1)], 0, 0)

    return pl.pallas_call(
        _ffn_body,
        grid_spec=pltpu.PrefetchScalarGridSpec(
            num_scalar_prefetch=5,
            grid=(nb,),
            in_specs=[pl.BlockSpec(block, blk),
                      pl.BlockSpec(memory_space=pl.ANY),
                      pl.BlockSpec((1, 1, f2), exp),
                      pl.BlockSpec(memory_space=pl.ANY),
                      pl.BlockSpec((1, 1, d), exp)],
            out_specs=pl.BlockSpec(block, blk),
            scratch_shapes=[pltpu.VMEM((2, d, f2), F32), pltpu.VMEM((2, f2 // 2, d), F32),
                            pltpu.VMEM((d, f2), BF16), pltpu.VMEM((f2 // 2, d), BF16),
                            pltpu.SemaphoreType.DMA((2, 2))]),
        out_shape=jax.ShapeDtypeStruct(xs.shape, F32),
        compiler_params=_cparams(("arbitrary",)),
        name="expert_ffn",
    )(block_expert, first, run_slot.astype(I32), next_expert, n_active, xs, w1, b1.reshape(ne, 1, f2), w2,
      b2.reshape(ne, 1, d))


def _combine_body(src_ref, dst_ref, ngrp_ref, tot_ref, h_ref, dloc_ref, gate_ref, nw_ref, ys_hbm, o_ref, stage,
                  sem):
    step = pl.program_id(0)
    tm = h_ref.shape[0]
    slot = step % 2

    def shares(at_step, at_slot, act):
        def per_expert(e, carry):
            idx = at_step * N_EXPERTS + e
            src, dst = src_ref[idx], dst_ref[idx]

            def piece(first, size):
                act(pltpu.make_async_copy(ys_hbm.at[_group_rows(dst + first, size)],
                                          stage.at[at_slot, _group_rows(src + first, size)], sem.at[at_slot]))

            _for_each_piece(ngrp_ref[idx], piece)
            return carry

        lax.fori_loop(0, N_EXPERTS, per_expert, 0)

    @pl.when(step == 0)
    def _():
        stage[...] = jnp.zeros(stage.shape, F32)
        shares(0, 0, lambda cp: cp.start())

    @pl.when(step + 1 < pl.num_programs(0))
    def _():
        shares(step + 1, 1 - slot, lambda cp: cp.start())

    _wait_groups(tot_ref[step], stage.at[slot], ys_hbm, sem.at[slot])
    dloc = dloc_ref[...]
    gates = gate_ref[...]
    moe = jnp.zeros(o_ref.shape, F32)
    for c in range(SLAB_ROWS // SLAB_CHUNK):
        cols = lax.broadcasted_iota(I32, (tm, SLAB_CHUNK), 1) + c * SLAB_CHUNK
        weight = jnp.zeros((tm, SLAB_CHUNK), F32)
        for k in range(TOP_K):
            weight = weight + jnp.where(dloc[:, k:k + 1] == cols, gates[:, k:k + 1], 0.0)
        rows = stage[slot, c * SLAB_CHUNK:(c + 1) * SLAB_CHUNK, :].astype(BF16)
        moe = moe + jnp.dot(weight.astype(BF16), rows, preferred_element_type=F32)
    y = h_ref[...] + moe
    o_ref[...] = y * lax.rsqrt(jnp.mean(y * y, axis=-1, keepdims=True) + EPS) * nw_ref[...]


def _combine(shares, h, dloc_tk, gates_tk, norm_w, ys):
    n_tok, d = h.shape
    tm = ROUTE_TILE
    return pl.pallas_call(
        _combine_body,
        grid_spec=pltpu.PrefetchScalarGridSpec(
            num_scalar_prefetch=4,
            grid=(n_tok // tm,),
            in_specs=[pl.BlockSpec((tm, d), lambda i, *_: (i, 0)),
                      pl.BlockSpec((tm, TOP_K), lambda i, *_: (i, 0)),
                      pl.BlockSpec((tm, TOP_K), lambda i, *_: (i, 0)),
                      pl.BlockSpec((1, d), lambda i, *_: (0, 0)),
                      pl.BlockSpec(memory_space=pl.ANY)],
            out_specs=pl.BlockSpec((tm, d), lambda i, *_: (i, 0)),
            scratch_shapes=[pltpu.VMEM((2, SLAB_ROWS, d), F32), pltpu.SemaphoreType.DMA((2,))]),
        out_shape=jax.ShapeDtypeStruct((n_tok, d), F32),
        compiler_params=_cparams(("arbitrary",)),
        name="combine_norm",
    )(*shares, h, dloc_tk, gates_tk, norm_w.reshape(1, d), ys)


def kernel(x, norm1_w, w_in, lb_logits, hgrn_norm_w, dw_w, dw_b, conv_ln_w, conv_ln_b, w_out, norm2_w,
           router_w, router_b, w1, b1, w2, b2, final_norm_w):
    batch, t_len, d = x.shape
    assert w_in.shape[0] == 1, "single-layer block"
    n_tok = batch * t_len
    hk = HGRN_HEADS * HEAD_DIM
    conv_ch = dw_w.shape[2]
    lb_table = jnp.cumsum(jax.nn.softmax(lb_logits.astype(F32), axis=1), axis=1)
    x2 = x.reshape(n_tok, d)
    proj = _in_proj(x2, norm1_w[0], w_in[0].astype(BF16))
    a = _hgrn(proj, lb_table[0, 0], lb_table[1, 0], hgrn_norm_w[0], batch, t_len)
    b = _conv(proj, dw_w[0], dw_b[0], conv_ln_w[0], conv_ln_b[0], batch, t_len, (5 * hk) // conv_ch)
    h_mid, n2_bf, dloc, gate, cnt_tiles = _mix(a, b, x2, w_out[0].astype(BF16), norm2_w[0],
                                                     router_w[0], router_b[0])
    nt = n_tok // ROUTE_TILE
    e_ids = jnp.arange(N_EXPERTS, dtype=I32)
    t_ids = jnp.arange(nt, dtype=I32)
    share = (cnt_tiles[:, :, 0].astype(I32) + SUBLANES - 1) // SUBLANES * SUBLANES
    in_slab = jnp.sum(jnp.where(e_ids[None, None, :] < e_ids[None, :, None], share[:, None, :], 0), axis=2)
    in_expert = jnp.sum(jnp.where((t_ids[None, :] < t_ids[:, None])[:, :, None], share[None, :, :], 0), axis=1)
    padded = (jnp.sum(share, axis=0) + MOE_BLOCK - 1) // MOE_BLOCK * MOE_BLOCK
    pad_end = jnp.sum(jnp.where(e_ids[None, :] <= e_ids[:, None], padded[None, :], 0), axis=1)
    in_xs = (pad_end - padded)[None, :] + in_expert
    n_blocks = -(-(n_tok * TOP_K + nt * N_EXPERTS * (SUBLANES - 1)) // MOE_BLOCK) + N_EXPERTS
    block_start = jnp.arange(n_blocks, dtype=I32) * MOE_BLOCK
    block_expert = jnp.minimum(jnp.sum((pad_end[None, :] <= block_start[:, None]).astype(I32), axis=1),
                               N_EXPERTS - 1)
    n_active = pad_end[-1:] // MOE_BLOCK
    shares = [(v // SUBLANES).reshape(-1) for v in (in_slab, in_xs, share)] + [jnp.sum(share, axis=1) // SUBLANES]
    xs = _dispatch(shares, n2_bf, dloc, n_blocks * MOE_BLOCK)
    ys = _ffn(block_expert, n_active, xs, w1[0], b1[0], w2[0], b2[0])
    out = _combine(shares, h_mid, dloc.T, gate.T, final_norm_w, ys)
    return out.reshape(batch, t_len, d)
```

```python
import functools
import math

import numpy as np
import jax
import jax.numpy as jnp
from jax import lax
from jax.experimental import pallas as pl
from jax.experimental.pallas import tpu as pltpu

F32 = jnp.float32
BF16 = jnp.bfloat16
I32 = jnp.int32

EPS = 1e-5
HGRN_HEADS = 4
HEAD_DIM = 128
HGRN_CHUNK = 128
CONV_LEN = 31
N_EXPERTS = 32
TOP_K = 4
SWIGLU_LIMIT = 7.0
SWIGLU_ALPHA = 1.702
MOE_BLOCK = 256
ROUTE_TILE = 512
LANES = 128
SUBLANES = 8
VMEM_LIMIT = 56 << 20

_NT = (((1,), (1,)), ((), ()))


def _sigmoid(x):
    return 0.5 * jnp.tanh(0.5 * x) + 0.5


def _silu(x):
    return x * _sigmoid(x)


def _cparams(sem):
    return pltpu.CompilerParams(dimension_semantics=sem, vmem_limit_bytes=VMEM_LIMIT)


def _inproj_body(x_ref, nw_ref, w_ref, o_ref):
    x = x_ref[...]
    n = x * lax.rsqrt(jnp.mean(x * x, axis=-1, keepdims=True) + EPS) * nw_ref[...]
    o_ref[...] = jnp.dot(n.astype(BF16), w_ref[...], preferred_element_type=F32)


def _in_proj(x2, norm_w, w_bf, tm=512):
    n_tok, d = x2.shape
    cols = w_bf.shape[1]
    return pl.pallas_call(
        _inproj_body,
        grid=(n_tok // tm,),
        in_specs=[pl.BlockSpec((tm, d), lambda i: (i, 0)),
                  pl.BlockSpec((1, d), lambda i: (0, 0)),
                  pl.BlockSpec((d, cols), lambda i: (0, 0))],
        out_specs=pl.BlockSpec((tm, cols), lambda i: (i, 0)),
        out_shape=jax.ShapeDtypeStruct((n_tok, cols), F32),
        compiler_params=_cparams(("parallel",)),
        name="in_proj",
    )(x2, norm_w.reshape(1, d), w_bf)


def _hgrn_levels(c):
    return [c >> (i + 1) for i in range(int(math.log2(c)))]


def _hgrn_sum_matrices(c, fwd):
    r = np.arange(c)
    m = _HGRN_SMALL_LEVEL
    small = np.zeros((c, c), np.float32)
    for t in range(c):
        p0 = t & ~(2 * m - 1)
        upper = (t & m) != 0
        if fwd:
            if upper:
                small[t, p0 + m:t + 1] = 1.0
            else:
                small[t, t + 1:p0 + m] = 1.0
        else:
            if upper:
                small[t, p0 + m:t] = 1.0
            else:
                small[t, t:p0 + m] = 1.0
    if fwd:
        mats = [r[None, :] <= r[:, None], r[None, :] > r[:, None], small]
    else:
        mats = [r[None, :] >= r[:, None], r[None, :] < r[:, None], small]
    return np.concatenate([np.asarray(x, np.float32) for x in mats], axis=0)


_HGRN_SMALL_LEVEL = 2


def _hgrn_chunk(q, k, lf, v_bf, vt_bf, st, sums_ref, fwd, c):
    dk = q.shape[1]
    hi = lf.astype(BF16)
    lo = (lf - hi.astype(F32)).astype(BF16)
    both = jnp.dot(sums_ref[...], jnp.concatenate([hi, lo], axis=1), preferred_element_type=F32)
    sums = both[:, :dk] + both[:, dk:]
    b = sums[0:c]
    d_out = sums[c:2 * c]
    row = lax.broadcasted_iota(I32, (c, dk), 0)
    rr = lax.broadcasted_iota(I32, (c, c), 0)
    cc = lax.broadcasted_iota(I32, (c, c), 1)
    sep = rr ^ cc
    a = None
    for m in _hgrn_levels(c):
        upper = (row & m) != 0
        is_q = upper if fwd else jnp.logical_not(upper)
        if m == 1:
            d = jnp.where(is_q, lf, 0.0)
        elif m == _HGRN_SMALL_LEVEL:
            d = sums[2 * c:3 * c]
        else:
            blocks = c // (2 * m)
            at = m - 1 if fwd else m
            edge = b.reshape(blocks, 2 * m, dk)[:, at:at + 1, :]
            edge = jnp.broadcast_to(edge, (blocks, 2 * m, dk)).reshape(c, dk)
            d = jnp.where(is_q, b - edge, edge - b)
        r = (jnp.where(is_q, q, k) * jnp.exp2(d)).astype(BF16)
        a_l = lax.dot_general(r, r, _NT, preferred_element_type=F32)
        a = a_l if a is None else jnp.where(sep < 2 * m, a_l, a)
    a_diag = lax.dot_general(q.astype(BF16), k.astype(BF16), _NT, preferred_element_type=F32)
    a = jnp.where(sep < 1, a_diag, a)
    a = jnp.where(rr >= cc if fwd else rr <= cc, a, 0.0)
    e_b = jnp.exp2(b)
    q_in = (q * e_b).astype(BF16)
    k_out = (k * jnp.exp2(d_out)).astype(BF16)
    o = jnp.dot(a.astype(BF16), v_bf, preferred_element_type=F32)
    o = o + lax.dot_general(q_in, st.astype(BF16), _NT, preferred_element_type=F32)
    edge_row = c - 1 if fwd else 0
    st_new = st * e_b[edge_row:edge_row + 1] + jnp.dot(vt_bf, k_out, preferred_element_type=F32)
    return o, st_new


def _hgrn_body(qz_ref, ff_ref, fb_ref, iv_ref, gz_ref, lbf_ref, lbb_ref, nw_ref, sf_ref, sb_ref, o_ref,
               q_s, kf_s, lff_s, kb_s, lfb_s, v_s, vt_s, acc_s, *, c):
    t_len, dk = qz_ref.shape
    nch = t_len // c
    q_s[...] = _silu(qz_ref[...])

    def gate(fz_ref, lb_ref, k_s, lf_s):
        lb = lb_ref[0]
        f = lb + (1.0 - lb) * _sigmoid(fz_ref[...])
        lf_s[...] = jnp.log2(f)
        k_s[...] = 1.0 - f

    gate(ff_ref, lbf_ref, kf_s, lff_s)
    gate(fb_ref, lbb_ref, kb_s, lfb_s)
    v = iv_ref[...]
    v_s[...] = v.astype(BF16)
    for ci in range(nch):
        vt_s[ci] = v[ci * c:(ci + 1) * c, :].T.astype(BF16)

    def run(k_s, lf_s, sums_ref, fwd):
        def step(i, st):
            ci = i if fwd else nch - 1 - i
            rows = pl.ds(pl.multiple_of(ci * c, c), c)
            o, st = _hgrn_chunk(q_s[rows, :], k_s[rows, :], lf_s[rows, :], v_s[rows, :], vt_s[ci],
                                st, sums_ref, fwd, c)
            if fwd:
                acc_s[rows, :] = o
            else:
                acc_s[rows, :] += o
            return st
        lax.fori_loop(0, nch, step, jnp.zeros((dk, dk), F32), unroll=True)

    run(kf_s, lff_s, sf_ref, True)
    run(kb_s, lfb_s, sb_ref, False)
    o = acc_s[...]
    o = o * lax.rsqrt(jnp.mean(o * o, axis=-1, keepdims=True) + EPS) * nw_ref[0]
    o_ref[...] = (o * _silu(gz_ref[...])).astype(o_ref.dtype)


def _hgrn(proj, lb_f, lb_b, norm_w, batch, t_len):
    h, dk, c = HGRN_HEADS, HEAD_DIM, HGRN_CHUNK
    nrows = 3
    sums_f = jnp.asarray(_hgrn_sum_matrices(c, True), BF16)
    sums_b = jnp.asarray(_hgrn_sum_matrices(c, False), BF16)

    def col(group):
        return pl.BlockSpec((t_len, dk), lambda b, hh: (b, group * h + hh))

    def per_head():
        return pl.BlockSpec((1, 1, dk), lambda b, hh: (hh, 0, 0))

    const = pl.BlockSpec((nrows * c, c), lambda b, hh: (0, 0))
    seq = lambda dt: pltpu.VMEM((t_len, dk), dt)
    return pl.pallas_call(
        functools.partial(_hgrn_body, c=c),
        grid=(batch, h),
        in_specs=[col(0), col(1), col(2), col(3), col(4), per_head(), per_head(), per_head(), const, const],
        out_specs=pl.BlockSpec((t_len, dk), lambda b, hh: (b, hh)),
        out_shape=jax.ShapeDtypeStruct((batch * t_len, h * dk), BF16),
        scratch_shapes=[seq(F32), seq(F32), seq(F32), seq(F32), seq(F32), seq(BF16),
                        pltpu.VMEM((t_len // c, dk, c), BF16), seq(F32)],
        compiler_params=_cparams(("parallel", "parallel")),
        name="hgrn",
    )(proj, proj, proj, proj, proj, lb_f.reshape(h, 1, dk), lb_b.reshape(h, 1, dk),
      norm_w.reshape(h, 1, dk), sums_f, sums_b)


CONV_ROWS = 64
CONV_HALO = 16


def _conv_body(cv_ref, cg_ref, w_ref, b_ref, lnw_ref, lnb_ref, o_ref, u_s, y_s):
    t_len, ch = cv_ref.shape
    halo, rows = CONV_HALO, CONV_ROWS
    shift0 = halo - (CONV_LEN - 1) // 2
    win = rows + 2 * halo
    for g in range(ch // LANES):
        lanes = slice(g * LANES, (g + 1) * LANES)
        u_s[g, 0:halo, :] = jnp.zeros((halo, LANES), F32)
        u_s[g, halo + t_len:, :] = jnp.zeros((halo, LANES), F32)
        u_s[g, halo:halo + t_len, :] = cv_ref[:, lanes] * _sigmoid(cg_ref[:, lanes])

    def conv_step(i, carry):
        t0 = pl.multiple_of(i * rows, rows)
        for g in range(ch // LANES):
            lanes = slice(g * LANES, (g + 1) * LANES)
            window = u_s.at[g, pl.ds(t0, win), :]
            acc = jnp.zeros((rows, LANES), F32)
            for j in range(CONV_LEN):
                off = j + shift0
                acc = acc + w_ref[j:j + 1, lanes] * window[off:off + rows, :]
            y_s[pl.ds(t0, rows), lanes] = acc + b_ref[:, lanes]
        return carry

    lax.fori_loop(0, t_len // rows, conv_step, 0)

    def step(i, carry):
        t0 = pl.multiple_of(i * rows, rows)
        y = y_s[pl.ds(t0, rows), :]
        mu = jnp.mean(y, axis=-1, keepdims=True)
        yc = y - mu
        var = jnp.mean(yc * yc, axis=-1, keepdims=True)
        z = yc * lax.rsqrt(var + EPS) * lnw_ref[...] + lnb_ref[...]
        o_ref[pl.ds(t0, rows), :] = _silu(z).astype(o_ref.dtype)
        return carry

    lax.fori_loop(0, t_len // rows, step, 0, unroll=4)


def _conv(proj, dw_w, dw_b, ln_w, ln_b, batch, t_len, first_col_block):
    ch = dw_w.shape[1]
    vec = pl.BlockSpec((1, ch), lambda b: (0, 0))
    return pl.pallas_call(
        _conv_body,
        grid=(batch,),
        in_specs=[pl.BlockSpec((t_len, ch), lambda b: (b, first_col_block)),
                  pl.BlockSpec((t_len, ch), lambda b: (b, first_col_block + 1)),
                  pl.BlockSpec((CONV_LEN, ch), lambda b: (0, 0)), vec, vec, vec],
        out_specs=pl.BlockSpec((t_len, ch), lambda b: (b, 0)),
        out_shape=jax.ShapeDtypeStruct((batch * t_len, ch), BF16),
        scratch_shapes=[pltpu.VMEM((ch // LANES, t_len + 2 * CONV_HALO, LANES), F32),
                        pltpu.VMEM((t_len, ch), F32)],
        compiler_params=_cparams(("parallel",)),
        name="conv",
    )(proj, proj, dw_w, dw_b.reshape(1, ch), ln_w.reshape(1, ch), ln_b.reshape(1, ch))


def _mix_body(a_ref, b_ref, x_ref, wa_ref, wb_ref, nw_ref, wr_ref, rb_ref, tri_ref, elow_ref,
              h_ref, n2_ref, dloc_ref, gate_ref, cnt_ref):
    tm = x_ref.shape[0]
    ne = wr_ref.shape[0] // 2

    h = x_ref[...] + jnp.dot(a_ref[...], wa_ref[...], preferred_element_type=F32) \
        + jnp.dot(b_ref[...], wb_ref[...], preferred_element_type=F32)
    h_ref[...] = h
    n2 = h * lax.rsqrt(jnp.mean(h * h, axis=-1, keepdims=True) + EPS) * nw_ref[...]
    hi = n2.astype(BF16)
    n2_ref[...] = hi
    lo = (n2 - hi.astype(F32)).astype(BF16)
    by_hi = lax.dot_general(wr_ref[...], hi, _NT, preferred_element_type=F32)
    by_lo = lax.dot_general(wr_ref[0:ne, :], lo, _NT, preferred_element_type=F32)
    logits = by_hi[0:ne] + by_hi[ne:] + by_lo + rb_ref[...]
    e_iota = lax.broadcasted_iota(I32, (ne, tm), 0)
    work = logits
    sels, vals = [], []
    for _ in range(TOP_K):
        mx = jnp.max(work, axis=0, keepdims=True)
        idx = jnp.min(jnp.where(work == mx, e_iota, ne), axis=0, keepdims=True)
        sel = e_iota == idx
        work = jnp.where(sel, -jnp.inf, work)
        sels.append(sel)
        vals.append(mx)
    exps = [jnp.exp(v - vals[0]) for v in vals]
    denom = exps[0] + exps[1] + exps[2] + exps[3]
    chosen = jnp.zeros((ne, tm), F32)
    for sel in sels:
        chosen = jnp.where(sel, 1.0, chosen)
    chosen_bf = chosen.astype(BF16)
    prior = jnp.dot(chosen_bf, tri_ref[...], preferred_element_type=F32)
    cnt = jnp.dot(chosen_bf, jnp.ones((tm, LANES), BF16), preferred_element_type=F32)
    cnt_ref[0] = cnt
    share = jnp.floor((cnt + (SUBLANES - 1)) * (1.0 / SUBLANES)) * SUBLANES
    lower = jnp.dot(elow_ref[...], share.astype(BF16), preferred_element_type=F32)
    slab_row = prior + jnp.concatenate([lower] * (tm // LANES), axis=1)
    for k in range(TOP_K):
        dloc_ref[k:k + 1, :] = jnp.sum(jnp.where(sels[k], slab_row, 0.0), axis=0, keepdims=True).astype(I32)
        gate_ref[k:k + 1, :] = exps[k] / denom


def _mix(a, b, x2, w_out_bf, norm_w, router_w, router_b, tm=ROUTE_TILE):
    n_tok, d = x2.shape
    wa, wb = w_out_bf[:a.shape[1]], w_out_bf[a.shape[1]:]
    ne = router_w.shape[1]
    wr_t = router_w.T
    wr_hi = wr_t.astype(BF16)
    wr_lo = (wr_t - wr_hi.astype(F32)).astype(BF16)
    tri = jnp.asarray(np.triu(np.ones((tm, tm), np.float32), k=1), BF16)
    e_lower = jnp.asarray(np.tril(np.ones((ne, ne), np.float32), k=-1), BF16)
    const = lambda shape: pl.BlockSpec(shape, lambda i: tuple(0 for _ in shape))
    return pl.pallas_call(
        _mix_body,
        grid=(n_tok // tm,),
        in_specs=[pl.BlockSpec((tm, a.shape[1]), lambda i: (i, 0)),
                  pl.BlockSpec((tm, b.shape[1]), lambda i: (i, 0)),
                  pl.BlockSpec((tm, d), lambda i: (i, 0)),
                  const(wa.shape), const(wb.shape), const((1, d)),
                  const((2 * ne, d)), const((ne, 1)), const((tm, tm)), const((ne, ne))],
        out_specs=[pl.BlockSpec((tm, d), lambda i: (i, 0)),
                   pl.BlockSpec((tm, d), lambda i: (i, 0)),
                   pl.BlockSpec((TOP_K, tm), lambda i: (0, i)),
                   pl.BlockSpec((TOP_K, tm), lambda i: (0, i)),
                   pl.BlockSpec((1, ne, LANES), lambda i: (i, 0, 0))],
        out_shape=[jax.ShapeDtypeStruct((n_tok, d), F32),
                   jax.ShapeDtypeStruct((n_tok, d), BF16),
                   jax.ShapeDtypeStruct((TOP_K, n_tok), I32),
                   jax.ShapeDtypeStruct((TOP_K, n_tok), F32),
                   jax.ShapeDtypeStruct((n_tok // tm, ne, LANES), F32)],
        compiler_params=_cparams(("parallel",)),
        name="mix_router",
    )(a, b, x2, wa, wb, norm_w.reshape(1, d), jnp.concatenate([wr_hi, wr_lo], axis=0), router_b.reshape(ne, 1),
      tri, e_lower)


SLAB_ROWS = TOP_K * ROUTE_TILE + N_EXPERTS * SUBLANES
SLAB_CHUNK = 256
SLAB_PIECES = (64, 32, 16, 8, 4, 2, 1)
SLAB_COMMON = ROUTE_TILE * TOP_K // N_EXPERTS // SUBLANES


def _for_each_piece(ngroups, fn):
    def pieces(sizes):
        for size in sizes:
            first = lax.bitwise_and(ngroups, ~(2 * size - 1))

            @pl.when(lax.bitwise_and(ngroups, size) != 0)
            def _():
                fn(first, size)

    rare = tuple(s for s in SLAB_PIECES if s >= 2 * SLAB_COMMON)

    @pl.when(ngroups >= min(rare))
    def _():
        pieces(rare)

    pieces(tuple(s for s in SLAB_PIECES if s < 2 * SLAB_COMMON))


def _group_rows(first_group, groups):
    return pl.ds(pl.multiple_of(first_group * SUBLANES, SUBLANES), groups * SUBLANES)


def _wait_groups(ngroups, stage_slot, hbm, sem):
    for size in (256, 128) + SLAB_PIECES:
        @pl.when(lax.bitwise_and(ngroups, size) != 0)
        def _():
            pltpu.make_async_copy(hbm.at[_group_rows(0, size)], stage_slot.at[_group_rows(0, size)], sem).wait()


def _dispatch_body(src_ref, dst_ref, ngrp_ref, tot_ref, n2_ref, dloc_ref, xs_hbm, stage, sem):
    step = pl.program_id(0)
    tm = n2_ref.shape[0]
    slot = step % 2

    def shares(at_step, at_slot, act):
        def per_expert(e, carry):
            idx = at_step * N_EXPERTS + e
            src, dst = src_ref[idx], dst_ref[idx]

            def piece(first, size):
                act(pltpu.make_async_copy(stage.at[at_slot, _group_rows(src + first, size)],
                                          xs_hbm.at[_group_rows(dst + first, size)], sem.at[at_slot]))

            _for_each_piece(ngrp_ref[idx], piece)
            return carry

        lax.fori_loop(0, N_EXPERTS, per_expert, 0)

    def wait_slab(at_step, at_slot):
        _wait_groups(tot_ref[at_step], stage.at[at_slot], xs_hbm, sem.at[at_slot])

    @pl.when(step >= 2)
    def _():
        wait_slab(step - 2, slot)

    n2 = n2_ref[...]
    dloc = dloc_ref[...]
    for c in range(SLAB_ROWS // SLAB_CHUNK):
        rows = lax.broadcasted_iota(I32, (SLAB_CHUNK, tm), 0) + c * SLAB_CHUNK
        pick = jnp.zeros((SLAB_CHUNK, tm), F32)
        for k in range(TOP_K):
            pick = pick + jnp.where(dloc[k:k + 1, :] == rows, 1.0, 0.0)
        stage[slot, c * SLAB_CHUNK:(c + 1) * SLAB_CHUNK, :] = jnp.dot(pick.astype(BF16), n2,
                                                                       preferred_element_type=F32)
    shares(step, slot, lambda cp: cp.start())

    @pl.when(step == pl.num_programs(0) - 1)
    def _():
        wait_slab(step, slot)

        @pl.when(step >= 1)
        def _():
            wait_slab(step - 1, 1 - slot)


def _dispatch(shares, n2_bf, dloc, n_rows):
    n_tok, d = n2_bf.shape
    tm = ROUTE_TILE
    return pl.pallas_call(
        _dispatch_body,
        grid_spec=pltpu.PrefetchScalarGridSpec(
            num_scalar_prefetch=4,
            grid=(n_tok // tm,),
            in_specs=[pl.BlockSpec((tm, d), lambda i, *_: (i, 0)),
                      pl.BlockSpec((TOP_K, tm), lambda i, *_: (0, i))],
            out_specs=pl.BlockSpec(memory_space=pl.ANY),
            scratch_shapes=[pltpu.VMEM((2, SLAB_ROWS, d), F32), pltpu.SemaphoreType.DMA((2,))]),
        out_shape=jax.ShapeDtypeStruct((n_rows, d), F32),
        compiler_params=_cparams(("arbitrary",)),
        name="dispatch",
    )(*shares, n2_bf, dloc)


FFN_CAST_ROWS = 64


def _ffn_body(be_ref, first_ref, slot_ref, next_ref, nact_ref, xs_ref, w1_hbm, b1_ref, w2_hbm, b2_ref, ys_ref,
              w1_f, w2_f, w1_b, w2_b, sem):
    i = pl.program_id(0)

    def fetch(expert, slot):
        return (pltpu.make_async_copy(w1_hbm.at[expert], w1_f.at[slot], sem.at[0, slot]),
                pltpu.make_async_copy(w2_hbm.at[expert], w2_f.at[slot], sem.at[1, slot]))

    @pl.when(i < nact_ref[0])
    def _():
        slot = slot_ref[i]

        @pl.when(first_ref[i] == 1)
        def _():
            @pl.when(i == 0)
            def _():
                for cp in fetch(be_ref[0], slot):
                    cp.start()

            for cp in fetch(be_ref[i], slot):
                cp.wait()

            @pl.when(next_ref[i] >= 0)
            def _():
                for cp in fetch(next_ref[i], 1 - slot):
                    cp.start()

            def cast(src, dst):
                def body(c, carry):
                    rows = pl.ds(pl.multiple_of(c * FFN_CAST_ROWS, FFN_CAST_ROWS), FFN_CAST_ROWS)
                    dst[rows, :] = src[slot, rows, :].astype(BF16)
                    return carry
                lax.fori_loop(0, dst.shape[0] // FFN_CAST_ROWS, body, 0)

            cast(w1_f, w1_b)
            cast(w2_f, w2_b)

        x = xs_ref[...].astype(BF16)
        hdn = jnp.dot(x, w1_b[...], preferred_element_type=F32) + b1_ref[0]
        d_ff = hdn.shape[1] // 2
        glu = jnp.minimum(hdn[:, :d_ff], SWIGLU_LIMIT)
        lin = jnp.clip(hdn[:, d_ff:], -SWIGLU_LIMIT, SWIGLU_LIMIT)
        act = glu * _sigmoid(SWIGLU_ALPHA * glu) * (lin + 1.0)
        y = jnp.dot(act.astype(BF16), w2_b[...], preferred_element_type=F32) + b2_ref[0]
        ys_ref[...] = y


def _ffn(block_expert, n_active, xs, w1, b1, w2, b2):
    ne, d, f2 = w1.shape
    nb = xs.shape[0] // MOE_BLOCK
    block = (MOE_BLOCK, d)
    idx = jnp.arange(nb, dtype=I32)
    active = idx < n_active[0]
    prev = jnp.concatenate([block_expert[:1] - 1, block_expert[:-1]])
    first = (active & (block_expert != prev)).astype(I32)
    run_slot = (jnp.sum(jnp.where(idx[None, :] <= idx[:, None], first[None, :], 0), axis=1) - 1) & 1
    later_first = (first[None, :] == 1) & (idx[None, :] > idx[:, None])
    next_block = jnp.min(jnp.where(later_first, idx[None, :], nb), axis=1)
    next_expert = jnp.where(next_block < nb, block_expert[jnp.minimum(next_block, nb - 1)], -1).astype(I32)

    def blk(i, be, fi, sl, nx, nact):
        return (jnp.minimum(i, nact[0] - 1), 0)

    def exp(i, be, fi, sl, nx, nact):
        return (be[jnp.minimum(i, nact[0] - 1)], 0, 0)

    return pl.pallas_call(
        _ffn_body,
        grid_spec=pltpu.PrefetchScalarGridSpec(
            num_scalar_prefetch=5,
            grid=(nb,),
            in_specs=[pl.BlockSpec(block, blk),
                      pl.BlockSpec(memory_space=pl.ANY),
                      pl.BlockSpec((1, 1, f2), exp),
                      pl.BlockSpec(memory_space=pl.ANY),
                      pl.BlockSpec((1, 1, d), exp)],
            out_specs=pl.BlockSpec(block, blk),
            scratch_shapes=[pltpu.VMEM((2, d, f2), F32), pltpu.VMEM((2, f2 // 2, d), F32),
                            pltpu.VMEM((d, f2), BF16), pltpu.VMEM((f2 // 2, d), BF16),
                            pltpu.SemaphoreType.DMA((2, 2))]),
        out_shape=jax.ShapeDtypeStruct(xs.shape, F32),
        compiler_params=_cparams(("arbitrary",)),
        name="expert_ffn",
    )(block_expert, first, run_slot.astype(I32), next_expert, n_active, xs, w1, b1.reshape(ne, 1, f2), w2,
      b2.reshape(ne, 1, d))


def _combine_body(src_ref, dst_ref, ngrp_ref, tot_ref, h_ref, dloc_ref, gate_ref, nw_ref, ys_hbm, o_ref, stage,
                  sem):
    step = pl.program_id(0)
    tm = h_ref.shape[0]
    slot = step % 2

    def shares(at_step, at_slot, act):
        def per_expert(e, carry):
            idx = at_step * N_EXPERTS + e
            src, dst = src_ref[idx], dst_ref[idx]

            def piece(first, size):
                act(pltpu.make_async_copy(ys_hbm.at[_group_rows(dst + first, size)],
                                          stage.at[at_slot, _group_rows(src + first, size)], sem.at[at_slot]))

            _for_each_piece(ngrp_ref[idx], piece)
            return carry

        lax.fori_loop(0, N_EXPERTS, per_expert, 0)

    @pl.when(step == 0)
    def _():
        stage[...] = jnp.zeros(stage.shape, F32)
        shares(0, 0, lambda cp: cp.start())

    @pl.when(step + 1 < pl.num_programs(0))
    def _():
        shares(step + 1, 1 - slot, lambda cp: cp.start())

    _wait_groups(tot_ref[step], stage.at[slot], ys_hbm, sem.at[slot])
    dloc = dloc_ref[...]
    gates = gate_ref[...]
    moe = jnp.zeros(o_ref.shape, F32)
    for c in range(SLAB_ROWS // SLAB_CHUNK):
        cols = lax.broadcasted_iota(I32, (tm, SLAB_CHUNK), 1) + c * SLAB_CHUNK
        weight = jnp.zeros((tm, SLAB_CHUNK), F32)
        for k in range(TOP_K):
            weight = weight + jnp.where(dloc[:, k:k + 1] == cols, gates[:, k:k + 1], 0.0)
        rows = stage[slot, c * SLAB_CHUNK:(c + 1) * SLAB_CHUNK, :].astype(BF16)
        moe = moe + jnp.dot(weight.astype(BF16), rows, preferred_element_type=F32)
    y = h_ref[...] + moe
    o_ref[...] = y * lax.rsqrt(jnp.mean(y * y, axis=-1, keepdims=True) + EPS) * nw_ref[...]


def _combine(shares, h, dloc_tk, gates_tk, norm_w, ys):
    n_tok, d = h.shape
    tm = ROUTE_TILE
    return pl.pallas_call(
        _combine_body,
        grid_spec=pltpu.PrefetchScalarGridSpec(
            num_scalar_prefetch=4,
            grid=(n_tok // tm,),
            in_specs=[pl.BlockSpec((tm, d), lambda i, *_: (i, 0)),
                      pl.BlockSpec((tm, TOP_K), lambda i, *_: (i, 0)),
                      pl.BlockSpec((tm, TOP_K), lambda i, *_: (i, 0)),
                      pl.BlockSpec((1, d), lambda i, *_: (0, 0)),
                      pl.BlockSpec(memory_space=pl.ANY)],
            out_specs=pl.BlockSpec((tm, d), lambda i, *_: (i, 0)),
            scratch_shapes=[pltpu.VMEM((2, SLAB_ROWS, d), F32), pltpu.SemaphoreType.DMA((2,))]),
        out_shape=jax.ShapeDtypeStruct((n_tok, d), F32),
        compiler_params=_cparams(("arbitrary",)),
        name="combine_norm",
    )(*shares, h, dloc_tk, gates_tk, norm_w.reshape(1, d), ys)


def kernel(x, norm1_w, w_in, lb_logits, hgrn_norm_w, dw_w, dw_b, conv_ln_w, conv_ln_b, w_out, norm2_w,
           router_w, router_b, w1, b1, w2, b2, final_norm_w):
    batch, t_len, d = x.shape
    assert w_in.shape[0] == 1, "single-layer block"
    n_tok = batch * t_len
    hk = HGRN_HEADS * HEAD_DIM
    conv_ch = dw_w.shape[2]
    lb_table = jnp.cumsum(jax.nn.softmax(lb_logits.astype(F32), axis=1), axis=1)
    x2 = x.reshape(n_tok, d)
    proj = _in_proj(x2, norm1_w[0], w_in[0].astype(BF16))
    a = _hgrn(proj, lb_table[0, 0], lb_table[1, 0], hgrn_norm_w[0], batch, t_len)
    b = _conv(proj, dw_w[0], dw_b[0], conv_ln_w[0], conv_ln_b[0], batch, t_len, (5 * hk) // conv_ch)
    h_mid, n2_bf, dloc, gate, cnt_tiles = _mix(a, b, x2, w_out[0].astype(BF16), norm2_w[0],
                                                     router_w[0], router_b[0])
    nt = n_tok // ROUTE_TILE
    e_ids = jnp.arange(N_EXPERTS, dtype=I32)
    t_ids = jnp.arange(nt, dtype=I32)
    share = (cnt_tiles[:, :, 0].astype(I32) + SUBLANES - 1) // SUBLANES * SUBLANES
    in_slab = jnp.sum(jnp.where(e_ids[None, None, :] < e_ids[None, :, None], share[:, None, :], 0), axis=2)
    in_expert = jnp.sum(jnp.where((t_ids[None, :] < t_ids[:, None])[:, :, None], share[None, :, :], 0), axis=1)
    padded = (jnp.sum(share, axis=0) + MOE_BLOCK - 1) // MOE_BLOCK * MOE_BLOCK
    pad_end = jnp.sum(jnp.where(e_ids[None, :] <= e_ids[:, None], padded[None, :], 0), axis=1)
    in_xs = (pad_end - padded)[None, :] + in_expert
    n_blocks = -(-(n_tok * TOP_K + nt * N_EXPERTS * (SUBLANES - 1)) // MOE_BLOCK) + N_EXPERTS
    block_start = jnp.arange(n_blocks, dtype=I32) * MOE_BLOCK
    block_expert = jnp.minimum(jnp.sum((pad_end[None, :] <= block_start[:, None]).astype(I32), axis=1),
                               N_EXPERTS - 1)
    n_active = pad_end[-1:] // MOE_BLOCK
    shares = [(v // SUBLANES).reshape(-1) for v in (in_slab, in_xs, share)] + [jnp.sum(share, axis=1) // SUBLANES]
    xs = _dispatch(shares, n2_bf, dloc, n_blocks * MOE_BLOCK)
    ys = _ffn(block_expert, n_active, xs, w1[0], b1[0], w2[0], b2[0])
    out = _combine(shares, h_mid, dloc.T, gate.T, final_norm_w, ys)
    return out.reshape(batch, t_len, d)
```

```python
import functools
import math

import numpy as np
import jax
import jax.numpy as jnp
from jax import lax
from jax.experimental import pallas as pl
from jax.experimental.pallas import tpu as pltpu

F32 = jnp.float32
BF16 = jnp.bfloat16
I32 = jnp.int32

EPS = 1e-5
HGRN_HEADS = 4
HEAD_DIM = 128
HGRN_CHUNK = 256
CONV_LEN = 31
N_EXPERTS = 32
TOP_K = 4
SWIGLU_LIMIT = 7.0
SWIGLU_ALPHA = 1.702
MOE_BLOCK = 256
ROUTE_TILE = 512
LANES = 128
SUBLANES = 8
VMEM_LIMIT = 56 << 20

_NT = (((1,), (1,)), ((), ()))


def _sigmoid(x):
    return 0.5 * jnp.tanh(0.5 * x) + 0.5


def _silu(x):
    return x * _sigmoid(x)


def _cparams(sem):
    return pltpu.CompilerParams(dimension_semantics=sem, vmem_limit_bytes=VMEM_LIMIT)


def _inproj_body(x_ref, nw_ref, w_ref, o_ref):
    x = x_ref[...]
    n = x * lax.rsqrt(jnp.mean(x * x, axis=-1, keepdims=True) + EPS) * nw_ref[...]
    o_ref[...] = jnp.dot(n.astype(BF16), w_ref[...], preferred_element_type=F32)


def _in_proj(x2, norm_w, w_bf, tm=512):
    n_tok, d = x2.shape
    cols = w_bf.shape[1]
    return pl.pallas_call(
        _inproj_body,
        grid=(n_tok // tm,),
        in_specs=[pl.BlockSpec((tm, d), lambda i: (i, 0)),
                  pl.BlockSpec((1, d), lambda i: (0, 0)),
                  pl.BlockSpec((d, cols), lambda i: (0, 0))],
        out_specs=pl.BlockSpec((tm, cols), lambda i: (i, 0)),
        out_shape=jax.ShapeDtypeStruct((n_tok, cols), F32),
        compiler_params=_cparams(("parallel",)),
        name="in_proj",
    )(x2, norm_w.reshape(1, d), w_bf)


def _hgrn_levels(c):
    return [c >> (i + 1) for i in range(int(math.log2(c)))]


def _hgrn_sum_matrices(c, fwd):
    r = np.arange(c)
    m = _HGRN_SMALL_LEVEL
    small = np.zeros((c, c), np.float32)
    for t in range(c):
        p0 = t & ~(2 * m - 1)
        upper = (t & m) != 0
        if fwd:
            if upper:
                small[t, p0 + m:t + 1] = 1.0
            else:
                small[t, t + 1:p0 + m] = 1.0
        else:
            if upper:
                small[t, p0 + m:t] = 1.0
            else:
                small[t, t:p0 + m] = 1.0
    if fwd:
        mats = [r[None, :] <= r[:, None], r[None, :] > r[:, None], small]
    else:
        mats = [r[None, :] >= r[:, None], r[None, :] < r[:, None], small]
    return np.concatenate([np.asarray(x, np.float32) for x in mats], axis=0)


_HGRN_SMALL_LEVEL = 2


def _hgrn_chunk(q, k, lf, v_bf, vt_bf, st, sums_ref, fwd, c):
    dk = q.shape[1]
    hi = lf.astype(BF16)
    lo = (lf - hi.astype(F32)).astype(BF16)
    both = jnp.dot(sums_ref[...], jnp.concatenate([hi, lo], axis=1), preferred_element_type=F32)
    sums = both[:, :dk] + both[:, dk:]
    b = sums[0:c]
    d_out = sums[c:2 * c]
    row = lax.broadcasted_iota(I32, (c, dk), 0)
    rr = lax.broadcasted_iota(I32, (c, c), 0)
    cc = lax.broadcasted_iota(I32, (c, c), 1)
    sep = rr ^ cc
    a = None
    for m in _hgrn_levels(c):
        upper = (row & m) != 0
        is_q = upper if fwd else jnp.logical_not(upper)
        if m == 1:
            d = jnp.where(is_q, lf, 0.0)
        elif m == _HGRN_SMALL_LEVEL:
            d = sums[2 * c:3 * c]
        else:
            blocks = c // (2 * m)
            at = m - 1 if fwd else m
            edge = b.reshape(blocks, 2 * m, dk)[:, at:at + 1, :]
            edge = jnp.broadcast_to(edge, (blocks, 2 * m, dk)).reshape(c, dk)
            d = jnp.where(is_q, b - edge, edge - b)
        r = (jnp.where(is_q, q, k) * jnp.exp2(d)).astype(BF16)
        a_l = lax.dot_general(r, r, _NT, preferred_element_type=F32)
        a = a_l if a is None else jnp.where(sep < 2 * m, a_l, a)
    a_diag = lax.dot_general(q.astype(BF16), k.astype(BF16), _NT, preferred_element_type=F32)
    a = jnp.where(sep < 1, a_diag, a)
    a = jnp.where(rr >= cc if fwd else rr <= cc, a, 0.0)
    e_b = jnp.exp2(b)
    q_in = (q * e_b).astype(BF16)
    k_out = (k * jnp.exp2(d_out)).astype(BF16)
    o = jnp.dot(a.astype(BF16), v_bf, preferred_element_type=F32)
    o = o + lax.dot_general(q_in, st.astype(BF16), _NT, preferred_element_type=F32)
    edge_row = c - 1 if fwd else 0
    st_new = st * e_b[edge_row:edge_row + 1] + jnp.dot(vt_bf, k_out, preferred_element_type=F32)
    return o, st_new


def _hgrn_body(qz_ref, ff_ref, fb_ref, iv_ref, gz_ref, lbf_ref, lbb_ref, nw_ref, sf_ref, sb_ref, o_ref,
               q_s, kf_s, lff_s, kb_s, lfb_s, v_s, vt_s, acc_s, *, c):
    t_len, dk = qz_ref.shape
    nch = t_len // c
    q_s[...] = _silu(qz_ref[...])

    def gate(fz_ref, lb_ref, k_s, lf_s):
        lb = lb_ref[0]
        f = lb + (1.0 - lb) * _sigmoid(fz_ref[...])
        lf_s[...] = jnp.log2(f)
        k_s[...] = 1.0 - f

    gate(ff_ref, lbf_ref, kf_s, lff_s)
    gate(fb_ref, lbb_ref, kb_s, lfb_s)
    v = iv_ref[...]
    v_s[...] = v.astype(BF16)
    for ci in range(nch):
        vt_s[ci] = v[ci * c:(ci + 1) * c, :].T.astype(BF16)

    def run(k_s, lf_s, sums_ref, fwd):
        def step(i, st):
            ci = i if fwd else nch - 1 - i
            rows = pl.ds(pl.multiple_of(ci * c, c), c)
            o, st = _hgrn_chunk(q_s[rows, :], k_s[rows, :], lf_s[rows, :], v_s[rows, :], vt_s[ci],
                                st, sums_ref, fwd, c)
            if fwd:
                acc_s[rows, :] = o
            else:
                acc_s[rows, :] += o
            return st
        lax.fori_loop(0, nch, step, jnp.zeros((dk, dk), F32), unroll=True)

    run(kf_s, lff_s, sf_ref, True)
    run(kb_s, lfb_s, sb_ref, False)
    o = acc_s[...]
    o = o * lax.rsqrt(jnp.mean(o * o, axis=-1, keepdims=True) + EPS) * nw_ref[0]
    o_ref[...] = (o * _silu(gz_ref[...])).astype(o_ref.dtype)


def _hgrn(proj, lb_f, lb_b, norm_w, batch, t_len):
    h, dk, c = HGRN_HEADS, HEAD_DIM, HGRN_CHUNK
    nrows = 3
    sums_f = jnp.asarray(_hgrn_sum_matrices(c, True), BF16)
    sums_b = jnp.asarray(_hgrn_sum_matrices(c, False), BF16)

    def col(group):
        return pl.BlockSpec((t_len, dk), lambda b, hh: (b, group * h + hh))

    def per_head():
        return pl.BlockSpec((1, 1, dk), lambda b, hh: (hh, 0, 0))

    const = pl.BlockSpec((nrows * c, c), lambda b, hh: (0, 0))
    seq = lambda dt: pltpu.VMEM((t_len, dk), dt)
    return pl.pallas_call(
        functools.partial(_hgrn_body, c=c),
        grid=(batch, h),
        in_specs=[col(0), col(1), col(2), col(3), col(4), per_head(), per_head(), per_head(), const, const],
        out_specs=pl.BlockSpec((t_len, dk), lambda b, hh: (b, hh)),
        out_shape=jax.ShapeDtypeStruct((batch * t_len, h * dk), BF16),
        scratch_shapes=[seq(F32), seq(F32), seq(F32), seq(F32), seq(F32), seq(BF16),
                        pltpu.VMEM((t_len // c, dk, c), BF16), seq(F32)],
        compiler_params=_cparams(("parallel", "parallel")),
        name="hgrn",
    )(proj, proj, proj, proj, proj, lb_f.reshape(h, 1, dk), lb_b.reshape(h, 1, dk),
      norm_w.reshape(h, 1, dk), sums_f, sums_b)


CONV_ROWS = 64
CONV_HALO = 16


def _conv_body(cv_ref, cg_ref, w_ref, b_ref, lnw_ref, lnb_ref, o_ref, u_s, y_s):
    t_len, ch = cv_ref.shape
    halo, rows = CONV_HALO, CONV_ROWS
    shift0 = halo - (CONV_LEN - 1) // 2
    win = rows + 2 * halo
    for g in range(ch // LANES):
        lanes = slice(g * LANES, (g + 1) * LANES)
        u_s[g, 0:halo, :] = jnp.zeros((halo, LANES), F32)
        u_s[g, halo + t_len:, :] = jnp.zeros((halo, LANES), F32)
        u_s[g, halo:halo + t_len, :] = cv_ref[:, lanes] * _sigmoid(cg_ref[:, lanes])

    def conv_step(i, carry):
        t0 = pl.multiple_of(i * rows, rows)
        for g in range(ch // LANES):
            lanes = slice(g * LANES, (g + 1) * LANES)
            window = u_s.at[g, pl.ds(t0, win), :]
            acc = jnp.zeros((rows, LANES), F32)
            for j in range(CONV_LEN):
                off = j + shift0
                acc = acc + w_ref[j:j + 1, lanes] * window[off:off + rows, :]
            y_s[pl.ds(t0, rows), lanes] = acc + b_ref[:, lanes]
        return carry

    lax.fori_loop(0, t_len // rows, conv_step, 0)

    def step(i, carry):
        t0 = pl.multiple_of(i * rows, rows)
        y = y_s[pl.ds(t0, rows), :]
        mu = jnp.mean(y, axis=-1, keepdims=True)
        yc = y - mu
        var = jnp.mean(yc * yc, axis=-1, keepdims=True)
        z = yc * lax.rsqrt(var + EPS) * lnw_ref[...] + lnb_ref[...]
        o_ref[pl.ds(t0, rows), :] = _silu(z).astype(o_ref.dtype)
        return carry

    lax.fori_loop(0, t_len // rows, step, 0, unroll=4)


def _conv(proj, dw_w, dw_b, ln_w, ln_b, batch, t_len, first_col_block):
    ch = dw_w.shape[1]
    vec = pl.BlockSpec((1, ch), lambda b: (0, 0))
    return pl.pallas_call(
        _conv_body,
        grid=(batch,),
        in_specs=[pl.BlockSpec((t_len, ch), lambda b: (b, first_col_block)),
                  pl.BlockSpec((t_len, ch), lambda b: (b, first_col_block + 1)),
                  pl.BlockSpec((CONV_LEN, ch), lambda b: (0, 0)), vec, vec, vec],
        out_specs=pl.BlockSpec((t_len, ch), lambda b: (b, 0)),
        out_shape=jax.ShapeDtypeStruct((batch * t_len, ch), BF16),
        scratch_shapes=[pltpu.VMEM((ch // LANES, t_len + 2 * CONV_HALO, LANES), F32),
                        pltpu.VMEM((t_len, ch), F32)],
        compiler_params=_cparams(("parallel",)),
        name="conv",
    )(proj, proj, dw_w, dw_b.reshape(1, ch), ln_w.reshape(1, ch), ln_b.reshape(1, ch))


def _mix_body(a_ref, b_ref, x_ref, wa_ref, wb_ref, nw_ref, wr_ref, rb_ref, tri_ref, elow_ref,
              h_ref, n2_ref, dloc_ref, gate_ref, cnt_ref):
    tm = x_ref.shape[0]
    ne = wr_ref.shape[0] // 2

    h = x_ref[...] + jnp.dot(a_ref[...], wa_ref[...], preferred_element_type=F32) \
        + jnp.dot(b_ref[...], wb_ref[...], preferred_element_type=F32)
    h_ref[...] = h
    n2 = h * lax.rsqrt(jnp.mean(h * h, axis=-1, keepdims=True) + EPS) * nw_ref[...]
    hi = n2.astype(BF16)
    n2_ref[...] = hi
    lo = (n2 - hi.astype(F32)).astype(BF16)
    by_hi = lax.dot_general(wr_ref[...], hi, _NT, preferred_element_type=F32)
    by_lo = lax.dot_general(wr_ref[0:ne, :], lo, _NT, preferred_element_type=F32)
    logits = by_hi[0:ne] + by_hi[ne:] + by_lo + rb_ref[...]
    e_iota = lax.broadcasted_iota(I32, (ne, tm), 0)
    work = logits
    sels, vals = [], []
    for _ in range(TOP_K):
        mx = jnp.max(work, axis=0, keepdims=True)
        idx = jnp.min(jnp.where(work == mx, e_iota, ne), axis=0, keepdims=True)
        sel = e_iota == idx
        work = jnp.where(sel, -jnp.inf, work)
        sels.append(sel)
        vals.append(mx)
    exps = [jnp.exp(v - vals[0]) for v in vals]
    denom = exps[0] + exps[1] + exps[2] + exps[3]
    chosen = jnp.zeros((ne, tm), F32)
    for sel in sels:
        chosen = jnp.where(sel, 1.0, chosen)
    chosen_bf = chosen.astype(BF16)
    prior = jnp.dot(chosen_bf, tri_ref[...], preferred_element_type=F32)
    cnt = jnp.dot(chosen_bf, jnp.ones((tm, LANES), BF16), preferred_element_type=F32)
    cnt_ref[0] = cnt
    share = jnp.floor((cnt + (SUBLANES - 1)) * (1.0 / SUBLANES)) * SUBLANES
    lower = jnp.dot(elow_ref[...], share.astype(BF16), preferred_element_type=F32)
    slab_row = prior + jnp.concatenate([lower] * (tm // LANES), axis=1)
    for k in range(TOP_K):
        dloc_ref[k:k + 1, :] = jnp.sum(jnp.where(sels[k], slab_row, 0.0), axis=0, keepdims=True).astype(I32)
        gate_ref[k:k + 1, :] = exps[k] / denom


def _mix(a, b, x2, w_out_bf, norm_w, router_w, router_b, tm=ROUTE_TILE):
    n_tok, d = x2.shape
    wa, wb = w_out_bf[:a.shape[1]], w_out_bf[a.shape[1]:]
    ne = router_w.shape[1]
    wr_t = router_w.T
    wr_hi = wr_t.astype(BF16)
    wr_lo = (wr_t - wr_hi.astype(F32)).astype(BF16)
    tri = jnp.asarray(np.triu(np.ones((tm, tm), np.float32), k=1), BF16)
    e_lower = jnp.asarray(np.tril(np.ones((ne, ne), np.float32), k=-1), BF16)
    const = lambda shape: pl.BlockSpec(shape, lambda i: tuple(0 for _ in shape))
    return pl.pallas_call(
        _mix_body,
        grid=(n_tok // tm,),
        in_specs=[pl.BlockSpec((tm, a.shape[1]), lambda i: (i, 0)),
                  pl.BlockSpec((tm, b.shape[1]), lambda i: (i, 0)),
                  pl.BlockSpec((tm, d), lambda i: (i, 0)),
                  const(wa.shape), const(wb.shape), const((1, d)),
                  const((2 * ne, d)), const((ne, 1)), const((tm, tm)), const((ne, ne))],
        out_specs=[pl.BlockSpec((tm, d), lambda i: (i, 0)),
                   pl.BlockSpec((tm, d), lambda i: (i, 0)),
                   pl.BlockSpec((TOP_K, tm), lambda i: (0, i)),
                   pl.BlockSpec((TOP_K, tm), lambda i: (0, i)),
                   pl.BlockSpec((1, ne, LANES), lambda i: (i, 0, 0))],
        out_shape=[jax.ShapeDtypeStruct((n_tok, d), F32),
                   jax.ShapeDtypeStruct((n_tok, d), BF16),
                   jax.ShapeDtypeStruct((TOP_K, n_tok), I32),
                   jax.ShapeDtypeStruct((TOP_K, n_tok), F32),
                   jax.ShapeDtypeStruct((n_tok // tm, ne, LANES), F32)],
        compiler_params=_cparams(("parallel",)),
        name="mix_router",
    )(a, b, x2, wa, wb, norm_w.reshape(1, d), jnp.concatenate([wr_hi, wr_lo], axis=0), router_b.reshape(ne, 1),
      tri, e_lower)


SLAB_ROWS = TOP_K * ROUTE_TILE + N_EXPERTS * SUBLANES
SLAB_CHUNK = 256
SLAB_PIECES = (64, 32, 16, 8, 4, 2, 1)
SLAB_COMMON = ROUTE_TILE * TOP_K // N_EXPERTS // SUBLANES


def _for_each_piece(ngroups, fn):
    def pieces(sizes):
        for size in sizes:
            first = lax.bitwise_and(ngroups, ~(2 * size - 1))

            @pl.when(lax.bitwise_and(ngroups, size) != 0)
            def _():
                fn(first, size)

    rare = tuple(s for s in SLAB_PIECES if s >= 2 * SLAB_COMMON)

    @pl.when(ngroups >= min(rare))
    def _():
        pieces(rare)

    pieces(tuple(s for s in SLAB_PIECES if s < 2 * SLAB_COMMON))


def _group_rows(first_group, groups):
    return pl.ds(pl.multiple_of(first_group * SUBLANES, SUBLANES), groups * SUBLANES)


def _wait_groups(ngroups, stage_slot, hbm, sem):
    for size in (256, 128) + SLAB_PIECES:
        @pl.when(lax.bitwise_and(ngroups, size) != 0)
        def _():
            pltpu.make_async_copy(hbm.at[_group_rows(0, size)], stage_slot.at[_group_rows(0, size)], sem).wait()


def _dispatch_body(src_ref, dst_ref, ngrp_ref, tot_ref, n2_ref, dloc_ref, xs_hbm, stage, sem):
    step = pl.program_id(0)
    tm = n2_ref.shape[0]
    slot = step % 2

    def shares(at_step, at_slot, act):
        def per_expert(e, carry):
            idx = at_step * N_EXPERTS + e
            src, dst = src_ref[idx], dst_ref[idx]

            def piece(first, size):
                act(pltpu.make_async_copy(stage.at[at_slot, _group_rows(src + first, size)],
                                          xs_hbm.at[_group_rows(dst + first, size)], sem.at[at_slot]))

            _for_each_piece(ngrp_ref[idx], piece)
            return carry

        lax.fori_loop(0, N_EXPERTS, per_expert, 0)

    def wait_slab(at_step, at_slot):
        _wait_groups(tot_ref[at_step], stage.at[at_slot], xs_hbm, sem.at[at_slot])

    @pl.when(step >= 2)
    def _():
        wait_slab(step - 2, slot)

    n2 = n2_ref[...]
    dloc = dloc_ref[...]
    for c in range(SLAB_ROWS // SLAB_CHUNK):
        rows = lax.broadcasted_iota(I32, (SLAB_CHUNK, tm), 0) + c * SLAB_CHUNK
        pick = jnp.zeros((SLAB_CHUNK, tm), F32)
        for k in range(TOP_K):
            pick = pick + jnp.where(dloc[k:k + 1, :] == rows, 1.0, 0.0)
        stage[slot, c * SLAB_CHUNK:(c + 1) * SLAB_CHUNK, :] = jnp.dot(pick.astype(BF16), n2,
                                                                       preferred_element_type=F32)
    shares(step, slot, lambda cp: cp.start())

    @pl.when(step == pl.num_programs(0) - 1)
    def _():
        wait_slab(step, slot)

        @pl.when(step >= 1)
        def _():
            wait_slab(step - 1, 1 - slot)


def _dispatch(shares, n2_bf, dloc, n_rows):
    n_tok, d = n2_bf.shape
    tm = ROUTE_TILE
    return pl.pallas_call(
        _dispatch_body,
        grid_spec=pltpu.PrefetchScalarGridSpec(
            num_scalar_prefetch=4,
            grid=(n_tok // tm,),
            in_specs=[pl.BlockSpec((tm, d), lambda i, *_: (i, 0)),
                      pl.BlockSpec((TOP_K, tm), lambda i, *_: (0, i))],
            out_specs=pl.BlockSpec(memory_space=pl.ANY),
            scratch_shapes=[pltpu.VMEM((2, SLAB_ROWS, d), F32), pltpu.SemaphoreType.DMA((2,))]),
        out_shape=jax.ShapeDtypeStruct((n_rows, d), F32),
        compiler_params=_cparams(("arbitrary",)),
        name="dispatch",
    )(*shares, n2_bf, dloc)


FFN_CAST_ROWS = 64


def _ffn_body(be_ref, first_ref, slot_ref, next_ref, nact_ref, xs_ref, w1_hbm, b1_ref, w2_hbm, b2_ref, ys_ref,
              w1_f, w2_f, w1_b, w2_b, sem):
    i = pl.program_id(0)

    def fetch(expert, slot):
        return (pltpu.make_async_copy(w1_hbm.at[expert], w1_f.at[slot], sem.at[0, slot]),
                pltpu.make_async_copy(w2_hbm.at[expert], w2_f.at[slot], sem.at[1, slot]))

    @pl.when(i < nact_ref[0])
    def _():
        slot = slot_ref[i]

        @pl.when(first_ref[i] == 1)
        def _():
            @pl.when(i == 0)
            def _():
                for cp in fetch(be_ref[0], slot):
                    cp.start()

            for cp in fetch(be_ref[i], slot):
                cp.wait()

            @pl.when(next_ref[i] >= 0)
            def _():
                for cp in fetch(next_ref[i], 1 - slot):
                    cp.start()

            def cast(src, dst):
                def body(c, carry):
                    rows = pl.ds(pl.multiple_of(c * FFN_CAST_ROWS, FFN_CAST_ROWS), FFN_CAST_ROWS)
                    dst[rows, :] = src[slot, rows, :].astype(BF16)
                    return carry
                lax.fori_loop(0, dst.shape[0] // FFN_CAST_ROWS, body, 0)

            cast(w1_f, w1_b)
            cast(w2_f, w2_b)

        x = xs_ref[...].astype(BF16)
        hdn = jnp.dot(x, w1_b[...], preferred_element_type=F32) + b1_ref[0]
        d_ff = hdn.shape[1] // 2
        glu = jnp.minimum(hdn[:, :d_ff], SWIGLU_LIMIT)
        lin = jnp.clip(hdn[:, d_ff:], -SWIGLU_LIMIT, SWIGLU_LIMIT)
        act = glu * _sigmoid(SWIGLU_ALPHA * glu) * (lin + 1.0)
        y = jnp.dot(act.astype(BF16), w2_b[...], preferred_element_type=F32) + b2_ref[0]
        ys_ref[...] = y


def _ffn(block_expert, n_active, xs, w1, b1, w2, b2):
    ne, d, f2 = w1.shape
    nb = xs.shape[0] // MOE_BLOCK
    block = (MOE_BLOCK, d)
    idx = jnp.arange(nb, dtype=I32)
    active = idx < n_active[0]
    prev = jnp.concatenate([block_expert[:1] - 1, block_expert[:-1]])
    first = (active & (block_expert != prev)).astype(I32)
    run_slot = (jnp.sum(jnp.where(idx[None, :] <= idx[:, None], first[None, :], 0), axis=1) - 1) & 1
    later_first = (first[None, :] == 1) & (idx[None, :] > idx[:, None])
    next_block = jnp.min(jnp.where(later_first, idx[None, :], nb), axis=1)
    next_expert = jnp.where(next_block < nb, block_expert[jnp.minimum(next_block, nb - 1)], -1).astype(I32)

    def blk(i, be, fi, sl, nx, nact):
        return (jnp.minimum(i, nact[0] - 1), 0)

    def exp(i, be, fi, sl, nx, nact):
        return (be[jnp.minimum(i, nact[0] - 1)], 0, 0)

    return pl.pallas_call(
        _ffn_body,
        grid_spec=pltpu.PrefetchScalarGridSpec(
            num_scalar_prefetch=5,
            grid=(nb,),
            in_specs=[pl.BlockSpec(block, blk),
                      pl.BlockSpec(memory_space=pl.ANY),
                      pl.BlockSpec((1, 1, f2), exp),
                      pl.BlockSpec(memory_space=pl.ANY),
                      pl.BlockSpec((1, 1, d), exp)],
            out_specs=pl.BlockSpec(block, blk),
            scratch_shapes=[pltpu.VMEM((2, d, f2), F32), pltpu.VMEM((2, f2 // 2, d), F32),
                            pltpu.VMEM((d, f2), BF16), pltpu.VMEM((f2 // 2, d), BF16),
                            pltpu.SemaphoreType.DMA((2, 2))]),
        out_shape=jax.ShapeDtypeStruct(xs.shape, F32),
        compiler_params=_cparams(("arbitrary",)),
        name="expert_ffn",
    )(block_expert, first, run_slot.astype(I32), next_expert, n_active, xs, w1, b1.reshape(ne, 1, f2), w2,
      b2.reshape(ne, 1, d))


def _combine_body(src_ref, dst_ref, ngrp_ref, tot_ref, h_ref, dloc_ref, gate_ref, nw_ref, ys_hbm, o_ref, stage,
                  sem):
    step = pl.program_id(0)
    tm = h_ref.shape[0]
    slot = step % 2

    def shares(at_step, at_slot, act):
        def per_expert(e, carry):
            idx = at_step * N_EXPERTS + e
            src, dst = src_ref[idx], dst_ref[idx]

            def piece(first, size):
                act(pltpu.make_async_copy(ys_hbm.at[_group_rows(dst + first, size)],
                                          stage.at[at_slot, _group_rows(src + first, size)], sem.at[at_slot]))

            _for_each_piece(ngrp_ref[idx], piece)
            return carry

        lax.fori_loop(0, N_EXPERTS, per_expert, 0)

    @pl.when(step == 0)
    def _():
        stage[...] = jnp.zeros(stage.shape, F32)
        shares(0, 0, lambda cp: cp.start())

    @pl.when(step + 1 < pl.num_programs(0))
    def _():
        shares(step + 1, 1 - slot, lambda cp: cp.start())

    _wait_groups(tot_ref[step], stage.at[slot], ys_hbm, sem.at[slot])
    dloc = dloc_ref[...]
    gates = gate_ref[...]
    moe = jnp.zeros(o_ref.shape, F32)
    for c in range(SLAB_ROWS // SLAB_CHUNK):
        cols = lax.broadcasted_iota(I32, (tm, SLAB_CHUNK), 1) + c * SLAB_CHUNK
        weight = jnp.zeros((tm, SLAB_CHUNK), F32)
        for k in range(TOP_K):
            weight = weight + jnp.where(dloc[:, k:k + 1] == cols, gates[:, k:k + 1], 0.0)
        rows = stage[slot, c * SLAB_CHUNK:(c + 1) * SLAB_CHUNK, :].astype(BF16)
        moe = moe + jnp.dot(weight.astype(BF16), rows, preferred_element_type=F32)
    y = h_ref[...] + moe
    o_ref[...] = y * lax.rsqrt(jnp.mean(y * y, axis=-1, keepdims=True) + EPS) * nw_ref[...]


def _combine(shares, h, dloc_tk, gates_tk, norm_w, ys):
    n_tok, d = h.shape
    tm = ROUTE_TILE
    return pl.pallas_call(
        _combine_body,
        grid_spec=pltpu.PrefetchScalarGridSpec(
            num_scalar_prefetch=4,
            grid=(n_tok // tm,),
            in_specs=[pl.BlockSpec((tm, d), lambda i, *_: (i, 0)),
                      pl.BlockSpec((tm, TOP_K), lambda i, *_: (i, 0)),
                      pl.BlockSpec((tm, TOP_K), lambda i, *_: (i, 0)),
                      pl.BlockSpec((1, d), lambda i, *_: (0, 0)),
                      pl.BlockSpec(memory_space=pl.ANY)],
            out_specs=pl.BlockSpec((tm, d), lambda i, *_: (i, 0)),
            scratch_shapes=[pltpu.VMEM((2, SLAB_ROWS, d), F32), pltpu.SemaphoreType.DMA((2,))]),
        out_shape=jax.ShapeDtypeStruct((n_tok, d), F32),
        compiler_params=_cparams(("arbitrary",)),
        name="combine_norm",
    )(*shares, h, dloc_tk, gates_tk, norm_w.reshape(1, d), ys)


def kernel(x, norm1_w, w_in, lb_logits, hgrn_norm_w, dw_w, dw_b, conv_ln_w, conv_ln_b, w_out, norm2_w,
           router_w, router_b, w1, b1, w2, b2, final_norm_w):
    batch, t_len, d = x.shape
    assert w_in.shape[0] == 1, "single-layer block"
    n_tok = batch * t_len
    hk = HGRN_HEADS * HEAD_DIM
    conv_ch = dw_w.shape[2]
    lb_table = jnp.cumsum(jax.nn.softmax(lb_logits.astype(F32), axis=1), axis=1)
    x2 = x.reshape(n_tok, d)
    proj = _in_proj(x2, norm1_w[0], w_in[0].astype(BF16))
    a = _hgrn(proj, lb_table[0, 0], lb_table[1, 0], hgrn_norm_w[0], batch, t_len)
    b = _conv(proj, dw_w[0], dw_b[0], conv_ln_w[0], conv_ln_b[0], batch, t_len, (5 * hk) // conv_ch)
    h_mid, n2_bf, dloc, gate, cnt_tiles = _mix(a, b, x2, w_out[0].astype(BF16), norm2_w[0],
                                                     router_w[0], router_b[0])
    nt = n_tok // ROUTE_TILE
    e_ids = jnp.arange(N_EXPERTS, dtype=I32)
    t_ids = jnp.arange(nt, dtype=I32)
    share = (cnt_tiles[:, :, 0].astype(I32) + SUBLANES - 1) // SUBLANES * SUBLANES
    in_slab = jnp.sum(jnp.where(e_ids[None, None, :] < e_ids[None, :, None], share[:, None, :], 0), axis=2)
    in_expert = jnp.sum(jnp.where((t_ids[None, :] < t_ids[:, None])[:, :, None], share[None, :, :], 0), axis=1)
    padded = (jnp.sum(share, axis=0) + MOE_BLOCK - 1) // MOE_BLOCK * MOE_BLOCK
    pad_end = jnp.sum(jnp.where(e_ids[None, :] <= e_ids[:, None], padded[None, :], 0), axis=1)
    in_xs = (pad_end - padded)[None, :] + in_expert
    n_blocks = -(-(n_tok * TOP_K + nt * N_EXPERTS * (SUBLANES - 1)) // MOE_BLOCK) + N_EXPERTS
    block_start = jnp.arange(n_blocks, dtype=I32) * MOE_BLOCK
    block_expert = jnp.minimum(jnp.sum((pad_end[None, :] <= block_start[:, None]).astype(I32), axis=1),
                               N_EXPERTS - 1)
    n_active = pad_end[-1:] // MOE_BLOCK
    shares = [(v // SUBLANES).reshape(-1) for v in (in_slab, in_xs, share)] + [jnp.sum(share, axis=1) // SUBLANES]
    xs = _dispatch(shares, n2_bf, dloc, n_blocks * MOE_BLOCK)
    ys = _ffn(block_expert, n_active, xs, w1[0], b1[0], w2[0], b2[0])
    out = _combine(shares, h_mid, dloc.T, gate.T, final_norm_w, ys)
    return out.reshape(batch, t_len, d)
```

```python
import functools
import math

import numpy as np
import jax
import jax.numpy as jnp
from jax import lax
from jax.experimental import pallas as pl
from jax.experimental.pallas import tpu as pltpu

F32 = jnp.float32
BF16 = jnp.bfloat16
I32 = jnp.int32

EPS = 1e-5
HGRN_HEADS = 4
HEAD_DIM = 128
HGRN_CHUNK = 256
CONV_LEN = 31
N_EXPERTS = 32
TOP_K = 4
SWIGLU_LIMIT = 7.0
SWIGLU_ALPHA = 1.702
MOE_BLOCK = 256
ROUTE_TILE = 512
LANES = 128
SUBLANES = 8
VMEM_LIMIT = 56 << 20

_NT = (((1,), (1,)), ((), ()))


def _sigmoid(x):
    return 0.5 * jnp.tanh(0.5 * x) + 0.5


def _silu(x):
    return x * _sigmoid(x)


def _cparams(sem):
    return pltpu.CompilerParams(dimension_semantics=sem, vmem_limit_bytes=VMEM_LIMIT)


def _inproj_body(x_ref, nw_ref, w_ref, o_ref):
    x = x_ref[...]
    n = x * lax.rsqrt(jnp.mean(x * x, axis=-1, keepdims=True) + EPS) * nw_ref[...]
    o_ref[...] = jnp.dot(n.astype(BF16), w_ref[...], preferred_element_type=F32)


def _in_proj(x2, norm_w, w_bf, tm=512):
    n_tok, d = x2.shape
    cols = w_bf.shape[1]
    return pl.pallas_call(
        _inproj_body,
        grid=(n_tok // tm,),
        in_specs=[pl.BlockSpec((tm, d), lambda i: (i, 0)),
                  pl.BlockSpec((1, d), lambda i: (0, 0)),
                  pl.BlockSpec((d, cols), lambda i: (0, 0))],
        out_specs=pl.BlockSpec((tm, cols), lambda i: (i, 0)),
        out_shape=jax.ShapeDtypeStruct((n_tok, cols), F32),
        compiler_params=_cparams(("parallel",)),
        name="in_proj",
    )(x2, norm_w.reshape(1, d), w_bf)


def _hgrn_levels(c):
    return [c >> (i + 1) for i in range(int(math.log2(c)))]


def _hgrn_sum_matrices(c, fwd):
    r = np.arange(c)
    m = _HGRN_SMALL_LEVEL
    small = np.zeros((c, c), np.float32)
    for t in range(c):
        p0 = t & ~(2 * m - 1)
        upper = (t & m) != 0
        if fwd:
            if upper:
                small[t, p0 + m:t + 1] = 1.0
            else:
                small[t, t + 1:p0 + m] = 1.0
        else:
            if upper:
                small[t, p0 + m:t] = 1.0
            else:
                small[t, t:p0 + m] = 1.0
    mats = [r[None, :] <= r[:, None] if fwd else r[None, :] >= r[:, None], small]
    return np.concatenate([np.asarray(x, np.float32) for x in mats], axis=0)


_HGRN_SMALL_LEVEL = 2
_HGRN_ROW_SLICE = 16


def _hgrn_chunk(q, k, lf, v_bf, vt_bf, st, sums_ref, fwd, c):
    dk = q.shape[1]
    hi = lf.astype(BF16)
    lo = (lf - hi.astype(F32)).astype(BF16)
    both = jnp.dot(sums_ref[...], jnp.concatenate([hi, lo], axis=1), preferred_element_type=F32)
    sums = both[:, :dk] + both[:, dk:]
    b = sums[0:c]
    edge_row = c - 1 if fwd else 0
    d_out = b[edge_row:edge_row + 1] - b
    row = lax.broadcasted_iota(I32, (c, dk), 0)
    rr = lax.broadcasted_iota(I32, (c, c), 0)
    cc = lax.broadcasted_iota(I32, (c, c), 1)
    sep = rr ^ cc
    a = None
    for m in _hgrn_levels(c):
        upper = (row & m) != 0
        is_q = upper if fwd else jnp.logical_not(upper)
        if m == 1:
            d = jnp.where(is_q, lf, 0.0)
        elif m == _HGRN_SMALL_LEVEL:
            d = sums[c:2 * c]
        else:
            blocks = c // (2 * m)
            at = m - 1 if fwd else m
            edge = b.reshape(blocks, 2 * m, dk)[:, at:at + 1, :]
            edge = jnp.broadcast_to(edge, (blocks, 2 * m, dk)).reshape(c, dk)
            d = jnp.where(is_q, b - edge, edge - b)
        r = (jnp.where(is_q, q, k) * jnp.exp2(d)).astype(BF16)
        if m < _HGRN_ROW_SLICE:
            a_l = lax.dot_general(r, r, _NT, preferred_element_type=F32)
            a = a_l if a is None else jnp.where(sep < 2 * m, a_l, a)
        else:
            halves = [(p0 + m, p0 + 2 * m) if fwd else (p0, p0 + m) for p0 in range(0, c, 2 * m)]
            a_q = lax.dot_general(jnp.concatenate([r[lo:hi] for lo, hi in halves], axis=0), r, _NT,
                                  preferred_element_type=F32)
            parts = []
            for n, (lo, hi) in enumerate(halves):
                new = a_q[n * m:(n + 1) * m]
                if a is None:
                    kept = jnp.zeros((m, c), F32)
                else:
                    new = jnp.where(sep[lo:hi] < 2 * m, new, a[lo:hi])
                    kept = a[lo - m:lo] if fwd else a[hi:hi + m]
                parts += [kept, new] if fwd else [new, kept]
            a = jnp.concatenate(parts, axis=0)
    a_diag = lax.dot_general(q.astype(BF16), k.astype(BF16), _NT, preferred_element_type=F32)
    a = jnp.where(sep < 1, a_diag, a)
    a = jnp.where(rr >= cc if fwd else rr <= cc, a, 0.0)
    e_b = jnp.exp2(b)
    q_in = (q * e_b).astype(BF16)
    k_out = (k * jnp.exp2(d_out)).astype(BF16)
    o = jnp.dot(a.astype(BF16), v_bf, preferred_element_type=F32)
    o = o + lax.dot_general(q_in, st.astype(BF16), _NT, preferred_element_type=F32)
    st_new = st * e_b[edge_row:edge_row + 1] + jnp.dot(vt_bf, k_out, preferred_element_type=F32)
    return o, st_new


def _hgrn_body(qz_ref, ff_ref, fb_ref, iv_ref, gz_ref, lbf_ref, lbb_ref, nw_ref, sf_ref, sb_ref, o_ref,
               q_s, kf_s, lff_s, kb_s, lfb_s, v_s, vt_s, acc_s, *, c):
    t_len, dk = qz_ref.shape
    nch = t_len // c
    q_s[...] = _silu(qz_ref[...])

    def gate(fz_ref, lb_ref, k_s, lf_s):
        lb = lb_ref[0]
        f = lb + (1.0 - lb) * _sigmoid(fz_ref[...])
        lf_s[...] = jnp.log2(f)
        k_s[...] = 1.0 - f

    gate(ff_ref, lbf_ref, kf_s, lff_s)
    gate(fb_ref, lbb_ref, kb_s, lfb_s)
    v = iv_ref[...]
    v_s[...] = v.astype(BF16)
    for ci in range(nch):
        vt_s[ci] = v[ci * c:(ci + 1) * c, :].T.astype(BF16)

    def run(k_s, lf_s, sums_ref, fwd):
        def step(i, st):
            ci = i if fwd else nch - 1 - i
            rows = pl.ds(pl.multiple_of(ci * c, c), c)
            o, st = _hgrn_chunk(q_s[rows, :], k_s[rows, :], lf_s[rows, :], v_s[rows, :], vt_s[ci],
                                st, sums_ref, fwd, c)
            if fwd:
                acc_s[rows, :] = o
            else:
                acc_s[rows, :] += o
            return st
        lax.fori_loop(0, nch, step, jnp.zeros((dk, dk), F32), unroll=True)

    run(kf_s, lff_s, sf_ref, True)
    run(kb_s, lfb_s, sb_ref, False)
    o = acc_s[...]
    o = o * lax.rsqrt(jnp.mean(o * o, axis=-1, keepdims=True) + EPS) * nw_ref[0]
    o_ref[...] = (o * _silu(gz_ref[...])).astype(o_ref.dtype)


def _hgrn(proj, lb_f, lb_b, norm_w, batch, t_len):
    h, dk, c = HGRN_HEADS, HEAD_DIM, HGRN_CHUNK
    nrows = 2
    sums_f = jnp.asarray(_hgrn_sum_matrices(c, True), BF16)
    sums_b = jnp.asarray(_hgrn_sum_matrices(c, False), BF16)

    def col(group):
        return pl.BlockSpec((t_len, dk), lambda b, hh: (b, group * h + hh))

    def per_head():
        return pl.BlockSpec((1, 1, dk), lambda b, hh: (hh, 0, 0))

    const = pl.BlockSpec((nrows * c, c), lambda b, hh: (0, 0))
    seq = lambda dt: pltpu.VMEM((t_len, dk), dt)
    return pl.pallas_call(
        functools.partial(_hgrn_body, c=c),
        grid=(batch, h),
        in_specs=[col(0), col(1), col(2), col(3), col(4), per_head(), per_head(), per_head(), const, const],
        out_specs=pl.BlockSpec((t_len, dk), lambda b, hh: (b, hh)),
        out_shape=jax.ShapeDtypeStruct((batch * t_len, h * dk), BF16),
        scratch_shapes=[seq(F32), seq(F32), seq(F32), seq(F32), seq(F32), seq(BF16),
                        pltpu.VMEM((t_len // c, dk, c), BF16), seq(F32)],
        compiler_params=_cparams(("parallel", "parallel")),
        name="hgrn",
    )(proj, proj, proj, proj, proj, lb_f.reshape(h, 1, dk), lb_b.reshape(h, 1, dk),
      norm_w.reshape(h, 1, dk), sums_f, sums_b)


CONV_ROWS = 64
CONV_HALO = 16


def _conv_body(cv_ref, cg_ref, w_ref, b_ref, lnw_ref, lnb_ref, o_ref, u_s, y_s):
    t_len, ch = cv_ref.shape
    halo, rows = CONV_HALO, CONV_ROWS
    shift0 = halo - (CONV_LEN - 1) // 2
    win = rows + 2 * halo
    for g in range(ch // LANES):
        lanes = slice(g * LANES, (g + 1) * LANES)
        u_s[g, 0:halo, :] = jnp.zeros((halo, LANES), F32)
        u_s[g, halo + t_len:, :] = jnp.zeros((halo, LANES), F32)
        u_s[g, halo:halo + t_len, :] = cv_ref[:, lanes] * _sigmoid(cg_ref[:, lanes])

    def conv_step(i, carry):
        t0 = pl.multiple_of(i * rows, rows)
        for g in range(ch // LANES):
            lanes = slice(g * LANES, (g + 1) * LANES)
            window = u_s.at[g, pl.ds(t0, win), :]
            acc = jnp.zeros((rows, LANES), F32)
            for j in range(CONV_LEN):
                off = j + shift0
                acc = acc + w_ref[j:j + 1, lanes] * window[off:off + rows, :]
            y_s[pl.ds(t0, rows), lanes] = acc + b_ref[:, lanes]
        return carry

    lax.fori_loop(0, t_len // rows, conv_step, 0)

    def step(i, carry):
        t0 = pl.multiple_of(i * rows, rows)
        y = y_s[pl.ds(t0, rows), :]
        mu = jnp.mean(y, axis=-1, keepdims=True)
        yc = y - mu
        var = jnp.mean(yc * yc, axis=-1, keepdims=True)
        z = yc * lax.rsqrt(var + EPS) * lnw_ref[...] + lnb_ref[...]
        o_ref[pl.ds(t0, rows), :] = _silu(z).astype(o_ref.dtype)
        return carry

    lax.fori_loop(0, t_len // rows, step, 0, unroll=4)


def _conv(proj, dw_w, dw_b, ln_w, ln_b, batch, t_len, first_col_block):
    ch = dw_w.shape[1]
    vec = pl.BlockSpec((1, ch), lambda b: (0, 0))
    return pl.pallas_call(
        _conv_body,
        grid=(batch,),
        in_specs=[pl.BlockSpec((t_len, ch), lambda b: (b, first_col_block)),
                  pl.BlockSpec((t_len, ch), lambda b: (b, first_col_block + 1)),
                  pl.BlockSpec((CONV_LEN, ch), lambda b: (0, 0)), vec, vec, vec],
        out_specs=pl.BlockSpec((t_len, ch), lambda b: (b, 0)),
        out_shape=jax.ShapeDtypeStruct((batch * t_len, ch), BF16),
        scratch_shapes=[pltpu.VMEM((ch // LANES, t_len + 2 * CONV_HALO, LANES), F32),
                        pltpu.VMEM((t_len, ch), F32)],
        compiler_params=_cparams(("parallel",)),
        name="conv",
    )(proj, proj, dw_w, dw_b.reshape(1, ch), ln_w.reshape(1, ch), ln_b.reshape(1, ch))


def _mix_body(a_ref, b_ref, x_ref, wa_ref, wb_ref, nw_ref, wr_ref, rb_ref, tri_ref, elow_ref,
              h_ref, n2_ref, dloc_ref, gate_ref, cnt_ref):
    tm = x_ref.shape[0]
    ne = wr_ref.shape[0] // 2

    h = x_ref[...] + jnp.dot(a_ref[...], wa_ref[...], preferred_element_type=F32) \
        + jnp.dot(b_ref[...], wb_ref[...], preferred_element_type=F32)
    h_ref[...] = h
    n2 = h * lax.rsqrt(jnp.mean(h * h, axis=-1, keepdims=True) + EPS) * nw_ref[...]
    hi = n2.astype(BF16)
    n2_ref[...] = hi
    lo = (n2 - hi.astype(F32)).astype(BF16)
    by_hi = lax.dot_general(wr_ref[...], hi, _NT, preferred_element_type=F32)
    by_lo = lax.dot_general(wr_ref[0:ne, :], lo, _NT, preferred_element_type=F32)
    logits = by_hi[0:ne] + by_hi[ne:] + by_lo + rb_ref[...]
    e_iota = lax.broadcasted_iota(I32, (ne, tm), 0)
    work = logits
    sels, vals = [], []
    for _ in range(TOP_K):
        mx = jnp.max(work, axis=0, keepdims=True)
        idx = jnp.min(jnp.where(work == mx, e_iota, ne), axis=0, keepdims=True)
        sel = e_iota == idx
        work = jnp.where(sel, -jnp.inf, work)
        sels.append(sel)
        vals.append(mx)
    exps = [jnp.exp(v - vals[0]) for v in vals]
    denom = exps[0] + exps[1] + exps[2] + exps[3]
    chosen = jnp.zeros((ne, tm), F32)
    for sel in sels:
        chosen = jnp.where(sel, 1.0, chosen)
    chosen_bf = chosen.astype(BF16)
    prior = jnp.dot(chosen_bf, tri_ref[...], preferred_element_type=F32)
    cnt = jnp.dot(chosen_bf, jnp.ones((tm, LANES), BF16), preferred_element_type=F32)
    cnt_ref[0] = cnt
    share = jnp.floor((cnt + (SUBLANES - 1)) * (1.0 / SUBLANES)) * SUBLANES
    lower = jnp.dot(elow_ref[...], share.astype(BF16), preferred_element_type=F32)
    slab_row = prior + jnp.concatenate([lower] * (tm // LANES), axis=1)
    for k in range(TOP_K):
        dloc_ref[k:k + 1, :] = jnp.sum(jnp.where(sels[k], slab_row, 0.0), axis=0, keepdims=True).astype(I32)
        gate_ref[k:k + 1, :] = exps[k] / denom


def _mix(a, b, x2, w_out_bf, norm_w, router_w, router_b, tm=ROUTE_TILE):
    n_tok, d = x2.shape
    wa, wb = w_out_bf[:a.shape[1]], w_out_bf[a.shape[1]:]
    ne = router_w.shape[1]
    wr_t = router_w.T
    wr_hi = wr_t.astype(BF16)
    wr_lo = (wr_t - wr_hi.astype(F32)).astype(BF16)
    tri = jnp.asarray(np.triu(np.ones((tm, tm), np.float32), k=1), BF16)
    e_lower = jnp.asarray(np.tril(np.ones((ne, ne), np.float32), k=-1), BF16)
    const = lambda shape: pl.BlockSpec(shape, lambda i: tuple(0 for _ in shape))
    return pl.pallas_call(
        _mix_body,
        grid=(n_tok // tm,),
        in_specs=[pl.BlockSpec((tm, a.shape[1]), lambda i: (i, 0)),
                  pl.BlockSpec((tm, b.shape[1]), lambda i: (i, 0)),
                  pl.BlockSpec((tm, d), lambda i: (i, 0)),
                  const(wa.shape), const(wb.shape), const((1, d)),
                  const((2 * ne, d)), const((ne, 1)), const((tm, tm)), const((ne, ne))],
        out_specs=[pl.BlockSpec((tm, d), lambda i: (i, 0)),
                   pl.BlockSpec((tm, d), lambda i: (i, 0)),
                   pl.BlockSpec((TOP_K, tm), lambda i: (0, i)),
                   pl.BlockSpec((TOP_K, tm), lambda i: (0, i)),
                   pl.BlockSpec((1, ne, LANES), lambda i: (i, 0, 0))],
        out_shape=[jax.ShapeDtypeStruct((n_tok, d), F32),
                   jax.ShapeDtypeStruct((n_tok, d), BF16),
                   jax.ShapeDtypeStruct((TOP_K, n_tok), I32),
                   jax.ShapeDtypeStruct((TOP_K, n_tok), F32),
                   jax.ShapeDtypeStruct((n_tok // tm, ne, LANES), F32)],
        compiler_params=_cparams(("parallel",)),
        name="mix_router",
    )(a, b, x2, wa, wb, norm_w.reshape(1, d), jnp.concatenate([wr_hi, wr_lo], axis=0), router_b.reshape(ne, 1),
      tri, e_lower)


SLAB_ROWS = TOP_K * ROUTE_TILE + N_EXPERTS * SUBLANES
SLAB_CHUNK = 256
SLAB_PIECES = (64, 32, 16, 8, 4, 2, 1)
SLAB_COMMON = ROUTE_TILE * TOP_K // N_EXPERTS // SUBLANES


def _for_each_piece(ngroups, fn):
    def pieces(sizes):
        for size in sizes:
            first = lax.bitwise_and(ngroups, ~(2 * size - 1))

            @pl.when(lax.bitwise_and(ngroups, size) != 0)
            def _():
                fn(first, size)

    rare = tuple(s for s in SLAB_PIECES if s >= 2 * SLAB_COMMON)

    @pl.when(ngroups >= min(rare))
    def _():
        pieces(rare)

    pieces(tuple(s for s in SLAB_PIECES if s < 2 * SLAB_COMMON))


def _group_rows(first_group, groups):
    return pl.ds(pl.multiple_of(first_group * SUBLANES, SUBLANES), groups * SUBLANES)


def _wait_groups(ngroups, stage_slot, hbm, sem):
    for size in (256, 128) + SLAB_PIECES:
        @pl.when(lax.bitwise_and(ngroups, size) != 0)
        def _():
            pltpu.make_async_copy(hbm.at[_group_rows(0, size)], stage_slot.at[_group_rows(0, size)], sem).wait()


def _dispatch_body(src_ref, dst_ref, ngrp_ref, tot_ref, n2_ref, dloc_ref, xs_hbm, stage, sem):
    step = pl.program_id(0)
    tm = n2_ref.shape[0]
    slot = step % 2

    def shares(at_step, at_slot, act):
        def per_expert(e, carry):
            idx = at_step * N_EXPERTS + e
            src, dst = src_ref[idx], dst_ref[idx]

            def piece(first, size):
                act(pltpu.make_async_copy(stage.at[at_slot, _group_rows(src + first, size)],
                                          xs_hbm.at[_group_rows(dst + first, size)], sem.at[at_slot]))

            _for_each_piece(ngrp_ref[idx], piece)
            return carry

        lax.fori_loop(0, N_EXPERTS, per_expert, 0)

    def wait_slab(at_step, at_slot):
        _wait_groups(tot_ref[at_step], stage.at[at_slot], xs_hbm, sem.at[at_slot])

    @pl.when(step >= 2)
    def _():
        wait_slab(step - 2, slot)

    n2 = n2_ref[...]
    dloc = dloc_ref[...]
    for c in range(SLAB_ROWS // SLAB_CHUNK):
        rows = lax.broadcasted_iota(I32, (SLAB_CHUNK, tm), 0) + c * SLAB_CHUNK
        pick = jnp.zeros((SLAB_CHUNK, tm), F32)
        for k in range(TOP_K):
            pick = pick + jnp.where(dloc[k:k + 1, :] == rows, 1.0, 0.0)
        stage[slot, c * SLAB_CHUNK:(c + 1) * SLAB_CHUNK, :] = jnp.dot(pick.astype(BF16), n2,
                                                                       preferred_element_type=F32)
    shares(step, slot, lambda cp: cp.start())

    @pl.when(step == pl.num_programs(0) - 1)
    def _():
        wait_slab(step, slot)

        @pl.when(step >= 1)
        def _():
            wait_slab(step - 1, 1 - slot)


def _dispatch(shares, n2_bf, dloc, n_rows):
    n_tok, d = n2_bf.shape
    tm = ROUTE_TILE
    return pl.pallas_call(
        _dispatch_body,
        grid_spec=pltpu.PrefetchScalarGridSpec(
            num_scalar_prefetch=4,
            grid=(n_tok // tm,),
            in_specs=[pl.BlockSpec((tm, d), lambda i, *_: (i, 0)),
                      pl.BlockSpec((TOP_K, tm), lambda i, *_: (0, i))],
            out_specs=pl.BlockSpec(memory_space=pl.ANY),
            scratch_shapes=[pltpu.VMEM((2, SLAB_ROWS, d), F32), pltpu.SemaphoreType.DMA((2,))]),
        out_shape=jax.ShapeDtypeStruct((n_rows, d), F32),
        compiler_params=_cparams(("arbitrary",)),
        name="dispatch",
    )(*shares, n2_bf, dloc)


FFN_CAST_ROWS = 64


def _ffn_body(be_ref, first_ref, slot_ref, next_ref, nact_ref, xs_ref, w1_hbm, b1_ref, w2_hbm, b2_ref, ys_ref,
              w1_f, w2_f, w1_b, w2_b, sem):
    i = pl.program_id(0)

    def fetch(expert, slot):
        return (pltpu.make_async_copy(w1_hbm.at[expert], w1_f.at[slot], sem.at[0, slot]),
                pltpu.make_async_copy(w2_hbm.at[expert], w2_f.at[slot], sem.at[1, slot]))

    @pl.when(i < nact_ref[0])
    def _():
        slot = slot_ref[i]

        @pl.when(first_ref[i] == 1)
        def _():
            @pl.when(i == 0)
            def _():
                for cp in fetch(be_ref[0], slot):
                    cp.start()

            for cp in fetch(be_ref[i], slot):
                cp.wait()

            @pl.when(next_ref[i] >= 0)
            def _():
                for cp in fetch(next_ref[i], 1 - slot):
                    cp.start()

            def cast(src, dst):
                def body(c, carry):
                    rows = pl.ds(pl.multiple_of(c * FFN_CAST_ROWS, FFN_CAST_ROWS), FFN_CAST_ROWS)
                    dst[rows, :] = src[slot, rows, :].astype(BF16)
                    return carry
                lax.fori_loop(0, dst.shape[0] // FFN_CAST_ROWS, body, 0)

            cast(w1_f, w1_b)
            cast(w2_f, w2_b)

        x = xs_ref[...].astype(BF16)
        hdn = jnp.dot(x, w1_b[...], preferred_element_type=F32) + b1_ref[0]
        d_ff = hdn.shape[1] // 2
        glu = jnp.minimum(hdn[:, :d_ff], SWIGLU_LIMIT)
        lin = jnp.clip(hdn[:, d_ff:], -SWIGLU_LIMIT, SWIGLU_LIMIT)
        act = glu * _sigmoid(SWIGLU_ALPHA * glu) * (lin + 1.0)
        y = jnp.dot(act.astype(BF16), w2_b[...], preferred_element_type=F32) + b2_ref[0]
        ys_ref[...] = y


def _ffn(block_expert, n_active, xs, w1, b1, w2, b2):
    ne, d, f2 = w1.shape
    nb = xs.shape[0] // MOE_BLOCK
    block = (MOE_BLOCK, d)
    idx = jnp.arange(nb, dtype=I32)
    active = idx < n_active[0]
    prev = jnp.concatenate([block_expert[:1] - 1, block_expert[:-1]])
    first = (active & (block_expert != prev)).astype(I32)
    run_slot = (jnp.sum(jnp.where(idx[None, :] <= idx[:, None], first[None, :], 0), axis=1) - 1) & 1
    later_first = (first[None, :] == 1) & (idx[None, :] > idx[:, None])
    next_block = jnp.min(jnp.where(later_first, idx[None, :], nb), axis=1)
    next_expert = jnp.where(next_block < nb, block_expert[jnp.minimum(next_block, nb - 1)], -1).astype(I32)

    def blk(i, be, fi, sl, nx, nact):
        return (jnp.minimum(i, nact[0] - 1), 0)

    def exp(i, be, fi, sl, nx, nact):
        return (be[jnp.minimum(i, nact[0] - 1)], 0, 0)

    return pl.pallas_call(
        _ffn_body,
        grid_spec=pltpu.PrefetchScalarGridSpec(
            num_scalar_prefetch=5,
            grid=(nb,),
            in_specs=[pl.BlockSpec(block, blk),
                      pl.BlockSpec(memory_space=pl.ANY),
                      pl.BlockSpec((1, 1, f2), exp),
                      pl.BlockSpec(memory_space=pl.ANY),
                      pl.BlockSpec((1, 1, d), exp)],
            out_specs=pl.BlockSpec(block, blk),
            scratch_shapes=[pltpu.VMEM((2, d, f2), F32), pltpu.VMEM((2, f2 // 2, d), F32),
                            pltpu.VMEM((d, f2), BF16), pltpu.VMEM((f2 // 2, d), BF16),
                            pltpu.SemaphoreType.DMA((2, 2))]),
        out_shape=jax.ShapeDtypeStruct(xs.shape, F32),
        compiler_params=_cparams(("arbitrary",)),
        name="expert_ffn",
    )(block_expert, first, run_slot.astype(I32), next_expert, n_active, xs, w1, b1.reshape(ne, 1, f2), w2,
      b2.reshape(ne, 1, d))


def _combine_body(src_ref, dst_ref, ngrp_ref, tot_ref, h_ref, dloc_ref, gate_ref, nw_ref, ys_hbm, o_ref, stage,
                  sem):
    step = pl.program_id(0)
    tm = h_ref.shape[0]
    slot = step % 2

    def shares(at_step, at_slot, act):
        def per_expert(e, carry):
            idx = at_step * N_EXPERTS + e
            src, dst = src_ref[idx], dst_ref[idx]

            def piece(first, size):
                act(pltpu.make_async_copy(ys_hbm.at[_group_rows(dst + first, size)],
                                          stage.at[at_slot, _group_rows(src + first, size)], sem.at[at_slot]))

            _for_each_piece(ngrp_ref[idx], piece)
            return carry

        lax.fori_loop(0, N_EXPERTS, per_expert, 0)

    @pl.when(step == 0)
    def _():
        stage[...] = jnp.zeros(stage.shape, F32)
        shares(0, 0, lambda cp: cp.start())

    @pl.when(step + 1 < pl.num_programs(0))
    def _():
        shares(step + 1, 1 - slot, lambda cp: cp.start())

    _wait_groups(tot_ref[step], stage.at[slot], ys_hbm, sem.at[slot])
    dloc = dloc_ref[...]
    gates = gate_ref[...]
    moe = jnp.zeros(o_ref.shape, F32)
    for c in range(SLAB_ROWS // SLAB_CHUNK):
        cols = lax.broadcasted_iota(I32, (tm, SLAB_CHUNK), 1) + c * SLAB_CHUNK
        weight = jnp.zeros((tm, SLAB_CHUNK), F32)
        for k in range(TOP_K):
            weight = weight + jnp.where(dloc[:, k:k + 1] == cols, gates[:, k:k + 1], 0.0)
        rows = stage[slot, c * SLAB_CHUNK:(c + 1) * SLAB_CHUNK, :].astype(BF16)
        moe = moe + jnp.dot(weight.astype(BF16), rows, preferred_element_type=F32)
    y = h_ref[...] + moe
    o_ref[...] = y * lax.rsqrt(jnp.mean(y * y, axis=-1, keepdims=True) + EPS) * nw_ref[...]


def _combine(shares, h, dloc_tk, gates_tk, norm_w, ys):
    n_tok, d = h.shape
    tm = ROUTE_TILE
    return pl.pallas_call(
        _combine_body,
        grid_spec=pltpu.PrefetchScalarGridSpec(
            num_scalar_prefetch=4,
            grid=(n_tok // tm,),
            in_specs=[pl.BlockSpec((tm, d), lambda i, *_: (i, 0)),
                      pl.BlockSpec((tm, TOP_K), lambda i, *_: (i, 0)),
                      pl.BlockSpec((tm, TOP_K), lambda i, *_: (i, 0)),
                      pl.BlockSpec((1, d), lambda i, *_: (0, 0)),
                      pl.BlockSpec(memory_space=pl.ANY)],
            out_specs=pl.BlockSpec((tm, d), lambda i, *_: (i, 0)),
            scratch_shapes=[pltpu.VMEM((2, SLAB_ROWS, d), F32), pltpu.SemaphoreType.DMA((2,))]),
        out_shape=jax.ShapeDtypeStruct((n_tok, d), F32),
        compiler_params=_cparams(("arbitrary",)),
        name="combine_norm",
    )(*shares, h, dloc_tk, gates_tk, norm_w.reshape(1, d), ys)


def kernel(x, norm1_w, w_in, lb_logits, hgrn_norm_w, dw_w, dw_b, conv_ln_w, conv_ln_b, w_out, norm2_w,
           router_w, router_b, w1, b1, w2, b2, final_norm_w):
    batch, t_len, d = x.shape
    assert w_in.shape[0] == 1, "single-layer block"
    n_tok = batch * t_len
    hk = HGRN_HEADS * HEAD_DIM
    conv_ch = dw_w.shape[2]
    lb_table = jnp.cumsum(jax.nn.softmax(lb_logits.astype(F32), axis=1), axis=1)
    x2 = x.reshape(n_tok, d)
    proj = _in_proj(x2, norm1_w[0], w_in[0].astype(BF16))
    a = _hgrn(proj, lb_table[0, 0], lb_table[1, 0], hgrn_norm_w[0], batch, t_len)
    b = _conv(proj, dw_w[0], dw_b[0], conv_ln_w[0], conv_ln_b[0], batch, t_len, (5 * hk) // conv_ch)
    h_mid, n2_bf, dloc, gate, cnt_tiles = _mix(a, b, x2, w_out[0].astype(BF16), norm2_w[0],
                                                     router_w[0], router_b[0])
    nt = n_tok // ROUTE_TILE
    e_ids = jnp.arange(N_EXPERTS, dtype=I32)
    t_ids = jnp.arange(nt, dtype=I32)
    share = (cnt_tiles[:, :, 0].astype(I32) + SUBLANES - 1) // SUBLANES * SUBLANES
    in_slab = jnp.sum(jnp.where(e_ids[None, None, :] < e_ids[None, :, None], share[:, None, :], 0), axis=2)
    in_expert = jnp.sum(jnp.where((t_ids[None, :] < t_ids[:, None])[:, :, None], share[None, :, :], 0), axis=1)
    padded = (jnp.sum(share, axis=0) + MOE_BLOCK - 1) // MOE_BLOCK * MOE_BLOCK
    pad_end = jnp.sum(jnp.where(e_ids[None, :] <= e_ids[:, None], padded[None, :], 0), axis=1)
    in_xs = (pad_end - padded)[None, :] + in_expert
    n_blocks = -(-(n_tok * TOP_K + nt * N_EXPERTS * (SUBLANES - 1)) // MOE_BLOCK) + N_EXPERTS
    block_start = jnp.arange(n_blocks, dtype=I32) * MOE_BLOCK
    block_expert = jnp.minimum(jnp.sum((pad_end[None, :] <= block_start[:, None]).astype(I32), axis=1),
                               N_EXPERTS - 1)
    n_active = pad_end[-1:] // MOE_BLOCK
    shares = [(v // SUBLANES).reshape(-1) for v in (in_slab, in_xs, share)] + [jnp.sum(share, axis=1) // SUBLANES]
    xs = _dispatch(shares, n2_bf, dloc, n_blocks * MOE_BLOCK)
    ys = _ffn(block_expert, n_active, xs, w1[0], b1[0], w2[0], b2[0])
    out = _combine(shares, h_mid, dloc.T, gate.T, final_norm_w, ys)
    return out.reshape(batch, t_len, d)
```

```python
import functools
import math

import numpy as np
import jax
import jax.numpy as jnp
from jax import lax
from jax.experimental import pallas as pl
from jax.experimental.pallas import tpu as pltpu

F32 = jnp.float32
BF16 = jnp.bfloat16
I32 = jnp.int32

EPS = 1e-5
HGRN_HEADS = 4
HEAD_DIM = 128
HGRN_CHUNK = 256
CONV_LEN = 31
N_EXPERTS = 32
TOP_K = 4
SWIGLU_LIMIT = 7.0
SWIGLU_ALPHA = 1.702
MOE_BLOCK = 256
ROUTE_TILE = 512
LANES = 128
SUBLANES = 8
VMEM_LIMIT = 56 << 20

_NT = (((1,), (1,)), ((), ()))


def _sigmoid(x):
    return 0.5 * jnp.tanh(0.5 * x) + 0.5


def _silu(x):
    return x * _sigmoid(x)


def _cparams(sem):
    return pltpu.CompilerParams(dimension_semantics=sem, vmem_limit_bytes=VMEM_LIMIT)


def _inproj_body(x_ref, nw_ref, w_ref, o_ref):
    x = x_ref[...]
    n = x * lax.rsqrt(jnp.mean(x * x, axis=-1, keepdims=True) + EPS) * nw_ref[...]
    o_ref[...] = jnp.dot(n.astype(BF16), w_ref[...], preferred_element_type=F32)


def _in_proj(x2, norm_w, w_bf, tm=512):
    n_tok, d = x2.shape
    cols = w_bf.shape[1]
    return pl.pallas_call(
        _inproj_body,
        grid=(n_tok // tm,),
        in_specs=[pl.BlockSpec((tm, d), lambda i: (i, 0)),
                  pl.BlockSpec((1, d), lambda i: (0, 0)),
                  pl.BlockSpec((d, cols), lambda i: (0, 0))],
        out_specs=pl.BlockSpec((tm, cols), lambda i: (i, 0)),
        out_shape=jax.ShapeDtypeStruct((n_tok, cols), F32),
        compiler_params=_cparams(("parallel",)),
        name="in_proj",
    )(x2, norm_w.reshape(1, d), w_bf)


def _hgrn_levels(c):
    return [c >> (i + 1) for i in range(int(math.log2(c)))]


def _hgrn_sum_matrices(c, fwd):
    r = np.arange(c)
    m = _HGRN_SMALL_LEVEL
    small = np.zeros((c, c), np.float32)
    for t in range(c):
        p0 = t & ~(2 * m - 1)
        upper = (t & m) != 0
        if fwd:
            if upper:
                small[t, p0 + m:t + 1] = 1.0
            else:
                small[t, t + 1:p0 + m] = 1.0
        else:
            if upper:
                small[t, p0 + m:t] = 1.0
            else:
                small[t, t:p0 + m] = 1.0
    mats = [r[None, :] <= r[:, None] if fwd else r[None, :] >= r[:, None], small]
    return np.concatenate([np.asarray(x, np.float32) for x in mats], axis=0)


_HGRN_SMALL_LEVEL = 2
_HGRN_ROW_SLICE = 16


def _hgrn_chunk(q, k, lf, v_bf, vt_bf, st, sums_ref, fwd, c):
    dk = q.shape[1]
    hi = lf.astype(BF16)
    lo = (lf - hi.astype(F32)).astype(BF16)
    both = jnp.dot(sums_ref[...], jnp.concatenate([hi, lo], axis=1), preferred_element_type=F32)
    sums = both[:, :dk] + both[:, dk:]
    b = sums[0:c]
    edge_row = c - 1 if fwd else 0
    d_out = b[edge_row:edge_row + 1] - b
    row = lax.broadcasted_iota(I32, (c, dk), 0)
    rr = lax.broadcasted_iota(I32, (c, c), 0)
    cc = lax.broadcasted_iota(I32, (c, c), 1)
    sep = rr ^ cc
    a = None
    for m in _hgrn_levels(c):
        upper = (row & m) != 0
        is_q = upper if fwd else jnp.logical_not(upper)
        if m == 1:
            d = jnp.where(is_q, lf, 0.0)
        elif m == _HGRN_SMALL_LEVEL:
            d = sums[c:2 * c]
        else:
            blocks = c // (2 * m)
            at = m - 1 if fwd else m
            edge = b.reshape(blocks, 2 * m, dk)[:, at:at + 1, :]
            edge = jnp.broadcast_to(edge, (blocks, 2 * m, dk)).reshape(c, dk)
            d = jnp.where(is_q, b - edge, edge - b)
        r = (jnp.where(is_q, q, k) * jnp.exp2(d)).astype(BF16)
        if m < _HGRN_ROW_SLICE:
            a_l = lax.dot_general(r, r, _NT, preferred_element_type=F32)
            a = a_l if a is None else jnp.where(sep < 2 * m, a_l, a)
        else:
            halves = [(p0 + m, p0 + 2 * m) if fwd else (p0, p0 + m) for p0 in range(0, c, 2 * m)]
            a_q = lax.dot_general(jnp.concatenate([r[lo:hi] for lo, hi in halves], axis=0), r, _NT,
                                  preferred_element_type=F32)
            parts = []
            for n, (lo, hi) in enumerate(halves):
                new = a_q[n * m:(n + 1) * m]
                if a is None:
                    kept = jnp.zeros((m, c), F32)
                else:
                    new = jnp.where(sep[lo:hi] < 2 * m, new, a[lo:hi])
                    kept = a[lo - m:lo] if fwd else a[hi:hi + m]
                parts += [kept, new] if fwd else [new, kept]
            a = jnp.concatenate(parts, axis=0)
    a_diag = lax.dot_general(q.astype(BF16), k.astype(BF16), _NT, preferred_element_type=F32)
    a = jnp.where(sep < 1, a_diag, a)
    a = jnp.where(rr >= cc if fwd else rr <= cc, a, 0.0)
    e_b = jnp.exp2(b)
    q_in = (q * e_b).astype(BF16)
    k_out = (k * jnp.exp2(d_out)).astype(BF16)
    o = jnp.dot(a.astype(BF16), v_bf, preferred_element_type=F32)
    o = o + lax.dot_general(q_in, st.astype(BF16), _NT, preferred_element_type=F32)
    st_new = st * e_b[edge_row:edge_row + 1] + jnp.dot(vt_bf, k_out, preferred_element_type=F32)
    return o, st_new


def _hgrn_body(qz_ref, ff_ref, fb_ref, iv_ref, gz_ref, lbf_ref, lbb_ref, nw_ref, sf_ref, sb_ref, o_ref,
               q_s, kf_s, lff_s, kb_s, lfb_s, v_s, vt_s, acc_s, *, c):
    t_len, dk = qz_ref.shape
    nch = t_len // c
    q_s[...] = _silu(qz_ref[...])

    def gate(fz_ref, lb_ref, k_s, lf_s):
        lb = lb_ref[0]
        f = lb + (1.0 - lb) * _sigmoid(fz_ref[...])
        lf_s[...] = jnp.log2(f)
        k_s[...] = 1.0 - f

    gate(ff_ref, lbf_ref, kf_s, lff_s)
    gate(fb_ref, lbb_ref, kb_s, lfb_s)
    v = iv_ref[...]
    v_s[...] = v.astype(BF16)
    for ci in range(nch):
        vt_s[ci] = v[ci * c:(ci + 1) * c, :].T.astype(BF16)

    def run(k_s, lf_s, sums_ref, fwd):
        def step(i, st):
            ci = i if fwd else nch - 1 - i
            rows = pl.ds(pl.multiple_of(ci * c, c), c)
            o, st = _hgrn_chunk(q_s[rows, :], k_s[rows, :], lf_s[rows, :], v_s[rows, :], vt_s[ci],
                                st, sums_ref, fwd, c)
            if fwd:
                acc_s[rows, :] = o
            else:
                acc_s[rows, :] += o
            return st
        lax.fori_loop(0, nch, step, jnp.zeros((dk, dk), F32), unroll=True)

    run(kf_s, lff_s, sf_ref, True)
    run(kb_s, lfb_s, sb_ref, False)
    o = acc_s[...]
    o = o * lax.rsqrt(jnp.mean(o * o, axis=-1, keepdims=True) + EPS) * nw_ref[0]
    o_ref[...] = (o * _silu(gz_ref[...])).astype(o_ref.dtype)


def _hgrn(proj, lb_f, lb_b, norm_w, batch, t_len):
    h, dk, c = HGRN_HEADS, HEAD_DIM, HGRN_CHUNK
    nrows = 2
    sums_f = jnp.asarray(_hgrn_sum_matrices(c, True), BF16)
    sums_b = jnp.asarray(_hgrn_sum_matrices(c, False), BF16)

    def col(group):
        return pl.BlockSpec((t_len, dk), lambda b, hh: (b, group * h + hh))

    def per_head():
        return pl.BlockSpec((1, 1, dk), lambda b, hh: (hh, 0, 0))

    const = pl.BlockSpec((nrows * c, c), lambda b, hh: (0, 0))
    seq = lambda dt: pltpu.VMEM((t_len, dk), dt)
    return pl.pallas_call(
        functools.partial(_hgrn_body, c=c),
        grid=(batch, h),
        in_specs=[col(0), col(1), col(2), col(3), col(4), per_head(), per_head(), per_head(), const, const],
        out_specs=pl.BlockSpec((t_len, dk), lambda b, hh: (b, hh)),
        out_shape=jax.ShapeDtypeStruct((batch * t_len, h * dk), BF16),
        scratch_shapes=[seq(F32), seq(F32), seq(F32), seq(F32), seq(F32), seq(BF16),
                        pltpu.VMEM((t_len // c, dk, c), BF16), seq(F32)],
        compiler_params=_cparams(("parallel", "parallel")),
        name="hgrn",
    )(proj, proj, proj, proj, proj, lb_f.reshape(h, 1, dk), lb_b.reshape(h, 1, dk),
      norm_w.reshape(h, 1, dk), sums_f, sums_b)


CONV_ROWS = 64
CONV_HALO = 16


def _conv_body(cv_ref, cg_ref, w_ref, b_ref, lnw_ref, lnb_ref, o_ref, u_s, y_s):
    t_len, ch = cv_ref.shape
    halo, rows = CONV_HALO, CONV_ROWS
    shift0 = halo - (CONV_LEN - 1) // 2
    win = rows + 2 * halo
    for g in range(ch // LANES):
        lanes = slice(g * LANES, (g + 1) * LANES)
        u_s[g, 0:halo, :] = jnp.zeros((halo, LANES), F32)
        u_s[g, halo + t_len:, :] = jnp.zeros((halo, LANES), F32)
        u_s[g, halo:halo + t_len, :] = cv_ref[:, lanes] * _sigmoid(cg_ref[:, lanes])

    def conv_step(i, carry):
        t0 = pl.multiple_of(i * rows, rows)
        for g in range(ch // LANES):
            lanes = slice(g * LANES, (g + 1) * LANES)
            window = u_s.at[g, pl.ds(t0, win), :]
            acc = jnp.zeros((rows, LANES), F32)
            for j in range(CONV_LEN):
                off = j + shift0
                acc = acc + w_ref[j:j + 1, lanes] * window[off:off + rows, :]
            y_s[pl.ds(t0, rows), lanes] = acc + b_ref[:, lanes]
        return carry

    lax.fori_loop(0, t_len // rows, conv_step, 0)

    def step(i, carry):
        t0 = pl.multiple_of(i * rows, rows)
        y = y_s[pl.ds(t0, rows), :]
        mu = jnp.mean(y, axis=-1, keepdims=True)
        yc = y - mu
        var = jnp.mean(yc * yc, axis=-1, keepdims=True)
        z = yc * lax.rsqrt(var + EPS) * lnw_ref[...] + lnb_ref[...]
        o_ref[pl.ds(t0, rows), :] = _silu(z).astype(o_ref.dtype)
        return carry

    lax.fori_loop(0, t_len // rows, step, 0, unroll=4)


def _conv(proj, dw_w, dw_b, ln_w, ln_b, batch, t_len, first_col_block):
    ch = dw_w.shape[1]
    vec = pl.BlockSpec((1, ch), lambda b: (0, 0))
    return pl.pallas_call(
        _conv_body,
        grid=(batch,),
        in_specs=[pl.BlockSpec((t_len, ch), lambda b: (b, first_col_block)),
                  pl.BlockSpec((t_len, ch), lambda b: (b, first_col_block + 1)),
                  pl.BlockSpec((CONV_LEN, ch), lambda b: (0, 0)), vec, vec, vec],
        out_specs=pl.BlockSpec((t_len, ch), lambda b: (b, 0)),
        out_shape=jax.ShapeDtypeStruct((batch * t_len, ch), BF16),
        scratch_shapes=[pltpu.VMEM((ch // LANES, t_len + 2 * CONV_HALO, LANES), F32),
                        pltpu.VMEM((t_len, ch), F32)],
        compiler_params=_cparams(("parallel",)),
        name="conv",
    )(proj, proj, dw_w, dw_b.reshape(1, ch), ln_w.reshape(1, ch), ln_b.reshape(1, ch))


def _mix_body(a_ref, b_ref, x_ref, wa_ref, wb_ref, nw_ref, wr_ref, rb_ref, tri_ref, elow_ref,
              h_ref, n2_ref, dloc_ref, gate_ref, cnt_ref):
    tm = x_ref.shape[0]
    ne = wr_ref.shape[0] // 2

    h = x_ref[...] + jnp.dot(a_ref[...], wa_ref[...], preferred_element_type=F32) \
        + jnp.dot(b_ref[...], wb_ref[...], preferred_element_type=F32)
    h_ref[...] = h
    n2 = h * lax.rsqrt(jnp.mean(h * h, axis=-1, keepdims=True) + EPS) * nw_ref[...]
    hi = n2.astype(BF16)
    n2_ref[...] = hi
    lo = (n2 - hi.astype(F32)).astype(BF16)
    by_hi = lax.dot_general(wr_ref[...], hi, _NT, preferred_element_type=F32)
    by_lo = lax.dot_general(wr_ref[0:ne, :], lo, _NT, preferred_element_type=F32)
    logits = by_hi[0:ne] + by_hi[ne:] + by_lo + rb_ref[...]
    e_iota = lax.broadcasted_iota(I32, (ne, tm), 0)
    work = logits
    sels, vals = [], []
    for _ in range(TOP_K):
        mx = jnp.max(work, axis=0, keepdims=True)
        idx = jnp.min(jnp.where(work == mx, e_iota, ne), axis=0, keepdims=True)
        sel = e_iota == idx
        work = jnp.where(sel, -jnp.inf, work)
        sels.append(sel)
        vals.append(mx)
    exps = [jnp.exp(v - vals[0]) for v in vals]
    denom = exps[0] + exps[1] + exps[2] + exps[3]
    chosen = jnp.zeros((ne, tm), F32)
    for sel in sels:
        chosen = jnp.where(sel, 1.0, chosen)
    chosen_bf = chosen.astype(BF16)
    prior = jnp.dot(chosen_bf, tri_ref[...], preferred_element_type=F32)
    cnt = jnp.dot(chosen_bf, jnp.ones((tm, LANES), BF16), preferred_element_type=F32)
    cnt_ref[0] = cnt
    share = jnp.floor((cnt + (SUBLANES - 1)) * (1.0 / SUBLANES)) * SUBLANES
    lower = jnp.dot(elow_ref[...], share.astype(BF16), preferred_element_type=F32)
    slab_row = prior + jnp.concatenate([lower] * (tm // LANES), axis=1)
    for k in range(TOP_K):
        dloc_ref[k:k + 1, :] = jnp.sum(jnp.where(sels[k], slab_row, 0.0), axis=0, keepdims=True).astype(I32)
        gate_ref[k:k + 1, :] = exps[k] / denom


def _mix(a, b, x2, w_out_bf, norm_w, router_w, router_b, tm=ROUTE_TILE):
    n_tok, d = x2.shape
    wa, wb = w_out_bf[:a.shape[1]], w_out_bf[a.shape[1]:]
    ne = router_w.shape[1]
    wr_t = router_w.T
    wr_hi = wr_t.astype(BF16)
    wr_lo = (wr_t - wr_hi.astype(F32)).astype(BF16)
    tri = jnp.asarray(np.triu(np.ones((tm, tm), np.float32), k=1), BF16)
    e_lower = jnp.asarray(np.tril(np.ones((ne, ne), np.float32), k=-1), BF16)
    const = lambda shape: pl.BlockSpec(shape, lambda i: tuple(0 for _ in shape))
    return pl.pallas_call(
        _mix_body,
        grid=(n_tok // tm,),
        in_specs=[pl.BlockSpec((tm, a.shape[1]), lambda i: (i, 0)),
                  pl.BlockSpec((tm, b.shape[1]), lambda i: (i, 0)),
                  pl.BlockSpec((tm, d), lambda i: (i, 0)),
                  const(wa.shape), const(wb.shape), const((1, d)),
                  const((2 * ne, d)), const((ne, 1)), const((tm, tm)), const((ne, ne))],
        out_specs=[pl.BlockSpec((tm, d), lambda i: (i, 0)),
                   pl.BlockSpec((tm, d), lambda i: (i, 0)),
                   pl.BlockSpec((TOP_K, tm), lambda i: (0, i)),
                   pl.BlockSpec((TOP_K, tm), lambda i: (0, i)),
                   pl.BlockSpec((1, ne, LANES), lambda i: (i, 0, 0))],
        out_shape=[jax.ShapeDtypeStruct((n_tok, d), F32),
                   jax.ShapeDtypeStruct((n_tok, d), BF16),
                   jax.ShapeDtypeStruct((TOP_K, n_tok), I32),
                   jax.ShapeDtypeStruct((TOP_K, n_tok), F32),
                   jax.ShapeDtypeStruct((n_tok // tm, ne, LANES), F32)],
        compiler_params=_cparams(("parallel",)),
        name="mix_router",
    )(a, b, x2, wa, wb, norm_w.reshape(1, d), jnp.concatenate([wr_hi, wr_lo], axis=0), router_b.reshape(ne, 1),
      tri, e_lower)


SLAB_ROWS = TOP_K * ROUTE_TILE + N_EXPERTS * SUBLANES
SLAB_CHUNK = 256
SLAB_PIECES = (64, 32, 16, 8, 4, 2, 1)
SLAB_COMMON = ROUTE_TILE * TOP_K // N_EXPERTS // SUBLANES


def _for_each_piece(ngroups, fn):
    def pieces(sizes):
        for size in sizes:
            first = lax.bitwise_and(ngroups, ~(2 * size - 1))

            @pl.when(lax.bitwise_and(ngroups, size) != 0)
            def _():
                fn(first, size)

    rare = tuple(s for s in SLAB_PIECES if s >= 2 * SLAB_COMMON)

    @pl.when(ngroups >= min(rare))
    def _():
        pieces(rare)

    pieces(tuple(s for s in SLAB_PIECES if s < 2 * SLAB_COMMON))


def _group_rows(first_group, groups):
    return pl.ds(pl.multiple_of(first_group * SUBLANES, SUBLANES), groups * SUBLANES)


def _wait_groups(ngroups, stage_slot, hbm, sem):
    for size in (256, 128) + SLAB_PIECES:
        @pl.when(lax.bitwise_and(ngroups, size) != 0)
        def _():
            pltpu.make_async_copy(hbm.at[_group_rows(0, size)], stage_slot.at[_group_rows(0, size)], sem).wait()


def _dispatch_body(src_ref, dst_ref, ngrp_ref, tot_ref, n2_ref, dloc_ref, xs_hbm, stage, sem):
    step = pl.program_id(0)
    tm = n2_ref.shape[0]
    slot = step % 2

    def shares(at_step, at_slot, act):
        def per_expert(e, carry):
            idx = at_step * N_EXPERTS + e
            src, dst = src_ref[idx], dst_ref[idx]

            def piece(first, size):
                act(pltpu.make_async_copy(stage.at[at_slot, _group_rows(src + first, size)],
                                          xs_hbm.at[_group_rows(dst + first, size)], sem.at[at_slot]))

            _for_each_piece(ngrp_ref[idx], piece)
            return carry

        lax.fori_loop(0, N_EXPERTS, per_expert, 0)

    def wait_slab(at_step, at_slot):
        _wait_groups(tot_ref[at_step], stage.at[at_slot], xs_hbm, sem.at[at_slot])

    @pl.when(step >= 2)
    def _():
        wait_slab(step - 2, slot)

    n2 = n2_ref[...]
    dloc = dloc_ref[...]
    for c in range(SLAB_ROWS // SLAB_CHUNK):
        rows = lax.broadcasted_iota(I32, (SLAB_CHUNK, tm), 0) + c * SLAB_CHUNK
        pick = jnp.zeros((SLAB_CHUNK, tm), F32)
        for k in range(TOP_K):
            pick = jnp.where(dloc[k:k + 1, :] == rows, 1.0, pick)
        stage[slot, c * SLAB_CHUNK:(c + 1) * SLAB_CHUNK, :] = jnp.dot(pick.astype(BF16), n2,
                                                                       preferred_element_type=F32)
    shares(step, slot, lambda cp: cp.start())

    @pl.when(step == pl.num_programs(0) - 1)
    def _():
        wait_slab(step, slot)

        @pl.when(step >= 1)
        def _():
            wait_slab(step - 1, 1 - slot)


def _dispatch(shares, n2_bf, dloc, n_rows):
    n_tok, d = n2_bf.shape
    tm = ROUTE_TILE
    return pl.pallas_call(
        _dispatch_body,
        grid_spec=pltpu.PrefetchScalarGridSpec(
            num_scalar_prefetch=4,
            grid=(n_tok // tm,),
            in_specs=[pl.BlockSpec((tm, d), lambda i, *_: (i, 0)),
                      pl.BlockSpec((TOP_K, tm), lambda i, *_: (0, i))],
            out_specs=pl.BlockSpec(memory_space=pl.ANY),
            scratch_shapes=[pltpu.VMEM((2, SLAB_ROWS, d), F32), pltpu.SemaphoreType.DMA((2,))]),
        out_shape=jax.ShapeDtypeStruct((n_rows, d), F32),
        compiler_params=_cparams(("arbitrary",)),
        name="dispatch",
    )(*shares, n2_bf, dloc)


FFN_CAST_ROWS = 64
FFN_STEP_BLOCKS = 2


def _ffn_body(be_ref, first_ref, slot_ref, next_ref, nact_ref, xs_ref, w1_hbm, b1_ref, w2_hbm, b2_ref, ys_ref,
              w1_f, w2_f, w1_b, w2_b, sem):
    first_block = pl.program_id(0) * FFN_STEP_BLOCKS
    n_active = nact_ref[0]

    def fetch(expert, slot):
        return (pltpu.make_async_copy(w1_hbm.at[expert], w1_f.at[slot], sem.at[0, slot]),
                pltpu.make_async_copy(w2_hbm.at[expert], w2_f.at[slot], sem.at[1, slot]))

    def start_run(blk):
        slot = slot_ref[blk]

        @pl.when(blk == 0)
        def _():
            for cp in fetch(be_ref[0], slot):
                cp.start()

        for cp in fetch(be_ref[blk], slot):
            cp.wait()

        @pl.when(next_ref[blk] >= 0)
        def _():
            for cp in fetch(next_ref[blk], 1 - slot):
                cp.start()

        def cast(src, dst):
            def body(c, carry):
                rows = pl.ds(pl.multiple_of(c * FFN_CAST_ROWS, FFN_CAST_ROWS), FFN_CAST_ROWS)
                dst[rows, :] = src[slot, rows, :].astype(BF16)
                return carry
            lax.fori_loop(0, dst.shape[0] // FFN_CAST_ROWS, body, 0)

        cast(w1_f, w1_b)
        cast(w2_f, w2_b)

    def compute(blk, lo, hi):
        expert = be_ref[blk]
        x = xs_ref[lo:hi, :].astype(BF16)
        hdn = jnp.dot(x, w1_b[...], preferred_element_type=F32) + b1_ref[expert]
        d_ff = hdn.shape[1] // 2
        glu = jnp.minimum(hdn[:, :d_ff], SWIGLU_LIMIT)
        lin = jnp.clip(hdn[:, d_ff:], -SWIGLU_LIMIT, SWIGLU_LIMIT)
        act = glu * _sigmoid(SWIGLU_ALPHA * glu) * (lin + 1.0)
        ys_ref[lo:hi, :] = jnp.dot(act.astype(BF16), w2_b[...], preferred_element_type=F32) + b2_ref[expert]

    second = first_block + 1

    @pl.when(first_block < n_active)
    def _():
        @pl.when(first_ref[first_block] == 1)
        def _():
            start_run(first_block)

        same_run = jnp.logical_and(second < n_active, first_ref[second] == 0)

        @pl.when(same_run)
        def _():
            compute(first_block, 0, 2 * MOE_BLOCK)

        @pl.when(jnp.logical_not(same_run))
        def _():
            compute(first_block, 0, MOE_BLOCK)

            @pl.when(second < n_active)
            def _():
                start_run(second)
                compute(second, MOE_BLOCK, 2 * MOE_BLOCK)


def _ffn(block_expert, n_active, xs, w1, b1, w2, b2):
    ne, d, f2 = w1.shape
    nb = xs.shape[0] // MOE_BLOCK
    assert nb % FFN_STEP_BLOCKS == 0
    block = (FFN_STEP_BLOCKS * MOE_BLOCK, d)
    idx = jnp.arange(nb, dtype=I32)
    active = idx < n_active[0]
    prev = jnp.concatenate([block_expert[:1] - 1, block_expert[:-1]])
    first = (active & (block_expert != prev)).astype(I32)
    run_slot = (jnp.sum(jnp.where(idx[None, :] <= idx[:, None], first[None, :], 0), axis=1) - 1) & 1
    later_first = (first[None, :] == 1) & (idx[None, :] > idx[:, None])
    next_block = jnp.min(jnp.where(later_first, idx[None, :], nb), axis=1)
    next_expert = jnp.where(next_block < nb, block_expert[jnp.minimum(next_block, nb - 1)], -1).astype(I32)

    def blk(i, be, fi, sl, nx, nact):
        return (jnp.minimum(i, (nact[0] + FFN_STEP_BLOCKS - 1) // FFN_STEP_BLOCKS - 1), 0)

    whole = lambda shape: pl.BlockSpec(shape, lambda i, *_: (0,) * len(shape))

    return pl.pallas_call(
        _ffn_body,
        grid_spec=pltpu.PrefetchScalarGridSpec(
            num_scalar_prefetch=5,
            grid=(nb // FFN_STEP_BLOCKS,),
            in_specs=[pl.BlockSpec(block, blk),
                      pl.BlockSpec(memory_space=pl.ANY),
                      whole((ne, 1, f2)),
                      pl.BlockSpec(memory_space=pl.ANY),
                      whole((ne, 1, d))],
            out_specs=pl.BlockSpec(block, blk),
            scratch_shapes=[pltpu.VMEM((2, d, f2), F32), pltpu.VMEM((2, f2 // 2, d), F32),
                            pltpu.VMEM((d, f2), BF16), pltpu.VMEM((f2 // 2, d), BF16),
                            pltpu.SemaphoreType.DMA((2, 2))]),
        out_shape=jax.ShapeDtypeStruct(xs.shape, F32),
        compiler_params=_cparams(("arbitrary",)),
        name="expert_ffn",
    )(block_expert, first, run_slot.astype(I32), next_expert, n_active, xs, w1, b1.reshape(ne, 1, f2), w2,
      b2.reshape(ne, 1, d))


def _combine_body(src_ref, dst_ref, ngrp_ref, tot_ref, h_ref, dloc_ref, gate_ref, nw_ref, ys_hbm, o_ref, stage,
                  sem):
    step = pl.program_id(0)
    tm = h_ref.shape[0]
    slot = step % 2

    def shares(at_step, at_slot, act):
        def per_expert(e, carry):
            idx = at_step * N_EXPERTS + e
            src, dst = src_ref[idx], dst_ref[idx]

            def piece(first, size):
                act(pltpu.make_async_copy(ys_hbm.at[_group_rows(dst + first, size)],
                                          stage.at[at_slot, _group_rows(src + first, size)], sem.at[at_slot]))

            _for_each_piece(ngrp_ref[idx], piece)
            return carry

        lax.fori_loop(0, N_EXPERTS, per_expert, 0)

    @pl.when(step == 0)
    def _():
        stage[...] = jnp.zeros(stage.shape, F32)
        shares(0, 0, lambda cp: cp.start())

    @pl.when(step + 1 < pl.num_programs(0))
    def _():
        shares(step + 1, 1 - slot, lambda cp: cp.start())

    _wait_groups(tot_ref[step], stage.at[slot], ys_hbm, sem.at[slot])
    dloc = dloc_ref[...]
    gates = gate_ref[...]
    moe = jnp.zeros(o_ref.shape, F32)
    for c in range(SLAB_ROWS // SLAB_CHUNK):
        cols = lax.broadcasted_iota(I32, (tm, SLAB_CHUNK), 1) + c * SLAB_CHUNK
        weight = jnp.zeros((tm, SLAB_CHUNK), F32)
        for k in range(TOP_K):
            weight = jnp.where(dloc[:, k:k + 1] == cols, gates[:, k:k + 1], weight)
        rows = stage[slot, c * SLAB_CHUNK:(c + 1) * SLAB_CHUNK, :].astype(BF16)
        moe = moe + jnp.dot(weight.astype(BF16), rows, preferred_element_type=F32)
    y = h_ref[...] + moe
    o_ref[...] = y * lax.rsqrt(jnp.mean(y * y, axis=-1, keepdims=True) + EPS) * nw_ref[...]


def _combine(shares, h, dloc_tk, gates_tk, norm_w, ys):
    n_tok, d = h.shape
    tm = ROUTE_TILE
    return pl.pallas_call(
        _combine_body,
        grid_spec=pltpu.PrefetchScalarGridSpec(
            num_scalar_prefetch=4,
            grid=(n_tok // tm,),
            in_specs=[pl.BlockSpec((tm, d), lambda i, *_: (i, 0)),
                      pl.BlockSpec((tm, TOP_K), lambda i, *_: (i, 0)),
                      pl.BlockSpec((tm, TOP_K), lambda i, *_: (i, 0)),
                      pl.BlockSpec((1, d), lambda i, *_: (0, 0)),
                      pl.BlockSpec(memory_space=pl.ANY)],
            out_specs=pl.BlockSpec((tm, d), lambda i, *_: (i, 0)),
            scratch_shapes=[pltpu.VMEM((2, SLAB_ROWS, d), F32), pltpu.SemaphoreType.DMA((2,))]),
        out_shape=jax.ShapeDtypeStruct((n_tok, d), F32),
        compiler_params=_cparams(("arbitrary",)),
        name="combine_norm",
    )(*shares, h, dloc_tk, gates_tk, norm_w.reshape(1, d), ys)


def kernel(x, norm1_w, w_in, lb_logits, hgrn_norm_w, dw_w, dw_b, conv_ln_w, conv_ln_b, w_out, norm2_w,
           router_w, router_b, w1, b1, w2, b2, final_norm_w):
    batch, t_len, d = x.shape
    assert w_in.shape[0] == 1, "single-layer block"
    n_tok = batch * t_len
    hk = HGRN_HEADS * HEAD_DIM
    conv_ch = dw_w.shape[2]
    lb_table = jnp.cumsum(jax.nn.softmax(lb_logits.astype(F32), axis=1), axis=1)
    x2 = x.reshape(n_tok, d)
    proj = _in_proj(x2, norm1_w[0], w_in[0].astype(BF16))
    a = _hgrn(proj, lb_table[0, 0], lb_table[1, 0], hgrn_norm_w[0], batch, t_len)
    b = _conv(proj, dw_w[0], dw_b[0], conv_ln_w[0], conv_ln_b[0], batch, t_len, (5 * hk) // conv_ch)
    h_mid, n2_bf, dloc, gate, cnt_tiles = _mix(a, b, x2, w_out[0].astype(BF16), norm2_w[0],
                                                     router_w[0], router_b[0])
    nt = n_tok // ROUTE_TILE
    e_ids = jnp.arange(N_EXPERTS, dtype=I32)
    t_ids = jnp.arange(nt, dtype=I32)
    share = (cnt_tiles[:, :, 0].astype(I32) + SUBLANES - 1) // SUBLANES * SUBLANES
    in_slab = jnp.sum(jnp.where(e_ids[None, None, :] < e_ids[None, :, None], share[:, None, :], 0), axis=2)
    in_expert = jnp.sum(jnp.where((t_ids[None, :] < t_ids[:, None])[:, :, None], share[None, :, :], 0), axis=1)
    padded = (jnp.sum(share, axis=0) + MOE_BLOCK - 1) // MOE_BLOCK * MOE_BLOCK
    pad_end = jnp.sum(jnp.where(e_ids[None, :] <= e_ids[:, None], padded[None, :], 0), axis=1)
    in_xs = (pad_end - padded)[None, :] + in_expert
    n_blocks = -(-(n_tok * TOP_K + nt * N_EXPERTS * (SUBLANES - 1)) // MOE_BLOCK) + N_EXPERTS
    n_blocks += -n_blocks % FFN_STEP_BLOCKS
    block_start = jnp.arange(n_blocks, dtype=I32) * MOE_BLOCK
    block_expert = jnp.minimum(jnp.sum((pad_end[None, :] <= block_start[:, None]).astype(I32), axis=1),
                               N_EXPERTS - 1)
    n_active = pad_end[-1:] // MOE_BLOCK
    shares = [(v // SUBLANES).reshape(-1) for v in (in_slab, in_xs, share)] + [jnp.sum(share, axis=1) // SUBLANES]
    xs = _dispatch(shares, n2_bf, dloc, n_blocks * MOE_BLOCK)
    ys = _ffn(block_expert, n_active, xs, w1[0], b1[0], w2[0], b2[0])
    out = _combine(shares, h_mid, dloc.T, gate.T, final_norm_w, ys)
    return out.reshape(batch, t_len, d)
```

```python
import functools
import math

import numpy as np
import jax
import jax.numpy as jnp
from jax import lax
from jax.experimental import pallas as pl
from jax.experimental.pallas import tpu as pltpu

F32 = jnp.float32
BF16 = jnp.bfloat16
I32 = jnp.int32

EPS = 1e-5
HGRN_HEADS = 4
HEAD_DIM = 128
HGRN_CHUNK = 256
CONV_LEN = 31
N_EXPERTS = 32
TOP_K = 4
SWIGLU_LIMIT = 7.0
SWIGLU_ALPHA = 1.702
MOE_BLOCK = 256
ROUTE_TILE = 512
LANES = 128
SUBLANES = 8
VMEM_LIMIT = 56 << 20

_NT = (((1,), (1,)), ((), ()))


def _sigmoid(x):
    return 0.5 * jnp.tanh(0.5 * x) + 0.5


def _silu(x):
    return x * _sigmoid(x)


def _cparams(sem):
    return pltpu.CompilerParams(dimension_semantics=sem, vmem_limit_bytes=VMEM_LIMIT)


def _inproj_body(x_ref, nw_ref, w_ref, o_ref):
    x = x_ref[...]
    n = x * lax.rsqrt(jnp.mean(x * x, axis=-1, keepdims=True) + EPS) * nw_ref[...]
    o_ref[...] = jnp.dot(n.astype(BF16), w_ref[...], preferred_element_type=F32)


def _in_proj(x2, norm_w, w_bf, tm=512):
    n_tok, d = x2.shape
    cols = w_bf.shape[1]
    return pl.pallas_call(
        _inproj_body,
        grid=(n_tok // tm,),
        in_specs=[pl.BlockSpec((tm, d), lambda i: (i, 0)),
                  pl.BlockSpec((1, d), lambda i: (0, 0)),
                  pl.BlockSpec((d, cols), lambda i: (0, 0))],
        out_specs=pl.BlockSpec((tm, cols), lambda i: (i, 0)),
        out_shape=jax.ShapeDtypeStruct((n_tok, cols), F32),
        compiler_params=_cparams(("parallel",)),
        name="in_proj",
    )(x2, norm_w.reshape(1, d), w_bf)


def _hgrn_levels(c):
    return [c >> (i + 1) for i in range(int(math.log2(c)))]


def _hgrn_sum_matrices(c, fwd):
    r = np.arange(c)
    m = _HGRN_SMALL_LEVEL
    small = np.zeros((c, c), np.float32)
    for t in range(c):
        p0 = t & ~(2 * m - 1)
        upper = (t & m) != 0
        if fwd:
            if upper:
                small[t, p0 + m:t + 1] = 1.0
            else:
                small[t, t + 1:p0 + m] = 1.0
        else:
            if upper:
                small[t, p0 + m:t] = 1.0
            else:
                small[t, t:p0 + m] = 1.0
    mats = [r[None, :] <= r[:, None] if fwd else r[None, :] >= r[:, None], small]
    return np.concatenate([np.asarray(x, np.float32) for x in mats], axis=0)


_HGRN_SMALL_LEVEL = 2
_HGRN_ROW_SLICE = 16


def _hgrn_chunk(q, k, lf, v_bf, vt_bf, st, sums_ref, fwd, c):
    dk = q.shape[1]
    hi = lf.astype(BF16)
    lo = (lf - hi.astype(F32)).astype(BF16)
    both = jnp.dot(sums_ref[...], jnp.concatenate([hi, lo], axis=1), preferred_element_type=F32)
    sums = both[:, :dk] + both[:, dk:]
    b = sums[0:c]
    edge_row = c - 1 if fwd else 0
    d_out = b[edge_row:edge_row + 1] - b
    row = lax.broadcasted_iota(I32, (c, dk), 0)
    rr = lax.broadcasted_iota(I32, (c, c), 0)
    cc = lax.broadcasted_iota(I32, (c, c), 1)
    sep = rr ^ cc
    a = None
    for m in _hgrn_levels(c):
        upper = (row & m) != 0
        is_q = upper if fwd else jnp.logical_not(upper)
        if m == 1:
            d = jnp.where(is_q, lf, 0.0)
        elif m == _HGRN_SMALL_LEVEL:
            d = sums[c:2 * c]
        else:
            blocks = c // (2 * m)
            at = m - 1 if fwd else m
            edge = b.reshape(blocks, 2 * m, dk)[:, at:at + 1, :]
            edge = jnp.broadcast_to(edge, (blocks, 2 * m, dk)).reshape(c, dk)
            d = jnp.where(is_q, b - edge, edge - b)
        r = (jnp.where(is_q, q, k) * jnp.exp2(d)).astype(BF16)
        if m < _HGRN_ROW_SLICE:
            a_l = lax.dot_general(r, r, _NT, preferred_element_type=F32)
            a = a_l if a is None else jnp.where(sep < 2 * m, a_l, a)
        else:
            halves = [(p0 + m, p0 + 2 * m) if fwd else (p0, p0 + m) for p0 in range(0, c, 2 * m)]
            a_q = lax.dot_general(jnp.concatenate([r[lo:hi] for lo, hi in halves], axis=0), r, _NT,
                                  preferred_element_type=F32)
            parts = []
            for n, (lo, hi) in enumerate(halves):
                new = a_q[n * m:(n + 1) * m]
                if a is None:
                    kept = jnp.zeros((m, c), F32)
                else:
                    new = jnp.where(sep[lo:hi] < 2 * m, new, a[lo:hi])
                    kept = a[lo - m:lo] if fwd else a[hi:hi + m]
                parts += [kept, new] if fwd else [new, kept]
            a = jnp.concatenate(parts, axis=0)
    a_diag = lax.dot_general(q.astype(BF16), k.astype(BF16), _NT, preferred_element_type=F32)
    a = jnp.where(sep < 1, a_diag, a)
    a = jnp.where(rr >= cc if fwd else rr <= cc, a, 0.0)
    e_b = jnp.exp2(b)
    q_in = (q * e_b).astype(BF16)
    k_out = (k * jnp.exp2(d_out)).astype(BF16)
    o = jnp.dot(a.astype(BF16), v_bf, preferred_element_type=F32)
    o = o + lax.dot_general(q_in, st.astype(BF16), _NT, preferred_element_type=F32)
    st_new = st * e_b[edge_row:edge_row + 1] + jnp.dot(vt_bf, k_out, preferred_element_type=F32)
    return o, st_new


def _hgrn_body(qz_ref, ff_ref, fb_ref, iv_ref, gz_ref, lbf_ref, lbb_ref, nw_ref, sf_ref, sb_ref, o_ref,
               q_s, kf_s, lff_s, kb_s, lfb_s, v_s, vt_s, acc_s, *, c):
    t_len, dk = qz_ref.shape
    nch = t_len // c
    q_s[...] = _silu(qz_ref[...])

    def gate(fz_ref, lb_ref, k_s, lf_s):
        lb = lb_ref[0]
        f = lb + (1.0 - lb) * _sigmoid(fz_ref[...])
        lf_s[...] = jnp.log2(f)
        k_s[...] = 1.0 - f

    gate(ff_ref, lbf_ref, kf_s, lff_s)
    gate(fb_ref, lbb_ref, kb_s, lfb_s)
    v = iv_ref[...]
    v_s[...] = v.astype(BF16)
    for ci in range(nch):
        vt_s[ci] = v[ci * c:(ci + 1) * c, :].T.astype(BF16)

    def run(k_s, lf_s, sums_ref, fwd):
        def step(i, st):
            ci = i if fwd else nch - 1 - i
            rows = pl.ds(pl.multiple_of(ci * c, c), c)
            o, st = _hgrn_chunk(q_s[rows, :], k_s[rows, :], lf_s[rows, :], v_s[rows, :], vt_s[ci],
                                st, sums_ref, fwd, c)
            if fwd:
                acc_s[rows, :] = o
            else:
                acc_s[rows, :] += o
            return st
        lax.fori_loop(0, nch, step, jnp.zeros((dk, dk), F32), unroll=True)

    run(kf_s, lff_s, sf_ref, True)
    run(kb_s, lfb_s, sb_ref, False)
    o = acc_s[...]
    o = o * lax.rsqrt(jnp.mean(o * o, axis=-1, keepdims=True) + EPS) * nw_ref[0]
    o_ref[...] = (o * _silu(gz_ref[...])).astype(o_ref.dtype)


def _hgrn(proj, lb_f, lb_b, norm_w, batch, t_len):
    h, dk, c = HGRN_HEADS, HEAD_DIM, HGRN_CHUNK
    nrows = 2
    sums_f = jnp.asarray(_hgrn_sum_matrices(c, True), BF16)
    sums_b = jnp.asarray(_hgrn_sum_matrices(c, False), BF16)

    def col(group):
        return pl.BlockSpec((t_len, dk), lambda b, hh: (b, group * h + hh))

    def per_head():
        return pl.BlockSpec((1, 1, dk), lambda b, hh: (hh, 0, 0))

    const = pl.BlockSpec((nrows * c, c), lambda b, hh: (0, 0))
    seq = lambda dt: pltpu.VMEM((t_len, dk), dt)
    return pl.pallas_call(
        functools.partial(_hgrn_body, c=c),
        grid=(batch, h),
        in_specs=[col(0), col(1), col(2), col(3), col(4), per_head(), per_head(), per_head(), const, const],
        out_specs=pl.BlockSpec((t_len, dk), lambda b, hh: (b, hh)),
        out_shape=jax.ShapeDtypeStruct((batch * t_len, h * dk), BF16),
        scratch_shapes=[seq(F32), seq(F32), seq(F32), seq(F32), seq(F32), seq(BF16),
                        pltpu.VMEM((t_len // c, dk, c), BF16), seq(F32)],
        compiler_params=_cparams(("parallel", "parallel")),
        name="hgrn",
    )(proj, proj, proj, proj, proj, lb_f.reshape(h, 1, dk), lb_b.reshape(h, 1, dk),
      norm_w.reshape(h, 1, dk), sums_f, sums_b)


CONV_ROWS = 64
CONV_HALO = 16


def _conv_body(cv_ref, cg_ref, w_ref, b_ref, lnw_ref, lnb_ref, o_ref, u_s, y_s):
    t_len, ch = cv_ref.shape
    halo, rows = CONV_HALO, CONV_ROWS
    shift0 = halo - (CONV_LEN - 1) // 2
    win = rows + 2 * halo
    for g in range(ch // LANES):
        lanes = slice(g * LANES, (g + 1) * LANES)
        u_s[g, 0:halo, :] = jnp.zeros((halo, LANES), F32)
        u_s[g, halo + t_len:, :] = jnp.zeros((halo, LANES), F32)
        u_s[g, halo:halo + t_len, :] = cv_ref[:, lanes] * _sigmoid(cg_ref[:, lanes])

    def conv_step(i, carry):
        t0 = pl.multiple_of(i * rows, rows)
        for g in range(ch // LANES):
            lanes = slice(g * LANES, (g + 1) * LANES)
            window = u_s.at[g, pl.ds(t0, win), :]
            acc = jnp.zeros((rows, LANES), F32)
            for j in range(CONV_LEN):
                off = j + shift0
                acc = acc + w_ref[j:j + 1, lanes] * window[off:off + rows, :]
            y_s[pl.ds(t0, rows), lanes] = acc + b_ref[:, lanes]
        return carry

    lax.fori_loop(0, t_len // rows, conv_step, 0)

    def step(i, carry):
        t0 = pl.multiple_of(i * rows, rows)
        y = y_s[pl.ds(t0, rows), :]
        mu = jnp.mean(y, axis=-1, keepdims=True)
        yc = y - mu
        var = jnp.mean(yc * yc, axis=-1, keepdims=True)
        z = yc * lax.rsqrt(var + EPS) * lnw_ref[...] + lnb_ref[...]
        o_ref[pl.ds(t0, rows), :] = _silu(z).astype(o_ref.dtype)
        return carry

    lax.fori_loop(0, t_len // rows, step, 0, unroll=4)


def _conv(proj, dw_w, dw_b, ln_w, ln_b, batch, t_len, first_col_block):
    ch = dw_w.shape[1]
    vec = pl.BlockSpec((1, ch), lambda b: (0, 0))
    return pl.pallas_call(
        _conv_body,
        grid=(batch,),
        in_specs=[pl.BlockSpec((t_len, ch), lambda b: (b, first_col_block)),
                  pl.BlockSpec((t_len, ch), lambda b: (b, first_col_block + 1)),
                  pl.BlockSpec((CONV_LEN, ch), lambda b: (0, 0)), vec, vec, vec],
        out_specs=pl.BlockSpec((t_len, ch), lambda b: (b, 0)),
        out_shape=jax.ShapeDtypeStruct((batch * t_len, ch), BF16),
        scratch_shapes=[pltpu.VMEM((ch // LANES, t_len + 2 * CONV_HALO, LANES), F32),
                        pltpu.VMEM((t_len, ch), F32)],
        compiler_params=_cparams(("parallel",)),
        name="conv",
    )(proj, proj, dw_w, dw_b.reshape(1, ch), ln_w.reshape(1, ch), ln_b.reshape(1, ch))


def _mix_body(a_ref, b_ref, x_ref, wa_ref, wb_ref, nw_ref, wr_ref, rb_ref, tri_ref, elow_ref,
              h_ref, n2_ref, dloc_ref, gate_ref, cnt_ref):
    tm = x_ref.shape[0]
    ne = wr_ref.shape[0] // 2

    h = x_ref[...] + jnp.dot(a_ref[...], wa_ref[...], preferred_element_type=F32) \
        + jnp.dot(b_ref[...], wb_ref[...], preferred_element_type=F32)
    h_ref[...] = h
    n2 = h * lax.rsqrt(jnp.mean(h * h, axis=-1, keepdims=True) + EPS) * nw_ref[...]
    hi = n2.astype(BF16)
    n2_ref[...] = hi
    lo = (n2 - hi.astype(F32)).astype(BF16)
    by_hi = lax.dot_general(wr_ref[...], hi, _NT, preferred_element_type=F32)
    by_lo = lax.dot_general(wr_ref[0:ne, :], lo, _NT, preferred_element_type=F32)
    logits = by_hi[0:ne] + by_hi[ne:] + by_lo + rb_ref[...]
    e_iota = lax.broadcasted_iota(I32, (ne, tm), 0)
    work = logits
    sels, vals = [], []
    for _ in range(TOP_K):
        mx = jnp.max(work, axis=0, keepdims=True)
        idx = jnp.min(jnp.where(work == mx, e_iota, ne), axis=0, keepdims=True)
        sel = e_iota == idx
        work = jnp.where(sel, -jnp.inf, work)
        sels.append(sel)
        vals.append(mx)
    exps = [jnp.exp(v - vals[0]) for v in vals]
    denom = exps[0] + exps[1] + exps[2] + exps[3]
    chosen = jnp.zeros((ne, tm), F32)
    for sel in sels:
        chosen = jnp.where(sel, 1.0, chosen)
    chosen_bf = chosen.astype(BF16)
    prior = jnp.dot(chosen_bf, tri_ref[...], preferred_element_type=F32)
    cnt = jnp.dot(chosen_bf, jnp.ones((tm, LANES), BF16), preferred_element_type=F32)
    cnt_ref[0] = cnt
    share = jnp.floor((cnt + (SUBLANES - 1)) * (1.0 / SUBLANES)) * SUBLANES
    lower = jnp.dot(elow_ref[...], share.astype(BF16), preferred_element_type=F32)
    slab_row = prior + jnp.concatenate([lower] * (tm // LANES), axis=1)
    for k in range(TOP_K):
        dloc_ref[k:k + 1, :] = jnp.sum(jnp.where(sels[k], slab_row, 0.0), axis=0, keepdims=True).astype(I32)
        gate_ref[k:k + 1, :] = exps[k] / denom


def _mix(a, b, x2, w_out_bf, norm_w, router_w, router_b, tm=ROUTE_TILE):
    n_tok, d = x2.shape
    wa, wb = w_out_bf[:a.shape[1]], w_out_bf[a.shape[1]:]
    ne = router_w.shape[1]
    wr_t = router_w.T
    wr_hi = wr_t.astype(BF16)
    wr_lo = (wr_t - wr_hi.astype(F32)).astype(BF16)
    tri = jnp.asarray(np.triu(np.ones((tm, tm), np.float32), k=1), BF16)
    e_lower = jnp.asarray(np.tril(np.ones((ne, ne), np.float32), k=-1), BF16)
    const = lambda shape: pl.BlockSpec(shape, lambda i: tuple(0 for _ in shape))
    return pl.pallas_call(
        _mix_body,
        grid=(n_tok // tm,),
        in_specs=[pl.BlockSpec((tm, a.shape[1]), lambda i: (i, 0)),
                  pl.BlockSpec((tm, b.shape[1]), lambda i: (i, 0)),
                  pl.BlockSpec((tm, d), lambda i: (i, 0)),
                  const(wa.shape), const(wb.shape), const((1, d)),
                  const((2 * ne, d)), const((ne, 1)), const((tm, tm)), const((ne, ne))],
        out_specs=[pl.BlockSpec((tm, d), lambda i: (i, 0)),
                   pl.BlockSpec((tm, d), lambda i: (i, 0)),
                   pl.BlockSpec((TOP_K, tm), lambda i: (0, i)),
                   pl.BlockSpec((TOP_K, tm), lambda i: (0, i)),
                   pl.BlockSpec((1, ne, LANES), lambda i: (i, 0, 0))],
        out_shape=[jax.ShapeDtypeStruct((n_tok, d), F32),
                   jax.ShapeDtypeStruct((n_tok, d), BF16),
                   jax.ShapeDtypeStruct((TOP_K, n_tok), I32),
                   jax.ShapeDtypeStruct((TOP_K, n_tok), F32),
                   jax.ShapeDtypeStruct((n_tok // tm, ne, LANES), F32)],
        compiler_params=_cparams(("parallel",)),
        name="mix_router",
    )(a, b, x2, wa, wb, norm_w.reshape(1, d), jnp.concatenate([wr_hi, wr_lo], axis=0), router_b.reshape(ne, 1),
      tri, e_lower)


def _powers_of_two_upto(n):
    return tuple(1 << i for i in reversed(range(n.bit_length())))


SLAB_ROWS = TOP_K * ROUTE_TILE + N_EXPERTS * SUBLANES
SLAB_CHUNK = 256
SLAB_PIECES = _powers_of_two_upto(ROUTE_TILE // SUBLANES)
SLAB_COMMON = ROUTE_TILE * TOP_K // N_EXPERTS // SUBLANES


def _for_each_piece(ngroups, fn):
    def pieces(sizes):
        for size in sizes:
            first = lax.bitwise_and(ngroups, ~(2 * size - 1))

            @pl.when(lax.bitwise_and(ngroups, size) != 0)
            def _():
                fn(first, size)

    rare = tuple(s for s in SLAB_PIECES if s >= 2 * SLAB_COMMON)

    @pl.when(ngroups >= min(rare))
    def _():
        pieces(rare)

    pieces(tuple(s for s in SLAB_PIECES if s < 2 * SLAB_COMMON))


def _group_rows(first_group, groups):
    return pl.ds(pl.multiple_of(first_group * SUBLANES, SUBLANES), groups * SUBLANES)


def _wait_groups(ngroups, stage_slot, hbm, sem):
    for size in _powers_of_two_upto(SLAB_ROWS // SUBLANES):
        @pl.when(lax.bitwise_and(ngroups, size) != 0)
        def _():
            pltpu.make_async_copy(hbm.at[_group_rows(0, size)], stage_slot.at[_group_rows(0, size)], sem).wait()


def _dispatch_body(src_ref, dst_ref, ngrp_ref, tot_ref, n2_ref, dloc_ref, xs_hbm, stage, sem):
    step = pl.program_id(0)
    tm = n2_ref.shape[0]
    slot = step % 2

    def shares(at_step, at_slot, act):
        def per_expert(e, carry):
            idx = at_step * N_EXPERTS + e
            src, dst = src_ref[idx], dst_ref[idx]

            def piece(first, size):
                act(pltpu.make_async_copy(stage.at[at_slot, _group_rows(src + first, size)],
                                          xs_hbm.at[_group_rows(dst + first, size)], sem.at[at_slot]))

            _for_each_piece(ngrp_ref[idx], piece)
            return carry

        lax.fori_loop(0, N_EXPERTS, per_expert, 0)

    def wait_slab(at_step, at_slot):
        _wait_groups(tot_ref[at_step], stage.at[at_slot], xs_hbm, sem.at[at_slot])

    @pl.when(step >= 2)
    def _():
        wait_slab(step - 2, slot)

    n2 = n2_ref[...]
    dloc = dloc_ref[...]
    for c in range(SLAB_ROWS // SLAB_CHUNK):
        rows = lax.broadcasted_iota(I32, (SLAB_CHUNK, tm), 0) + c * SLAB_CHUNK
        pick = jnp.zeros((SLAB_CHUNK, tm), F32)
        for k in range(TOP_K):
            pick = jnp.where(dloc[k:k + 1, :] == rows, 1.0, pick)
        stage[slot, c * SLAB_CHUNK:(c + 1) * SLAB_CHUNK, :] = jnp.dot(pick.astype(BF16), n2,
                                                                       preferred_element_type=F32)
    shares(step, slot, lambda cp: cp.start())

    @pl.when(step == pl.num_programs(0) - 1)
    def _():
        wait_slab(step, slot)

        @pl.when(step >= 1)
        def _():
            wait_slab(step - 1, 1 - slot)


def _dispatch(shares, n2_bf, dloc, n_rows):
    n_tok, d = n2_bf.shape
    tm = ROUTE_TILE
    return pl.pallas_call(
        _dispatch_body,
        grid_spec=pltpu.PrefetchScalarGridSpec(
            num_scalar_prefetch=4,
            grid=(n_tok // tm,),
            in_specs=[pl.BlockSpec((tm, d), lambda i, *_: (i, 0)),
                      pl.BlockSpec((TOP_K, tm), lambda i, *_: (0, i))],
            out_specs=pl.BlockSpec(memory_space=pl.ANY),
            scratch_shapes=[pltpu.VMEM((2, SLAB_ROWS, d), F32), pltpu.SemaphoreType.DMA((2,))]),
        out_shape=jax.ShapeDtypeStruct((n_rows, d), F32),
        compiler_params=_cparams(("arbitrary",)),
        name="dispatch",
    )(*shares, n2_bf, dloc)


FFN_CAST_ROWS = 64
FFN_STEP_BLOCKS = 2


def _ffn_body(be_ref, first_ref, slot_ref, next_ref, nact_ref, xs_ref, w1_hbm, b1_ref, w2_hbm, b2_ref, ys_ref,
              w1_f, w2_f, w1_b, w2_b, sem):
    first_block = pl.program_id(0) * FFN_STEP_BLOCKS
    n_active = nact_ref[0]

    def fetch(expert, slot):
        return (pltpu.make_async_copy(w1_hbm.at[expert], w1_f.at[slot], sem.at[0, slot]),
                pltpu.make_async_copy(w2_hbm.at[expert], w2_f.at[slot], sem.at[1, slot]))

    def start_run(blk):
        slot = slot_ref[blk]

        @pl.when(blk == 0)
        def _():
            for cp in fetch(be_ref[0], slot):
                cp.start()

        for cp in fetch(be_ref[blk], slot):
            cp.wait()

        @pl.when(next_ref[blk] >= 0)
        def _():
            for cp in fetch(next_ref[blk], 1 - slot):
                cp.start()

        def cast(src, dst):
            def body(c, carry):
                rows = pl.ds(pl.multiple_of(c * FFN_CAST_ROWS, FFN_CAST_ROWS), FFN_CAST_ROWS)
                dst[rows, :] = src[slot, rows, :].astype(BF16)
                return carry
            lax.fori_loop(0, dst.shape[0] // FFN_CAST_ROWS, body, 0)

        cast(w1_f, w1_b)
        cast(w2_f, w2_b)

    def compute(blk, lo, hi):
        expert = be_ref[blk]
        x = xs_ref[lo:hi, :].astype(BF16)
        hdn = jnp.dot(x, w1_b[...], preferred_element_type=F32) + b1_ref[expert]
        d_ff = hdn.shape[1] // 2
        glu = jnp.minimum(hdn[:, :d_ff], SWIGLU_LIMIT)
        lin = jnp.clip(hdn[:, d_ff:], -SWIGLU_LIMIT, SWIGLU_LIMIT)
        act = glu * _sigmoid(SWIGLU_ALPHA * glu) * (lin + 1.0)
        ys_ref[lo:hi, :] = jnp.dot(act.astype(BF16), w2_b[...], preferred_element_type=F32) + b2_ref[expert]

    second = first_block + 1

    @pl.when(first_block < n_active)
    def _():
        @pl.when(first_ref[first_block] == 1)
        def _():
            start_run(first_block)

        same_run = jnp.logical_and(second < n_active, first_ref[second] == 0)

        @pl.when(same_run)
        def _():
            compute(first_block, 0, 2 * MOE_BLOCK)

        @pl.when(jnp.logical_not(same_run))
        def _():
            compute(first_block, 0, MOE_BLOCK)

            @pl.when(second < n_active)
            def _():
                start_run(second)
                compute(second, MOE_BLOCK, 2 * MOE_BLOCK)


def _ffn(block_expert, n_active, xs, w1, b1, w2, b2):
    ne, d, f2 = w1.shape
    nb = xs.shape[0] // MOE_BLOCK
    assert nb % FFN_STEP_BLOCKS == 0
    block = (FFN_STEP_BLOCKS * MOE_BLOCK, d)
    idx = jnp.arange(nb, dtype=I32)
    active = idx < n_active[0]
    prev = jnp.concatenate([block_expert[:1] - 1, block_expert[:-1]])
    first = (active & (block_expert != prev)).astype(I32)
    run_slot = (jnp.sum(jnp.where(idx[None, :] <= idx[:, None], first[None, :], 0), axis=1) - 1) & 1
    later_first = (first[None, :] == 1) & (idx[None, :] > idx[:, None])
    next_block = jnp.min(jnp.where(later_first, idx[None, :], nb), axis=1)
    next_expert = jnp.where(next_block < nb, block_expert[jnp.minimum(next_block, nb - 1)], -1).astype(I32)

    def blk(i, be, fi, sl, nx, nact):
        return (jnp.minimum(i, (nact[0] + FFN_STEP_BLOCKS - 1) // FFN_STEP_BLOCKS - 1), 0)

    whole = lambda shape: pl.BlockSpec(shape, lambda i, *_: (0,) * len(shape))

    return pl.pallas_call(
        _ffn_body,
        grid_spec=pltpu.PrefetchScalarGridSpec(
            num_scalar_prefetch=5,
            grid=(nb // FFN_STEP_BLOCKS,),
            in_specs=[pl.BlockSpec(block, blk),
                      pl.BlockSpec(memory_space=pl.ANY),
                      whole((ne, 1, f2)),
                      pl.BlockSpec(memory_space=pl.ANY),
                      whole((ne, 1, d))],
            out_specs=pl.BlockSpec(block, blk),
            scratch_shapes=[pltpu.VMEM((2, d, f2), F32), pltpu.VMEM((2, f2 // 2, d), F32),
                            pltpu.VMEM((d, f2), BF16), pltpu.VMEM((f2 // 2, d), BF16),
                            pltpu.SemaphoreType.DMA((2, 2))]),
        out_shape=jax.ShapeDtypeStruct(xs.shape, F32),
        compiler_params=_cparams(("arbitrary",)),
        name="expert_ffn",
    )(block_expert, first, run_slot.astype(I32), next_expert, n_active, xs, w1, b1.reshape(ne, 1, f2), w2,
      b2.reshape(ne, 1, d))


def _combine_body(src_ref, dst_ref, ngrp_ref, tot_ref, h_ref, dloc_ref, gate_ref, nw_ref, ys_hbm, o_ref, stage,
                  sem):
    step = pl.program_id(0)
    tm = h_ref.shape[0]
    slot = step % 2

    def shares(at_step, at_slot, act):
        def per_expert(e, carry):
            idx = at_step * N_EXPERTS + e
            src, dst = src_ref[idx], dst_ref[idx]

            def piece(first, size):
                act(pltpu.make_async_copy(ys_hbm.at[_group_rows(dst + first, size)],
                                          stage.at[at_slot, _group_rows(src + first, size)], sem.at[at_slot]))

            _for_each_piece(ngrp_ref[idx], piece)
            return carry

        lax.fori_loop(0, N_EXPERTS, per_expert, 0)

    @pl.when(step == 0)
    def _():
        stage[...] = jnp.zeros(stage.shape, F32)
        shares(0, 0, lambda cp: cp.start())

    @pl.when(step + 1 < pl.num_programs(0))
    def _():
        shares(step + 1, 1 - slot, lambda cp: cp.start())

    _wait_groups(tot_ref[step], stage.at[slot], ys_hbm, sem.at[slot])
    dloc = dloc_ref[...]
    gates = gate_ref[...]
    moe = jnp.zeros(o_ref.shape, F32)
    for c in range(SLAB_ROWS // SLAB_CHUNK):
        cols = lax.broadcasted_iota(I32, (tm, SLAB_CHUNK), 1) + c * SLAB_CHUNK
        weight = jnp.zeros((tm, SLAB_CHUNK), F32)
        for k in range(TOP_K):
            weight = jnp.where(dloc[:, k:k + 1] == cols, gates[:, k:k + 1], weight)
        rows = stage[slot, c * SLAB_CHUNK:(c + 1) * SLAB_CHUNK, :].astype(BF16)
        moe = moe + jnp.dot(weight.astype(BF16), rows, preferred_element_type=F32)
    y = h_ref[...] + moe
    o_ref[...] = y * lax.rsqrt(jnp.mean(y * y, axis=-1, keepdims=True) + EPS) * nw_ref[...]


def _combine(shares, h, dloc_tk, gates_tk, norm_w, ys):
    n_tok, d = h.shape
    tm = ROUTE_TILE
    return pl.pallas_call(
        _combine_body,
        grid_spec=pltpu.PrefetchScalarGridSpec(
            num_scalar_prefetch=4,
            grid=(n_tok // tm,),
            in_specs=[pl.BlockSpec((tm, d), lambda i, *_: (i, 0)),
                      pl.BlockSpec((tm, TOP_K), lambda i, *_: (i, 0)),
                      pl.BlockSpec((tm, TOP_K), lambda i, *_: (i, 0)),
                      pl.BlockSpec((1, d), lambda i, *_: (0, 0)),
                      pl.BlockSpec(memory_space=pl.ANY)],
            out_specs=pl.BlockSpec((tm, d), lambda i, *_: (i, 0)),
            scratch_shapes=[pltpu.VMEM((2, SLAB_ROWS, d), F32), pltpu.SemaphoreType.DMA((2,))]),
        out_shape=jax.ShapeDtypeStruct((n_tok, d), F32),
        compiler_params=_cparams(("arbitrary",)),
        name="combine_norm",
    )(*shares, h, dloc_tk, gates_tk, norm_w.reshape(1, d), ys)


def kernel(x, norm1_w, w_in, lb_logits, hgrn_norm_w, dw_w, dw_b, conv_ln_w, conv_ln_b, w_out, norm2_w,
           router_w, router_b, w1, b1, w2, b2, final_norm_w):
    batch, t_len, d = x.shape
    assert w_in.shape[0] == 1, "single-layer block"
    n_tok = batch * t_len
    hk = HGRN_HEADS * HEAD_DIM
    conv_ch = dw_w.shape[2]
    lb_table = jnp.cumsum(jax.nn.softmax(lb_logits.astype(F32), axis=1), axis=1)
    x2 = x.reshape(n_tok, d)
    proj = _in_proj(x2, norm1_w[0], w_in[0].astype(BF16))
    a = _hgrn(proj, lb_table[0, 0], lb_table[1, 0], hgrn_norm_w[0], batch, t_len)
    b = _conv(proj, dw_w[0], dw_b[0], conv_ln_w[0], conv_ln_b[0], batch, t_len, (5 * hk) // conv_ch)
    h_mid, n2_bf, dloc, gate, cnt_tiles = _mix(a, b, x2, w_out[0].astype(BF16), norm2_w[0],
                                                     router_w[0], router_b[0])
    nt = n_tok // ROUTE_TILE
    e_ids = jnp.arange(N_EXPERTS, dtype=I32)
    t_ids = jnp.arange(nt, dtype=I32)
    share = (cnt_tiles[:, :, 0].astype(I32) + SUBLANES - 1) // SUBLANES * SUBLANES
    in_slab = jnp.sum(jnp.where(e_ids[None, None, :] < e_ids[None, :, None], share[:, None, :], 0), axis=2)
    in_expert = jnp.sum(jnp.where((t_ids[None, :] < t_ids[:, None])[:, :, None], share[None, :, :], 0), axis=1)
    padded = (jnp.sum(share, axis=0) + MOE_BLOCK - 1) // MOE_BLOCK * MOE_BLOCK
    pad_end = jnp.sum(jnp.where(e_ids[None, :] <= e_ids[:, None], padded[None, :], 0), axis=1)
    in_xs = (pad_end - padded)[None, :] + in_expert
    n_blocks = -(-(n_tok * TOP_K + nt * N_EXPERTS * (SUBLANES - 1)) // MOE_BLOCK) + N_EXPERTS
    n_blocks += -n_blocks % FFN_STEP_BLOCKS
    block_start = jnp.arange(n_blocks, dtype=I32) * MOE_BLOCK
    block_expert = jnp.minimum(jnp.sum((pad_end[None, :] <= block_start[:, None]).astype(I32), axis=1),
                               N_EXPERTS - 1)
    n_active = pad_end[-1:] // MOE_BLOCK
    shares = [(v // SUBLANES).reshape(-1) for v in (in_slab, in_xs, share)] + [jnp.sum(share, axis=1) // SUBLANES]
    xs = _dispatch(shares, n2_bf, dloc, n_blocks * MOE_BLOCK)
    ys = _ffn(block_expert, n_active, xs, w1[0], b1[0], w2[0], b2[0])
    out = _combine(shares, h_mid, dloc.T, gate.T, final_norm_w, ys)
    return out.reshape(batch, t_len, d)
```

```python
import functools
import math

import numpy as np
import jax
import jax.numpy as jnp
from jax import lax
from jax.experimental import pallas as pl
from jax.experimental.pallas import tpu as pltpu

F32 = jnp.float32
BF16 = jnp.bfloat16
I32 = jnp.int32

EPS = 1e-5
HGRN_HEADS = 4
HEAD_DIM = 128
HGRN_CHUNK = 256
CONV_LEN = 31
N_EXPERTS = 32
TOP_K = 4
SWIGLU_LIMIT = 7.0
SWIGLU_ALPHA = 1.702
MOE_BLOCK = 256
ROUTE_TILE = 512
LANES = 128
SUBLANES = 8
VMEM_LIMIT = 56 << 20

_NT = (((1,), (1,)), ((), ()))


def _sigmoid(x):
    return 0.5 * jnp.tanh(0.5 * x) + 0.5


def _silu(x):
    return x * _sigmoid(x)


def _cparams(sem):
    return pltpu.CompilerParams(dimension_semantics=sem, vmem_limit_bytes=VMEM_LIMIT)


def _inproj_body(x_ref, nw_ref, w_ref, o_ref):
    x = x_ref[...]
    n = x * lax.rsqrt(jnp.mean(x * x, axis=-1, keepdims=True) + EPS) * nw_ref[...]
    o_ref[...] = jnp.dot(n.astype(BF16), w_ref[...], preferred_element_type=F32)


def _in_proj(x2, norm_w, w_bf, tm=512):
    n_tok, d = x2.shape
    cols = w_bf.shape[1]
    return pl.pallas_call(
        _inproj_body,
        grid=(n_tok // tm,),
        in_specs=[pl.BlockSpec((tm, d), lambda i: (i, 0)),
                  pl.BlockSpec((1, d), lambda i: (0, 0)),
                  pl.BlockSpec((d, cols), lambda i: (0, 0))],
        out_specs=pl.BlockSpec((tm, cols), lambda i: (i, 0)),
        out_shape=jax.ShapeDtypeStruct((n_tok, cols), F32),
        compiler_params=_cparams(("parallel",)),
        name="in_proj",
    )(x2, norm_w.reshape(1, d), w_bf)


def _hgrn_levels(c):
    return [c >> (i + 1) for i in range(int(math.log2(c)))]


def _hgrn_sum_matrices(c, fwd):
    r = np.arange(c)
    m = _HGRN_SMALL_LEVEL
    small = np.zeros((c, c), np.float32)
    for t in range(c):
        p0 = t & ~(2 * m - 1)
        upper = (t & m) != 0
        if fwd:
            if upper:
                small[t, p0 + m:t + 1] = 1.0
            else:
                small[t, t + 1:p0 + m] = 1.0
        else:
            if upper:
                small[t, p0 + m:t] = 1.0
            else:
                small[t, t:p0 + m] = 1.0
    mats = [r[None, :] <= r[:, None] if fwd else r[None, :] >= r[:, None], small]
    return np.concatenate([np.asarray(x, np.float32) for x in mats], axis=0)


_HGRN_SMALL_LEVEL = 2
_HGRN_ROW_SLICE = 16


def _hgrn_chunk(q, k, lf, v_bf, vt_bf, st, sums_ref, fwd, c):
    dk = q.shape[1]
    hi = lf.astype(BF16)
    lo = (lf - hi.astype(F32)).astype(BF16)
    both = jnp.dot(sums_ref[...], jnp.concatenate([hi, lo], axis=1), preferred_element_type=F32)
    sums = both[:, :dk] + both[:, dk:]
    b = sums[0:c]
    edge_row = c - 1 if fwd else 0
    d_out = b[edge_row:edge_row + 1] - b
    row = lax.broadcasted_iota(I32, (c, dk), 0)
    rr = lax.broadcasted_iota(I32, (c, c), 0)
    cc = lax.broadcasted_iota(I32, (c, c), 1)
    sep = rr ^ cc
    a = None
    for m in _hgrn_levels(c):
        upper = (row & m) != 0
        is_q = upper if fwd else jnp.logical_not(upper)
        if m == 1:
            d = jnp.where(is_q, lf, 0.0)
        elif m == _HGRN_SMALL_LEVEL:
            d = sums[c:2 * c]
        else:
            blocks = c // (2 * m)
            at = m - 1 if fwd else m
            edge = b.reshape(blocks, 2 * m, dk)[:, at:at + 1, :]
            edge = jnp.broadcast_to(edge, (blocks, 2 * m, dk)).reshape(c, dk)
            d = jnp.where(is_q, b - edge, edge - b)
        r = (jnp.where(is_q, q, k) * jnp.exp2(d)).astype(BF16)
        if m < _HGRN_ROW_SLICE:
            a_l = lax.dot_general(r, r, _NT, preferred_element_type=F32)
            a = a_l if a is None else jnp.where(sep < 2 * m, a_l, a)
        else:
            halves = [(p0 + m, p0 + 2 * m) if fwd else (p0, p0 + m) for p0 in range(0, c, 2 * m)]
            a_q = lax.dot_general(jnp.concatenate([r[lo:hi] for lo, hi in halves], axis=0), r, _NT,
                                  preferred_element_type=F32)
            parts = []
            for n, (lo, hi) in enumerate(halves):
                new = a_q[n * m:(n + 1) * m]
                if a is None:
                    kept = jnp.zeros((m, c), F32)
                else:
                    new = jnp.where(sep[lo:hi] < 2 * m, new, a[lo:hi])
                    kept = a[lo - m:lo] if fwd else a[hi:hi + m]
                parts += [kept, new] if fwd else [new, kept]
            a = jnp.concatenate(parts, axis=0)
    a_diag = lax.dot_general(q.astype(BF16), k.astype(BF16), _NT, preferred_element_type=F32)
    a = jnp.where(sep < 1, a_diag, a)
    a = jnp.where(rr >= cc if fwd else rr <= cc, a, 0.0)
    e_b = jnp.exp2(b)
    q_in = (q * e_b).astype(BF16)
    k_out = (k * jnp.exp2(d_out)).astype(BF16)
    o = jnp.dot(a.astype(BF16), v_bf, preferred_element_type=F32)
    o = o + lax.dot_general(q_in, st.astype(BF16), _NT, preferred_element_type=F32)
    st_new = st * e_b[edge_row:edge_row + 1] + jnp.dot(vt_bf, k_out, preferred_element_type=F32)
    return o, st_new


def _hgrn_body(qz_ref, ff_ref, fb_ref, iv_ref, gz_ref, lbf_ref, lbb_ref, nw_ref, sf_ref, sb_ref, o_ref,
               q_s, kf_s, lff_s, kb_s, lfb_s, v_s, vt_s, acc_s, *, c):
    t_len, dk = qz_ref.shape
    nch = t_len // c
    q_s[...] = _silu(qz_ref[...])

    def gate(fz_ref, lb_ref, k_s, lf_s):
        lb = lb_ref[0]
        f = lb + (1.0 - lb) * _sigmoid(fz_ref[...])
        lf_s[...] = jnp.log2(f)
        k_s[...] = 1.0 - f

    gate(ff_ref, lbf_ref, kf_s, lff_s)
    gate(fb_ref, lbb_ref, kb_s, lfb_s)
    v = iv_ref[...]
    v_s[...] = v.astype(BF16)
    for ci in range(nch):
        vt_s[ci] = v[ci * c:(ci + 1) * c, :].T.astype(BF16)

    def run(k_s, lf_s, sums_ref, fwd):
        def step(i, st):
            ci = i if fwd else nch - 1 - i
            rows = pl.ds(pl.multiple_of(ci * c, c), c)
            o, st = _hgrn_chunk(q_s[rows, :], k_s[rows, :], lf_s[rows, :], v_s[rows, :], vt_s[ci],
                                st, sums_ref, fwd, c)
            if fwd:
                acc_s[rows, :] = o
            else:
                acc_s[rows, :] += o
            return st
        lax.fori_loop(0, nch, step, jnp.zeros((dk, dk), F32), unroll=True)

    run(kf_s, lff_s, sf_ref, True)
    run(kb_s, lfb_s, sb_ref, False)
    o = acc_s[...]
    o = o * lax.rsqrt(jnp.mean(o * o, axis=-1, keepdims=True) + EPS) * nw_ref[0]
    o_ref[...] = (o * _silu(gz_ref[...])).astype(o_ref.dtype)


def _hgrn(proj, lb_f, lb_b, norm_w, batch, t_len):
    h, dk, c = HGRN_HEADS, HEAD_DIM, HGRN_CHUNK
    nrows = 2
    sums_f = jnp.asarray(_hgrn_sum_matrices(c, True), BF16)
    sums_b = jnp.asarray(_hgrn_sum_matrices(c, False), BF16)

    def col(group):
        return pl.BlockSpec((t_len, dk), lambda b, hh: (b, group * h + hh))

    def per_head():
        return pl.BlockSpec((1, 1, dk), lambda b, hh: (hh, 0, 0))

    const = pl.BlockSpec((nrows * c, c), lambda b, hh: (0, 0))
    seq = lambda dt: pltpu.VMEM((t_len, dk), dt)
    return pl.pallas_call(
        functools.partial(_hgrn_body, c=c),
        grid=(batch, h),
        in_specs=[col(0), col(1), col(2), col(3), col(4), per_head(), per_head(), per_head(), const, const],
        out_specs=pl.BlockSpec((t_len, dk), lambda b, hh: (b, hh)),
        out_shape=jax.ShapeDtypeStruct((batch * t_len, h * dk), BF16),
        scratch_shapes=[seq(F32), seq(F32), seq(F32), seq(F32), seq(F32), seq(BF16),
                        pltpu.VMEM((t_len // c, dk, c), BF16), seq(F32)],
        compiler_params=_cparams(("parallel", "parallel")),
        name="hgrn",
    )(proj, proj, proj, proj, proj, lb_f.reshape(h, 1, dk), lb_b.reshape(h, 1, dk),
      norm_w.reshape(h, 1, dk), sums_f, sums_b)


CONV_ROWS = 64
CONV_HALO = 16


def _conv_body(cv_ref, cg_ref, w_ref, b_ref, lnw_ref, lnb_ref, o_ref, u_s, y_s):
    t_len, ch = cv_ref.shape
    halo, rows = CONV_HALO, CONV_ROWS
    shift0 = halo - (CONV_LEN - 1) // 2
    win = rows + 2 * halo
    for g in range(ch // LANES):
        lanes = slice(g * LANES, (g + 1) * LANES)
        u_s[g, 0:halo, :] = jnp.zeros((halo, LANES), F32)
        u_s[g, halo + t_len:, :] = jnp.zeros((halo, LANES), F32)
        u_s[g, halo:halo + t_len, :] = cv_ref[:, lanes] * _sigmoid(cg_ref[:, lanes])

    def conv_step(i, carry):
        t0 = pl.multiple_of(i * rows, rows)
        for g in range(ch // LANES):
            lanes = slice(g * LANES, (g + 1) * LANES)
            window = u_s.at[g, pl.ds(t0, win), :]
            acc = jnp.zeros((rows, LANES), F32)
            for j in range(CONV_LEN):
                off = j + shift0
                acc = acc + w_ref[j:j + 1, lanes] * window[off:off + rows, :]
            y_s[pl.ds(t0, rows), lanes] = acc + b_ref[:, lanes]
        return carry

    lax.fori_loop(0, t_len // rows, conv_step, 0)

    def step(i, carry):
        t0 = pl.multiple_of(i * rows, rows)
        y = y_s[pl.ds(t0, rows), :]
        mu = jnp.mean(y, axis=-1, keepdims=True)
        yc = y - mu
        var = jnp.mean(yc * yc, axis=-1, keepdims=True)
        z = yc * lax.rsqrt(var + EPS) * lnw_ref[...] + lnb_ref[...]
        o_ref[pl.ds(t0, rows), :] = _silu(z).astype(o_ref.dtype)
        return carry

    lax.fori_loop(0, t_len // rows, step, 0, unroll=4)


def _conv(proj, dw_w, dw_b, ln_w, ln_b, batch, t_len, first_col_block):
    ch = dw_w.shape[1]
    vec = pl.BlockSpec((1, ch), lambda b: (0, 0))
    return pl.pallas_call(
        _conv_body,
        grid=(batch,),
        in_specs=[pl.BlockSpec((t_len, ch), lambda b: (b, first_col_block)),
                  pl.BlockSpec((t_len, ch), lambda b: (b, first_col_block + 1)),
                  pl.BlockSpec((CONV_LEN, ch), lambda b: (0, 0)), vec, vec, vec],
        out_specs=pl.BlockSpec((t_len, ch), lambda b: (b, 0)),
        out_shape=jax.ShapeDtypeStruct((batch * t_len, ch), BF16),
        scratch_shapes=[pltpu.VMEM((ch // LANES, t_len + 2 * CONV_HALO, LANES), F32),
                        pltpu.VMEM((t_len, ch), F32)],
        compiler_params=_cparams(("parallel",)),
        name="conv",
    )(proj, proj, dw_w, dw_b.reshape(1, ch), ln_w.reshape(1, ch), ln_b.reshape(1, ch))


def _mix_body(a_ref, b_ref, x_ref, wa_ref, wb_ref, nw_ref, wr_ref, rb_ref, tri_ref, elow_ref,
              h_ref, n2_ref, dloc_ref, gate_ref, cnt_ref):
    tm = x_ref.shape[0]
    ne = wr_ref.shape[0] // 2

    h = x_ref[...] + jnp.dot(a_ref[...], wa_ref[...], preferred_element_type=F32) \
        + jnp.dot(b_ref[...], wb_ref[...], preferred_element_type=F32)
    h_ref[...] = h
    n2 = h * lax.rsqrt(jnp.mean(h * h, axis=-1, keepdims=True) + EPS) * nw_ref[...]
    hi = n2.astype(BF16)
    n2_ref[...] = hi
    lo = (n2 - hi.astype(F32)).astype(BF16)
    by_hi = lax.dot_general(wr_ref[...], hi, _NT, preferred_element_type=F32)
    by_lo = lax.dot_general(wr_ref[0:ne, :], lo, _NT, preferred_element_type=F32)
    logits = by_hi[0:ne] + by_hi[ne:] + by_lo + rb_ref[...]
    e_iota = lax.broadcasted_iota(I32, (ne, tm), 0)
    work = logits
    sels, vals = [], []
    for _ in range(TOP_K):
        mx = jnp.max(work, axis=0, keepdims=True)
        idx = jnp.min(jnp.where(work == mx, e_iota, ne), axis=0, keepdims=True)
        sel = e_iota == idx
        work = jnp.where(sel, -jnp.inf, work)
        sels.append(sel)
        vals.append(mx)
    exps = [jnp.exp(v - vals[0]) for v in vals]
    denom = exps[0] + exps[1] + exps[2] + exps[3]
    chosen = jnp.zeros((ne, tm), F32)
    for sel in sels:
        chosen = jnp.where(sel, 1.0, chosen)
    chosen_bf = chosen.astype(BF16)
    prior = jnp.dot(chosen_bf, tri_ref[...], preferred_element_type=F32)
    cnt = jnp.dot(chosen_bf, jnp.ones((tm, LANES), BF16), preferred_element_type=F32)
    cnt_ref[0] = cnt
    share = jnp.floor((cnt + (SUBLANES - 1)) * (1.0 / SUBLANES)) * SUBLANES
    lower = jnp.dot(elow_ref[...], share.astype(BF16), preferred_element_type=F32)
    slab_row = prior + jnp.concatenate([lower] * (tm // LANES), axis=1)
    for k in range(TOP_K):
        dloc_ref[k:k + 1, :] = jnp.sum(jnp.where(sels[k], slab_row, 0.0), axis=0, keepdims=True).astype(I32)
        gate_ref[k:k + 1, :] = exps[k] / denom


def _mix(a, b, x2, w_out_bf, norm_w, router_w, router_b, tm=ROUTE_TILE):
    n_tok, d = x2.shape
    wa, wb = w_out_bf[:a.shape[1]], w_out_bf[a.shape[1]:]
    ne = router_w.shape[1]
    wr_t = router_w.T
    wr_hi = wr_t.astype(BF16)
    wr_lo = (wr_t - wr_hi.astype(F32)).astype(BF16)
    tri = jnp.asarray(np.triu(np.ones((tm, tm), np.float32), k=1), BF16)
    e_lower = jnp.asarray(np.tril(np.ones((ne, ne), np.float32), k=-1), BF16)
    const = lambda shape: pl.BlockSpec(shape, lambda i: tuple(0 for _ in shape))
    return pl.pallas_call(
        _mix_body,
        grid=(n_tok // tm,),
        in_specs=[pl.BlockSpec((tm, a.shape[1]), lambda i: (i, 0)),
                  pl.BlockSpec((tm, b.shape[1]), lambda i: (i, 0)),
                  pl.BlockSpec((tm, d), lambda i: (i, 0)),
                  const(wa.shape), const(wb.shape), const((1, d)),
                  const((2 * ne, d)), const((ne, 1)), const((tm, tm)), const((ne, ne))],
        out_specs=[pl.BlockSpec((tm, d), lambda i: (i, 0)),
                   pl.BlockSpec((tm, d), lambda i: (i, 0)),
                   pl.BlockSpec((TOP_K, tm), lambda i: (0, i)),
                   pl.BlockSpec((TOP_K, tm), lambda i: (0, i)),
                   pl.BlockSpec((1, ne, LANES), lambda i: (i, 0, 0))],
        out_shape=[jax.ShapeDtypeStruct((n_tok, d), F32),
                   jax.ShapeDtypeStruct((n_tok, d), BF16),
                   jax.ShapeDtypeStruct((TOP_K, n_tok), I32),
                   jax.ShapeDtypeStruct((TOP_K, n_tok), F32),
                   jax.ShapeDtypeStruct((n_tok // tm, ne, LANES), F32)],
        compiler_params=_cparams(("parallel",)),
        name="mix_router",
    )(a, b, x2, wa, wb, norm_w.reshape(1, d), jnp.concatenate([wr_hi, wr_lo], axis=0), router_b.reshape(ne, 1),
      tri, e_lower)


def _powers_of_two_upto(n):
    return tuple(1 << i for i in reversed(range(n.bit_length())))


SLAB_ROWS = TOP_K * ROUTE_TILE + N_EXPERTS * SUBLANES
SLAB_CHUNK = 256
SLAB_PIECES = _powers_of_two_upto(ROUTE_TILE // SUBLANES)
SLAB_COMMON = ROUTE_TILE * TOP_K // N_EXPERTS // SUBLANES


def _for_each_piece(ngroups, fn):
    def pieces(sizes):
        for size in sizes:
            first = lax.bitwise_and(ngroups, ~(2 * size - 1))

            @pl.when(lax.bitwise_and(ngroups, size) != 0)
            def _():
                fn(first, size)

    rare = tuple(s for s in SLAB_PIECES if s >= 2 * SLAB_COMMON)

    @pl.when(ngroups >= min(rare))
    def _():
        pieces(rare)

    pieces(tuple(s for s in SLAB_PIECES if s < 2 * SLAB_COMMON))


def _group_rows(first_group, groups):
    return pl.ds(pl.multiple_of(first_group * SUBLANES, SUBLANES), groups * SUBLANES)


def _wait_groups(ngroups, stage_slot, hbm, sem):
    for size in _powers_of_two_upto(SLAB_ROWS // SUBLANES):
        @pl.when(lax.bitwise_and(ngroups, size) != 0)
        def _():
            pltpu.make_async_copy(hbm.at[_group_rows(0, size)], stage_slot.at[_group_rows(0, size)], sem).wait()


def _dispatch_body(src_ref, dst_ref, ngrp_ref, tot_ref, n2_ref, dloc_ref, xs_hbm, stage, sem):
    step = pl.program_id(0)
    tm = n2_ref.shape[0]
    slot = step % 2

    def shares(at_step, at_slot, act):
        def per_expert(e, carry):
            idx = at_step * N_EXPERTS + e
            src, dst = src_ref[idx], dst_ref[idx]

            def piece(first, size):
                act(pltpu.make_async_copy(stage.at[at_slot, _group_rows(src + first, size)],
                                          xs_hbm.at[_group_rows(dst + first, size)], sem.at[at_slot]))

            _for_each_piece(ngrp_ref[idx], piece)
            return carry

        lax.fori_loop(0, N_EXPERTS, per_expert, 0)

    def wait_slab(at_step, at_slot):
        _wait_groups(tot_ref[at_step], stage.at[at_slot], xs_hbm, sem.at[at_slot])

    @pl.when(step >= 2)
    def _():
        wait_slab(step - 2, slot)

    n2 = n2_ref[...]
    dloc = dloc_ref[...]
    for c in range(SLAB_ROWS // SLAB_CHUNK):
        rows = lax.broadcasted_iota(I32, (SLAB_CHUNK, tm), 0) + c * SLAB_CHUNK
        pick = jnp.zeros((SLAB_CHUNK, tm), F32)
        for k in range(TOP_K):
            pick = jnp.where(dloc[k:k + 1, :] == rows, 1.0, pick)
        stage[slot, c * SLAB_CHUNK:(c + 1) * SLAB_CHUNK, :] = jnp.dot(pick.astype(BF16), n2,
                                                                       preferred_element_type=F32)
    shares(step, slot, lambda cp: cp.start())

    @pl.when(step == pl.num_programs(0) - 1)
    def _():
        wait_slab(step, slot)

        @pl.when(step >= 1)
        def _():
            wait_slab(step - 1, 1 - slot)


def _dispatch(shares, n2_bf, dloc, n_rows):
    n_tok, d = n2_bf.shape
    tm = ROUTE_TILE
    return pl.pallas_call(
        _dispatch_body,
        grid_spec=pltpu.PrefetchScalarGridSpec(
            num_scalar_prefetch=4,
            grid=(n_tok // tm,),
            in_specs=[pl.BlockSpec((tm, d), lambda i, *_: (i, 0)),
                      pl.BlockSpec((TOP_K, tm), lambda i, *_: (0, i))],
            out_specs=pl.BlockSpec(memory_space=pl.ANY),
            scratch_shapes=[pltpu.VMEM((2, SLAB_ROWS, d), F32), pltpu.SemaphoreType.DMA((2,))]),
        out_shape=jax.ShapeDtypeStruct((n_rows, d), F32),
        compiler_params=_cparams(("arbitrary",)),
        name="dispatch",
    )(*shares, n2_bf, dloc)


FFN_CAST_ROWS = 256
FFN_STEP_BLOCKS = 2


def _ffn_body(be_ref, first_ref, slot_ref, next_ref, nact_ref, xs_ref, w1_hbm, b1_ref, w2_hbm, b2_ref, ys_ref,
              w1_f, w2_f, w1_b, w2_b, sem):
    first_block = pl.program_id(0) * FFN_STEP_BLOCKS
    n_active = nact_ref[0]

    def fetch(expert, slot):
        return (pltpu.make_async_copy(w1_hbm.at[expert], w1_f.at[slot], sem.at[0, slot]),
                pltpu.make_async_copy(w2_hbm.at[expert], w2_f.at[slot], sem.at[1, slot]))

    def start_run(blk):
        slot = slot_ref[blk]

        @pl.when(blk == 0)
        def _():
            for cp in fetch(be_ref[0], slot):
                cp.start()

        for cp in fetch(be_ref[blk], slot):
            cp.wait()

        @pl.when(next_ref[blk] >= 0)
        def _():
            for cp in fetch(next_ref[blk], 1 - slot):
                cp.start()

        def cast(src, dst):
            def body(c, carry):
                rows = pl.ds(pl.multiple_of(c * FFN_CAST_ROWS, FFN_CAST_ROWS), FFN_CAST_ROWS)
                dst[rows, :] = src[slot, rows, :].astype(BF16)
                return carry
            lax.fori_loop(0, dst.shape[0] // FFN_CAST_ROWS, body, 0)

        cast(w1_f, w1_b)
        cast(w2_f, w2_b)

    def compute(blk, lo, hi):
        expert = be_ref[blk]
        x = xs_ref[lo:hi, :].astype(BF16)
        hdn = jnp.dot(x, w1_b[...], preferred_element_type=F32) + b1_ref[expert]
        d_ff = hdn.shape[1] // 2
        glu = jnp.minimum(hdn[:, :d_ff], SWIGLU_LIMIT)
        lin = jnp.clip(hdn[:, d_ff:], -SWIGLU_LIMIT, SWIGLU_LIMIT)
        act = glu * _sigmoid(SWIGLU_ALPHA * glu) * (lin + 1.0)
        ys_ref[lo:hi, :] = jnp.dot(act.astype(BF16), w2_b[...], preferred_element_type=F32) + b2_ref[expert]

    second = first_block + 1

    @pl.when(first_block < n_active)
    def _():
        @pl.when(first_ref[first_block] == 1)
        def _():
            start_run(first_block)

        same_run = jnp.logical_and(second < n_active, first_ref[second] == 0)

        @pl.when(same_run)
        def _():
            compute(first_block, 0, 2 * MOE_BLOCK)

        @pl.when(jnp.logical_not(same_run))
        def _():
            compute(first_block, 0, MOE_BLOCK)

            @pl.when(second < n_active)
            def _():
                start_run(second)
                compute(second, MOE_BLOCK, 2 * MOE_BLOCK)


def _ffn(block_expert, n_active, xs, w1, b1, w2, b2):
    ne, d, f2 = w1.shape
    nb = xs.shape[0] // MOE_BLOCK
    assert nb % FFN_STEP_BLOCKS == 0
    block = (FFN_STEP_BLOCKS * MOE_BLOCK, d)
    idx = jnp.arange(nb, dtype=I32)
    active = idx < n_active[0]
    prev = jnp.concatenate([block_expert[:1] - 1, block_expert[:-1]])
    first = (active & (block_expert != prev)).astype(I32)
    run_slot = (jnp.sum(jnp.where(idx[None, :] <= idx[:, None], first[None, :], 0), axis=1) - 1) & 1
    later_first = (first[None, :] == 1) & (idx[None, :] > idx[:, None])
    next_block = jnp.min(jnp.where(later_first, idx[None, :], nb), axis=1)
    next_expert = jnp.where(next_block < nb, block_expert[jnp.minimum(next_block, nb - 1)], -1).astype(I32)

    def blk(i, be, fi, sl, nx, nact):
        return (jnp.minimum(i, (nact[0] + FFN_STEP_BLOCKS - 1) // FFN_STEP_BLOCKS - 1), 0)

    whole = lambda shape: pl.BlockSpec(shape, lambda i, *_: (0,) * len(shape))

    return pl.pallas_call(
        _ffn_body,
        grid_spec=pltpu.PrefetchScalarGridSpec(
            num_scalar_prefetch=5,
            grid=(nb // FFN_STEP_BLOCKS,),
            in_specs=[pl.BlockSpec(block, blk),
                      pl.BlockSpec(memory_space=pl.ANY),
                      whole((ne, 1, f2)),
                      pl.BlockSpec(memory_space=pl.ANY),
                      whole((ne, 1, d))],
            out_specs=pl.BlockSpec(block, blk),
            scratch_shapes=[pltpu.VMEM((2, d, f2), F32), pltpu.VMEM((2, f2 // 2, d), F32),
                            pltpu.VMEM((d, f2), BF16), pltpu.VMEM((f2 // 2, d), BF16),
                            pltpu.SemaphoreType.DMA((2, 2))]),
        out_shape=jax.ShapeDtypeStruct(xs.shape, F32),
        compiler_params=_cparams(("arbitrary",)),
        name="expert_ffn",
    )(block_expert, first, run_slot.astype(I32), next_expert, n_active, xs, w1, b1.reshape(ne, 1, f2), w2,
      b2.reshape(ne, 1, d))


def _combine_body(src_ref, dst_ref, ngrp_ref, tot_ref, h_ref, dloc_ref, gate_ref, nw_ref, ys_hbm, o_ref, stage,
                  sem):
    step = pl.program_id(0)
    tm = h_ref.shape[0]
    slot = step % 2

    def shares(at_step, at_slot, act):
        def per_expert(e, carry):
            idx = at_step * N_EXPERTS + e
            src, dst = src_ref[idx], dst_ref[idx]

            def piece(first, size):
                act(pltpu.make_async_copy(ys_hbm.at[_group_rows(dst + first, size)],
                                          stage.at[at_slot, _group_rows(src + first, size)], sem.at[at_slot]))

            _for_each_piece(ngrp_ref[idx], piece)
            return carry

        lax.fori_loop(0, N_EXPERTS, per_expert, 0)

    @pl.when(step == 0)
    def _():
        stage[...] = jnp.zeros(stage.shape, F32)
        shares(0, 0, lambda cp: cp.start())

    @pl.when(step + 1 < pl.num_programs(0))
    def _():
        shares(step + 1, 1 - slot, lambda cp: cp.start())

    _wait_groups(tot_ref[step], stage.at[slot], ys_hbm, sem.at[slot])
    dloc = dloc_ref[...]
    gates = gate_ref[...]
    moe = jnp.zeros(o_ref.shape, F32)
    for c in range(SLAB_ROWS // SLAB_CHUNK):
        srow = lax.broadcasted_iota(I32, (SLAB_CHUNK, tm), 0) + c * SLAB_CHUNK
        weight_t = jnp.zeros((SLAB_CHUNK, tm), F32)
        for k in range(TOP_K):
            weight_t = jnp.where(dloc[k:k + 1, :] == srow, gates[k:k + 1, :], weight_t)
        rows = stage[slot, c * SLAB_CHUNK:(c + 1) * SLAB_CHUNK, :].astype(BF16)
        moe = moe + lax.dot_general(weight_t.astype(BF16), rows, (((0,), (0,)), ((), ())),
                                    preferred_element_type=F32)
    y = h_ref[...] + moe
    o_ref[...] = y * lax.rsqrt(jnp.mean(y * y, axis=-1, keepdims=True) + EPS) * nw_ref[...]


def _combine(shares, h, dloc_tk, gates_tk, norm_w, ys):
    n_tok, d = h.shape
    tm = ROUTE_TILE
    return pl.pallas_call(
        _combine_body,
        grid_spec=pltpu.PrefetchScalarGridSpec(
            num_scalar_prefetch=4,
            grid=(n_tok // tm,),
            in_specs=[pl.BlockSpec((tm, d), lambda i, *_: (i, 0)),
                      pl.BlockSpec((TOP_K, tm), lambda i, *_: (0, i)),
                      pl.BlockSpec((TOP_K, tm), lambda i, *_: (0, i)),
                      pl.BlockSpec((1, d), lambda i, *_: (0, 0)),
                      pl.BlockSpec(memory_space=pl.ANY)],
            out_specs=pl.BlockSpec((tm, d), lambda i, *_: (i, 0)),
            scratch_shapes=[pltpu.VMEM((2, SLAB_ROWS, d), F32), pltpu.SemaphoreType.DMA((2,))]),
        out_shape=jax.ShapeDtypeStruct((n_tok, d), F32),
        compiler_params=_cparams(("arbitrary",)),
        name="combine_norm",
    )(*shares, h, dloc_tk, gates_tk, norm_w.reshape(1, d), ys)


def kernel(x, norm1_w, w_in, lb_logits, hgrn_norm_w, dw_w, dw_b, conv_ln_w, conv_ln_b, w_out, norm2_w,
           router_w, router_b, w1, b1, w2, b2, final_norm_w):
    batch, t_len, d = x.shape
    assert w_in.shape[0] == 1, "single-layer block"
    n_tok = batch * t_len
    hk = HGRN_HEADS * HEAD_DIM
    conv_ch = dw_w.shape[2]
    lb_table = jnp.cumsum(jax.nn.softmax(lb_logits.astype(F32), axis=1), axis=1)
    x2 = x.reshape(n_tok, d)
    proj = _in_proj(x2, norm1_w[0], w_in[0].astype(BF16))
    a = _hgrn(proj, lb_table[0, 0], lb_table[1, 0], hgrn_norm_w[0], batch, t_len)
    b = _conv(proj, dw_w[0], dw_b[0], conv_ln_w[0], conv_ln_b[0], batch, t_len, (5 * hk) // conv_ch)
    h_mid, n2_bf, dloc, gate, cnt_tiles = _mix(a, b, x2, w_out[0].astype(BF16), norm2_w[0],
                                                     router_w[0], router_b[0])
    nt = n_tok // ROUTE_TILE
    e_ids = jnp.arange(N_EXPERTS, dtype=I32)
    t_ids = jnp.arange(nt, dtype=I32)
    share = (cnt_tiles[:, :, 0].astype(I32) + SUBLANES - 1) // SUBLANES * SUBLANES
    in_slab = jnp.sum(jnp.where(e_ids[None, None, :] < e_ids[None, :, None], share[:, None, :], 0), axis=2)
    in_expert = jnp.sum(jnp.where((t_ids[None, :] < t_ids[:, None])[:, :, None], share[None, :, :], 0), axis=1)
    padded = (jnp.sum(share, axis=0) + MOE_BLOCK - 1) // MOE_BLOCK * MOE_BLOCK
    pad_end = jnp.sum(jnp.where(e_ids[None, :] <= e_ids[:, None], padded[None, :], 0), axis=1)
    in_xs = (pad_end - padded)[None, :] + in_expert
    n_blocks = -(-(n_tok * TOP_K + nt * N_EXPERTS * (SUBLANES - 1)) // MOE_BLOCK) + N_EXPERTS
    n_blocks += -n_blocks % FFN_STEP_BLOCKS
    block_start = jnp.arange(n_blocks, dtype=I32) * MOE_BLOCK
    block_expert = jnp.minimum(jnp.sum((pad_end[None, :] <= block_start[:, None]).astype(I32), axis=1),
                               N_EXPERTS - 1)
    n_active = pad_end[-1:] // MOE_BLOCK
    shares = [(v // SUBLANES).reshape(-1) for v in (in_slab, in_xs, share)] + [jnp.sum(share, axis=1) // SUBLANES]
    xs = _dispatch(shares, n2_bf, dloc, n_blocks * MOE_BLOCK)
    ys = _ffn(block_expert, n_active, xs, w1[0], b1[0], w2[0], b2[0])
    out = _combine(shares, h_mid, dloc, gate, final_norm_w, ys)
    return out.reshape(batch, t_len, d)
```

```python
import functools
import math

import numpy as np
import jax
import jax.numpy as jnp
from jax import lax
from jax.experimental import pallas as pl
from jax.experimental.pallas import tpu as pltpu

F32 = jnp.float32
BF16 = jnp.bfloat16
I32 = jnp.int32

EPS = 1e-5
HGRN_HEADS = 4
HEAD_DIM = 128
HGRN_CHUNK = 256
CONV_LEN = 31
N_EXPERTS = 32
TOP_K = 4
SWIGLU_LIMIT = 7.0
SWIGLU_ALPHA = 1.702
MOE_BLOCK = 256
ROUTE_TILE = 512
LANES = 128
SUBLANES = 8
VMEM_LIMIT = 56 << 20

_NT = (((1,), (1,)), ((), ()))


def _sigmoid(x):
    return 0.5 * jnp.tanh(0.5 * x) + 0.5


def _silu(x):
    return x * _sigmoid(x)


def _cparams(sem):
    return pltpu.CompilerParams(dimension_semantics=sem, vmem_limit_bytes=VMEM_LIMIT)


def _inproj_body(x_ref, nw_ref, w_ref, o_ref):
    x = x_ref[...]
    n = x * lax.rsqrt(jnp.mean(x * x, axis=-1, keepdims=True) + EPS) * nw_ref[...]
    o_ref[...] = jnp.dot(n.astype(BF16), w_ref[...], preferred_element_type=F32)


def _in_proj(x2, norm_w, w_bf, tm=512):
    n_tok, d = x2.shape
    cols = w_bf.shape[1]
    return pl.pallas_call(
        _inproj_body,
        grid=(n_tok // tm,),
        in_specs=[pl.BlockSpec((tm, d), lambda i: (i, 0)),
                  pl.BlockSpec((1, d), lambda i: (0, 0)),
                  pl.BlockSpec((d, cols), lambda i: (0, 0))],
        out_specs=pl.BlockSpec((tm, cols), lambda i: (i, 0)),
        out_shape=jax.ShapeDtypeStruct((n_tok, cols), F32),
        compiler_params=_cparams(("parallel",)),
        name="in_proj",
    )(x2, norm_w.reshape(1, d), w_bf)


def _hgrn_levels(c):
    return [c >> (i + 1) for i in range(int(math.log2(c)))]


def _hgrn_sum_matrices(c, fwd):
    r = np.arange(c)
    m = _HGRN_SMALL_LEVEL
    small = np.zeros((c, c), np.float32)
    for t in range(c):
        p0 = t & ~(2 * m - 1)
        upper = (t & m) != 0
        if fwd:
            if upper:
                small[t, p0 + m:t + 1] = 1.0
            else:
                small[t, t + 1:p0 + m] = 1.0
        else:
            if upper:
                small[t, p0 + m:t] = 1.0
            else:
                small[t, t:p0 + m] = 1.0
    mats = [r[None, :] <= r[:, None] if fwd else r[None, :] >= r[:, None], small]
    return np.concatenate([np.asarray(x, np.float32) for x in mats], axis=0)


_HGRN_SMALL_LEVEL = 2
_HGRN_ROW_SLICE = 16


def _hgrn_chunk(q, k, lf, v_bf, vt_bf, st, sums_ref, fwd, c):
    dk = q.shape[1]
    hi = lf.astype(BF16)
    lo = (lf - hi.astype(F32)).astype(BF16)
    both = jnp.dot(sums_ref[...], jnp.concatenate([hi, lo], axis=1), preferred_element_type=F32)
    sums = both[:, :dk] + both[:, dk:]
    b = sums[0:c]
    edge_row = c - 1 if fwd else 0
    d_out = b[edge_row:edge_row + 1] - b
    row = lax.broadcasted_iota(I32, (c, dk), 0)
    rr = lax.broadcasted_iota(I32, (c, c), 0)
    cc = lax.broadcasted_iota(I32, (c, c), 1)
    sep = rr ^ cc
    a = None
    for m in _hgrn_levels(c):
        upper = (row & m) != 0
        is_q = upper if fwd else jnp.logical_not(upper)
        if m == 1:
            d = jnp.where(is_q, lf, 0.0)
        elif m == _HGRN_SMALL_LEVEL:
            d = sums[c:2 * c]
        else:
            blocks = c // (2 * m)
            at = m - 1 if fwd else m
            edge = b.reshape(blocks, 2 * m, dk)[:, at:at + 1, :]
            edge = jnp.broadcast_to(edge, (blocks, 2 * m, dk)).reshape(c, dk)
            d = jnp.where(is_q, b - edge, edge - b)
        r = (jnp.where(is_q, q, k) * jnp.exp2(d)).astype(BF16)
        if m < _HGRN_ROW_SLICE:
            a_l = lax.dot_general(r, r, _NT, preferred_element_type=F32)
            a = a_l if a is None else jnp.where(sep < 2 * m, a_l, a)
        else:
            halves = [(p0 + m, p0 + 2 * m) if fwd else (p0, p0 + m) for p0 in range(0, c, 2 * m)]
            a_q = lax.dot_general(jnp.concatenate([r[lo:hi] for lo, hi in halves], axis=0), r, _NT,
                                  preferred_element_type=F32)
            parts = []
            for n, (lo, hi) in enumerate(halves):
                new = a_q[n * m:(n + 1) * m]
                if a is None:
                    kept = jnp.zeros((m, c), F32)
                else:
                    new = jnp.where(sep[lo:hi] < 2 * m, new, a[lo:hi])
                    kept = a[lo - m:lo] if fwd else a[hi:hi + m]
                parts += [kept, new] if fwd else [new, kept]
            a = jnp.concatenate(parts, axis=0)
    a_diag = lax.dot_general(q.astype(BF16), k.astype(BF16), _NT, preferred_element_type=F32)
    a = jnp.where(sep < 1, a_diag, a)
    a = jnp.where(rr >= cc if fwd else rr <= cc, a, 0.0)
    e_b = jnp.exp2(b)
    q_in = (q * e_b).astype(BF16)
    k_out = (k * jnp.exp2(d_out)).astype(BF16)
    o = jnp.dot(a.astype(BF16), v_bf, preferred_element_type=F32)
    o = o + lax.dot_general(q_in, st.astype(BF16), _NT, preferred_element_type=F32)
    st_new = st * e_b[edge_row:edge_row + 1] + jnp.dot(vt_bf, k_out, preferred_element_type=F32)
    return o, st_new


def _hgrn_body(qz_ref, ff_ref, fb_ref, iv_ref, gz_ref, lbf_ref, lbb_ref, nw_ref, sf_ref, sb_ref, o_ref,
               q_s, kf_s, lff_s, kb_s, lfb_s, v_s, vt_s, acc_s, *, c):
    t_len, dk = qz_ref.shape
    nch = t_len // c
    q_s[...] = _silu(qz_ref[...])

    def gate(fz_ref, lb_ref, k_s, lf_s):
        lb = lb_ref[0]
        f = lb + (1.0 - lb) * _sigmoid(fz_ref[...])
        lf_s[...] = jnp.log2(f)
        k_s[...] = 1.0 - f

    gate(ff_ref, lbf_ref, kf_s, lff_s)
    gate(fb_ref, lbb_ref, kb_s, lfb_s)
    v = iv_ref[...]
    v_s[...] = v.astype(BF16)
    for ci in range(nch):
        vt_s[ci] = v[ci * c:(ci + 1) * c, :].T.astype(BF16)

    def run(k_s, lf_s, sums_ref, fwd):
        def step(i, st):
            ci = i if fwd else nch - 1 - i
            rows = pl.ds(pl.multiple_of(ci * c, c), c)
            o, st = _hgrn_chunk(q_s[rows, :], k_s[rows, :], lf_s[rows, :], v_s[rows, :], vt_s[ci],
                                st, sums_ref, fwd, c)
            if fwd:
                acc_s[rows, :] = o
            else:
                acc_s[rows, :] += o
            return st
        lax.fori_loop(0, nch, step, jnp.zeros((dk, dk), F32), unroll=True)

    run(kf_s, lff_s, sf_ref, True)
    run(kb_s, lfb_s, sb_ref, False)
    o = acc_s[...]
    o = o * lax.rsqrt(jnp.mean(o * o, axis=-1, keepdims=True) + EPS) * nw_ref[0]
    o_ref[...] = (o * _silu(gz_ref[...])).astype(o_ref.dtype)


def _hgrn(proj, lb_f, lb_b, norm_w, batch, t_len):
    h, dk, c = HGRN_HEADS, HEAD_DIM, HGRN_CHUNK
    nrows = 2
    sums_f = jnp.asarray(_hgrn_sum_matrices(c, True), BF16)
    sums_b = jnp.asarray(_hgrn_sum_matrices(c, False), BF16)

    def col(group):
        return pl.BlockSpec((t_len, dk), lambda b, hh: (b, group * h + hh))

    def per_head():
        return pl.BlockSpec((1, 1, dk), lambda b, hh: (hh, 0, 0))

    const = pl.BlockSpec((nrows * c, c), lambda b, hh: (0, 0))
    seq = lambda dt: pltpu.VMEM((t_len, dk), dt)
    return pl.pallas_call(
        functools.partial(_hgrn_body, c=c),
        grid=(batch, h),
        in_specs=[col(0), col(1), col(2), col(3), col(4), per_head(), per_head(), per_head(), const, const],
        out_specs=pl.BlockSpec((t_len, dk), lambda b, hh: (b, hh)),
        out_shape=jax.ShapeDtypeStruct((batch * t_len, h * dk), BF16),
        scratch_shapes=[seq(F32), seq(F32), seq(F32), seq(F32), seq(F32), seq(BF16),
                        pltpu.VMEM((t_len // c, dk, c), BF16), seq(F32)],
        compiler_params=_cparams(("parallel", "parallel")),
        name="hgrn",
    )(proj, proj, proj, proj, proj, lb_f.reshape(h, 1, dk), lb_b.reshape(h, 1, dk),
      norm_w.reshape(h, 1, dk), sums_f, sums_b)


CONV_ROWS = 64
CONV_HALO = 16


def _conv_body(cv_ref, cg_ref, w_ref, b_ref, lnw_ref, lnb_ref, o_ref, u_s, y_s):
    t_len, ch = cv_ref.shape
    halo, rows = CONV_HALO, CONV_ROWS
    shift0 = halo - (CONV_LEN - 1) // 2
    win = rows + 2 * halo
    for g in range(ch // LANES):
        lanes = slice(g * LANES, (g + 1) * LANES)
        u_s[g, 0:halo, :] = jnp.zeros((halo, LANES), F32)
        u_s[g, halo + t_len:, :] = jnp.zeros((halo, LANES), F32)
        u_s[g, halo:halo + t_len, :] = cv_ref[:, lanes] * _sigmoid(cg_ref[:, lanes])

    def conv_step(i, carry):
        t0 = pl.multiple_of(i * rows, rows)
        for g in range(ch // LANES):
            lanes = slice(g * LANES, (g + 1) * LANES)
            window = u_s.at[g, pl.ds(t0, win), :]
            acc = jnp.zeros((rows, LANES), F32)
            for j in range(CONV_LEN):
                off = j + shift0
                acc = acc + w_ref[j:j + 1, lanes] * window[off:off + rows, :]
            y_s[pl.ds(t0, rows), lanes] = acc + b_ref[:, lanes]
        return carry

    lax.fori_loop(0, t_len // rows, conv_step, 0)

    def step(i, carry):
        t0 = pl.multiple_of(i * rows, rows)
        y = y_s[pl.ds(t0, rows), :]
        mu = jnp.mean(y, axis=-1, keepdims=True)
        yc = y - mu
        var = jnp.mean(yc * yc, axis=-1, keepdims=True)
        z = yc * lax.rsqrt(var + EPS) * lnw_ref[...] + lnb_ref[...]
        o_ref[pl.ds(t0, rows), :] = _silu(z).astype(o_ref.dtype)
        return carry

    lax.fori_loop(0, t_len // rows, step, 0, unroll=4)


def _conv(proj, dw_w, dw_b, ln_w, ln_b, batch, t_len, first_col_block):
    ch = dw_w.shape[1]
    vec = pl.BlockSpec((1, ch), lambda b: (0, 0))
    return pl.pallas_call(
        _conv_body,
        grid=(batch,),
        in_specs=[pl.BlockSpec((t_len, ch), lambda b: (b, first_col_block)),
                  pl.BlockSpec((t_len, ch), lambda b: (b, first_col_block + 1)),
                  pl.BlockSpec((CONV_LEN, ch), lambda b: (0, 0)), vec, vec, vec],
        out_specs=pl.BlockSpec((t_len, ch), lambda b: (b, 0)),
        out_shape=jax.ShapeDtypeStruct((batch * t_len, ch), BF16),
        scratch_shapes=[pltpu.VMEM((ch // LANES, t_len + 2 * CONV_HALO, LANES), F32),
                        pltpu.VMEM((t_len, ch), F32)],
        compiler_params=_cparams(("parallel",)),
        name="conv",
    )(proj, proj, dw_w, dw_b.reshape(1, ch), ln_w.reshape(1, ch), ln_b.reshape(1, ch))


def _mix_body(a_ref, b_ref, x_ref, wa_ref, wb_ref, nw_ref, wr_ref, rb_ref, tri_ref, elow_ref,
              h_ref, n2_ref, dloc_ref, gate_ref, cnt_ref):
    tm = x_ref.shape[0]
    ne = wr_ref.shape[0] // 2

    h = x_ref[...] + jnp.dot(a_ref[...], wa_ref[...], preferred_element_type=F32) \
        + jnp.dot(b_ref[...], wb_ref[...], preferred_element_type=F32)
    h_ref[...] = h
    n2 = h * lax.rsqrt(jnp.mean(h * h, axis=-1, keepdims=True) + EPS) * nw_ref[...]
    hi = n2.astype(BF16)
    n2_ref[...] = hi
    lo = (n2 - hi.astype(F32)).astype(BF16)
    by_hi = lax.dot_general(wr_ref[...], hi, _NT, preferred_element_type=F32)
    by_lo = lax.dot_general(wr_ref[0:ne, :], lo, _NT, preferred_element_type=F32)
    logits = by_hi[0:ne] + by_hi[ne:] + by_lo + rb_ref[...]
    e_iota = lax.broadcasted_iota(I32, (ne, tm), 0)
    work = logits
    sels, vals = [], []
    for _ in range(TOP_K):
        mx = jnp.max(work, axis=0, keepdims=True)
        idx = jnp.min(jnp.where(work == mx, e_iota, ne), axis=0, keepdims=True)
        sel = e_iota == idx
        work = jnp.where(sel, -jnp.inf, work)
        sels.append(sel)
        vals.append(mx)
    exps = [jnp.exp(v - vals[0]) for v in vals]
    denom = exps[0] + exps[1] + exps[2] + exps[3]
    chosen = jnp.zeros((ne, tm), F32)
    for sel in sels:
        chosen = jnp.where(sel, 1.0, chosen)
    chosen_bf = chosen.astype(BF16)
    prior = jnp.dot(chosen_bf, tri_ref[...], preferred_element_type=F32)
    cnt = jnp.dot(chosen_bf, jnp.ones((tm, LANES), BF16), preferred_element_type=F32)
    cnt_ref[0] = cnt
    share = jnp.floor((cnt + (SUBLANES - 1)) * (1.0 / SUBLANES)) * SUBLANES
    lower = jnp.dot(elow_ref[...], share.astype(BF16), preferred_element_type=F32)
    slab_row = prior + jnp.concatenate([lower] * (tm // LANES), axis=1)
    for k in range(TOP_K):
        dloc_ref[k:k + 1, :] = jnp.sum(jnp.where(sels[k], slab_row, 0.0), axis=0, keepdims=True).astype(I32)
        gate_ref[k:k + 1, :] = exps[k] / denom


def _mix(a, b, x2, w_out_bf, norm_w, router_w, router_b, tm=ROUTE_TILE):
    n_tok, d = x2.shape
    wa, wb = w_out_bf[:a.shape[1]], w_out_bf[a.shape[1]:]
    ne = router_w.shape[1]
    wr_t = router_w.T
    wr_hi = wr_t.astype(BF16)
    wr_lo = (wr_t - wr_hi.astype(F32)).astype(BF16)
    tri = jnp.asarray(np.triu(np.ones((tm, tm), np.float32), k=1), BF16)
    e_lower = jnp.asarray(np.tril(np.ones((ne, ne), np.float32), k=-1), BF16)
    const = lambda shape: pl.BlockSpec(shape, lambda i: tuple(0 for _ in shape))
    return pl.pallas_call(
        _mix_body,
        grid=(n_tok // tm,),
        in_specs=[pl.BlockSpec((tm, a.shape[1]), lambda i: (i, 0)),
                  pl.BlockSpec((tm, b.shape[1]), lambda i: (i, 0)),
                  pl.BlockSpec((tm, d), lambda i: (i, 0)),
                  const(wa.shape), const(wb.shape), const((1, d)),
                  const((2 * ne, d)), const((ne, 1)), const((tm, tm)), const((ne, ne))],
        out_specs=[pl.BlockSpec((tm, d), lambda i: (i, 0)),
                   pl.BlockSpec((tm, d), lambda i: (i, 0)),
                   pl.BlockSpec((TOP_K, tm), lambda i: (0, i)),
                   pl.BlockSpec((TOP_K, tm), lambda i: (0, i)),
                   pl.BlockSpec((1, ne, LANES), lambda i: (i, 0, 0))],
        out_shape=[jax.ShapeDtypeStruct((n_tok, d), F32),
                   jax.ShapeDtypeStruct((n_tok, d), BF16),
                   jax.ShapeDtypeStruct((TOP_K, n_tok), I32),
                   jax.ShapeDtypeStruct((TOP_K, n_tok), F32),
                   jax.ShapeDtypeStruct((n_tok // tm, ne, LANES), F32)],
        compiler_params=_cparams(("parallel",)),
        name="mix_router",
    )(a, b, x2, wa, wb, norm_w.reshape(1, d), jnp.concatenate([wr_hi, wr_lo], axis=0), router_b.reshape(ne, 1),
      tri, e_lower)


def _powers_of_two_upto(n):
    return tuple(1 << i for i in reversed(range(n.bit_length())))


SLAB_ROWS = TOP_K * ROUTE_TILE + N_EXPERTS * SUBLANES
SLAB_CHUNK = 256
SLAB_PIECES = _powers_of_two_upto(ROUTE_TILE // SUBLANES)
SLAB_COMMON = ROUTE_TILE * TOP_K // N_EXPERTS // SUBLANES


def _for_each_piece(ngroups, fn):
    def pieces(sizes):
        for size in sizes:
            first = lax.bitwise_and(ngroups, ~(2 * size - 1))

            @pl.when(lax.bitwise_and(ngroups, size) != 0)
            def _():
                fn(first, size)

    rare = tuple(s for s in SLAB_PIECES if s >= 2 * SLAB_COMMON)

    @pl.when(ngroups >= min(rare))
    def _():
        pieces(rare)

    pieces(tuple(s for s in SLAB_PIECES if s < 2 * SLAB_COMMON))


def _group_rows(first_group, groups):
    return pl.ds(pl.multiple_of(first_group * SUBLANES, SUBLANES), groups * SUBLANES)


def _wait_groups(ngroups, stage_slot, hbm, sem):
    for size in _powers_of_two_upto(SLAB_ROWS // SUBLANES):
        @pl.when(lax.bitwise_and(ngroups, size) != 0)
        def _():
            pltpu.make_async_copy(hbm.at[_group_rows(0, size)], stage_slot.at[_group_rows(0, size)], sem).wait()


def _dispatch_body(src_ref, dst_ref, ngrp_ref, tot_ref, n2_ref, dloc_ref, xs_hbm, stage, sem):
    step = pl.program_id(0)
    tm = n2_ref.shape[0]
    slot = step % 2

    def shares(at_step, at_slot, act):
        def per_expert(e, carry):
            idx = at_step * N_EXPERTS + e
            src, dst = src_ref[idx], dst_ref[idx]

            def piece(first, size):
                act(pltpu.make_async_copy(stage.at[at_slot, _group_rows(src + first, size)],
                                          xs_hbm.at[_group_rows(dst + first, size)], sem.at[at_slot]))

            _for_each_piece(ngrp_ref[idx], piece)
            return carry

        lax.fori_loop(0, N_EXPERTS, per_expert, 0)

    def wait_slab(at_step, at_slot):
        _wait_groups(tot_ref[at_step], stage.at[at_slot], xs_hbm, sem.at[at_slot])

    @pl.when(step >= 2)
    def _():
        wait_slab(step - 2, slot)

    n2 = n2_ref[...]
    dloc = dloc_ref[...]
    for c in range(SLAB_ROWS // SLAB_CHUNK):
        rows = lax.broadcasted_iota(I32, (SLAB_CHUNK, tm), 0) + c * SLAB_CHUNK
        pick = jnp.zeros((SLAB_CHUNK, tm), F32)
        for k in range(TOP_K):
            pick = jnp.where(dloc[k:k + 1, :] == rows, 1.0, pick)
        stage[slot, c * SLAB_CHUNK:(c + 1) * SLAB_CHUNK, :] = jnp.dot(pick.astype(BF16), n2,
                                                                       preferred_element_type=F32)
    shares(step, slot, lambda cp: cp.start())

    @pl.when(step == pl.num_programs(0) - 1)
    def _():
        wait_slab(step, slot)

        @pl.when(step >= 1)
        def _():
            wait_slab(step - 1, 1 - slot)


def _dispatch(shares, n2_bf, dloc, n_rows):
    n_tok, d = n2_bf.shape
    tm = ROUTE_TILE
    return pl.pallas_call(
        _dispatch_body,
        grid_spec=pltpu.PrefetchScalarGridSpec(
            num_scalar_prefetch=4,
            grid=(n_tok // tm,),
            in_specs=[pl.BlockSpec((tm, d), lambda i, *_: (i, 0)),
                      pl.BlockSpec((TOP_K, tm), lambda i, *_: (0, i))],
            out_specs=pl.BlockSpec(memory_space=pl.ANY),
            scratch_shapes=[pltpu.VMEM((2, SLAB_ROWS, d), F32), pltpu.SemaphoreType.DMA((2,))]),
        out_shape=jax.ShapeDtypeStruct((n_rows, d), F32),
        compiler_params=_cparams(("arbitrary",)),
        name="dispatch",
    )(*shares, n2_bf, dloc)


FFN_CAST_ROWS = 256
FFN_STEP_BLOCKS = 2


def _ffn_body(be_ref, first_ref, slot_ref, next_ref, nact_ref, xs_ref, w1_hbm, b1_ref, w2_hbm, b2_ref, ys_ref,
              w1_f, w2_f, w1_b, w2_b, sem):
    first_block = pl.program_id(0) * FFN_STEP_BLOCKS
    n_active = nact_ref[0]

    def fetch(expert, slot):
        return (pltpu.make_async_copy(w1_hbm.at[expert], w1_f.at[slot], sem.at[0, slot]),
                pltpu.make_async_copy(w2_hbm.at[expert], w2_f.at[slot], sem.at[1, slot]))

    def start_run(blk):
        slot = slot_ref[blk]

        @pl.when(blk == 0)
        def _():
            for cp in fetch(be_ref[0], slot):
                cp.start()

        for cp in fetch(be_ref[blk], slot):
            cp.wait()

        @pl.when(next_ref[blk] >= 0)
        def _():
            for cp in fetch(next_ref[blk], 1 - slot):
                cp.start()

    def compute(blk, lo, hi, fresh=False):
        expert = be_ref[blk]
        if fresh:
            slot = slot_ref[blk]
            w1 = w1_f[slot].astype(BF16)
            w2 = w2_f[slot].astype(BF16)
            w1_b[...] = w1
            w2_b[...] = w2
        else:
            w1, w2 = w1_b[...], w2_b[...]
        x = xs_ref[lo:hi, :].astype(BF16)
        hdn = jnp.dot(x, w1, preferred_element_type=F32) + b1_ref[expert]
        d_ff = hdn.shape[1] // 2
        glu = jnp.minimum(hdn[:, :d_ff], SWIGLU_LIMIT)
        lin = jnp.clip(hdn[:, d_ff:], -SWIGLU_LIMIT, SWIGLU_LIMIT)
        act = glu * _sigmoid(SWIGLU_ALPHA * glu) * (lin + 1.0)
        ys_ref[lo:hi, :] = jnp.dot(act.astype(BF16), w2, preferred_element_type=F32) + b2_ref[expert]

    second = first_block + 1

    @pl.when(first_block < n_active)
    def _():
        starts_run = first_ref[first_block] == 1

        @pl.when(starts_run)
        def _():
            start_run(first_block)

        same_run = jnp.logical_and(second < n_active, first_ref[second] == 0)
        for fresh in (False, True):
            matches = starts_run if fresh else jnp.logical_not(starts_run)

            @pl.when(jnp.logical_and(matches, same_run))
            def _():
                compute(first_block, 0, 2 * MOE_BLOCK, fresh)

            @pl.when(jnp.logical_and(matches, jnp.logical_not(same_run)))
            def _():
                compute(first_block, 0, MOE_BLOCK, fresh)

                @pl.when(second < n_active)
                def _():
                    start_run(second)
                    compute(second, MOE_BLOCK, 2 * MOE_BLOCK, True)


def _ffn(block_expert, n_active, xs, w1, b1, w2, b2):
    ne, d, f2 = w1.shape
    nb = xs.shape[0] // MOE_BLOCK
    assert nb % FFN_STEP_BLOCKS == 0
    block = (FFN_STEP_BLOCKS * MOE_BLOCK, d)
    idx = jnp.arange(nb, dtype=I32)
    active = idx < n_active[0]
    prev = jnp.concatenate([block_expert[:1] - 1, block_expert[:-1]])
    first = (active & (block_expert != prev)).astype(I32)
    run_slot = (jnp.sum(jnp.where(idx[None, :] <= idx[:, None], first[None, :], 0), axis=1) - 1) & 1
    later_first = (first[None, :] == 1) & (idx[None, :] > idx[:, None])
    next_block = jnp.min(jnp.where(later_first, idx[None, :], nb), axis=1)
    next_expert = jnp.where(next_block < nb, block_expert[jnp.minimum(next_block, nb - 1)], -1).astype(I32)

    def blk(i, be, fi, sl, nx, nact):
        return (jnp.minimum(i, (nact[0] + FFN_STEP_BLOCKS - 1) // FFN_STEP_BLOCKS - 1), 0)

    whole = lambda shape: pl.BlockSpec(shape, lambda i, *_: (0,) * len(shape))

    return pl.pallas_call(
        _ffn_body,
        grid_spec=pltpu.PrefetchScalarGridSpec(
            num_scalar_prefetch=5,
            grid=(nb // FFN_STEP_BLOCKS,),
            in_specs=[pl.BlockSpec(block, blk),
                      pl.BlockSpec(memory_space=pl.ANY),
                      whole((ne, 1, f2)),
                      pl.BlockSpec(memory_space=pl.ANY),
                      whole((ne, 1, d))],
            out_specs=pl.BlockSpec(block, blk),
            scratch_shapes=[pltpu.VMEM((2, d, f2), F32), pltpu.VMEM((2, f2 // 2, d), F32),
                            pltpu.VMEM((d, f2), BF16), pltpu.VMEM((f2 // 2, d), BF16),
                            pltpu.SemaphoreType.DMA((2, 2))]),
        out_shape=jax.ShapeDtypeStruct(xs.shape, F32),
        compiler_params=_cparams(("arbitrary",)),
        name="expert_ffn",
    )(block_expert, first, run_slot.astype(I32), next_expert, n_active, xs, w1, b1.reshape(ne, 1, f2), w2,
      b2.reshape(ne, 1, d))


def _combine_body(src_ref, dst_ref, ngrp_ref, tot_ref, h_ref, dloc_ref, gate_ref, nw_ref, ys_hbm, o_ref, stage,
                  sem):
    step = pl.program_id(0)
    tm = h_ref.shape[0]
    slot = step % 2

    def shares(at_step, at_slot, act):
        def per_expert(e, carry):
            idx = at_step * N_EXPERTS + e
            src, dst = src_ref[idx], dst_ref[idx]

            def piece(first, size):
                act(pltpu.make_async_copy(ys_hbm.at[_group_rows(dst + first, size)],
                                          stage.at[at_slot, _group_rows(src + first, size)], sem.at[at_slot]))

            _for_each_piece(ngrp_ref[idx], piece)
            return carry

        lax.fori_loop(0, N_EXPERTS, per_expert, 0)

    @pl.when(step == 0)
    def _():
        stage[...] = jnp.zeros(stage.shape, F32)
        shares(0, 0, lambda cp: cp.start())

    @pl.when(step + 1 < pl.num_programs(0))
    def _():
        shares(step + 1, 1 - slot, lambda cp: cp.start())

    _wait_groups(tot_ref[step], stage.at[slot], ys_hbm, sem.at[slot])
    dloc = dloc_ref[...]
    gates = gate_ref[...]
    moe = jnp.zeros(o_ref.shape, F32)
    for c in range(SLAB_ROWS // SLAB_CHUNK):
        srow = lax.broadcasted_iota(I32, (SLAB_CHUNK, tm), 0) + c * SLAB_CHUNK
        weight_t = jnp.zeros((SLAB_CHUNK, tm), F32)
        for k in range(TOP_K):
            weight_t = jnp.where(dloc[k:k + 1, :] == srow, gates[k:k + 1, :], weight_t)
        rows = stage[slot, c * SLAB_CHUNK:(c + 1) * SLAB_CHUNK, :].astype(BF16)
        moe = moe + lax.dot_general(weight_t.astype(BF16), rows, (((0,), (0,)), ((), ())),
                                    preferred_element_type=F32)
    y = h_ref[...] + moe
    o_ref[...] = y * lax.rsqrt(jnp.mean(y * y, axis=-1, keepdims=True) + EPS) * nw_ref[...]


def _combine(shares, h, dloc_tk, gates_tk, norm_w, ys):
    n_tok, d = h.shape
    tm = ROUTE_TILE
    return pl.pallas_call(
        _combine_body,
        grid_spec=pltpu.PrefetchScalarGridSpec(
            num_scalar_prefetch=4,
            grid=(n_tok // tm,),
            in_specs=[pl.BlockSpec((tm, d), lambda i, *_: (i, 0)),
                      pl.BlockSpec((TOP_K, tm), lambda i, *_: (0, i)),
                      pl.BlockSpec((TOP_K, tm), lambda i, *_: (0, i)),
                      pl.BlockSpec((1, d), lambda i, *_: (0, 0)),
                      pl.BlockSpec(memory_space=pl.ANY)],
            out_specs=pl.BlockSpec((tm, d), lambda i, *_: (i, 0)),
            scratch_shapes=[pltpu.VMEM((2, SLAB_ROWS, d), F32), pltpu.SemaphoreType.DMA((2,))]),
        out_shape=jax.ShapeDtypeStruct((n_tok, d), F32),
        compiler_params=_cparams(("arbitrary",)),
        name="combine_norm",
    )(*shares, h, dloc_tk, gates_tk, norm_w.reshape(1, d), ys)


def kernel(x, norm1_w, w_in, lb_logits, hgrn_norm_w, dw_w, dw_b, conv_ln_w, conv_ln_b, w_out, norm2_w,
           router_w, router_b, w1, b1, w2, b2, final_norm_w):
    batch, t_len, d = x.shape
    assert w_in.shape[0] == 1, "single-layer block"
    n_tok = batch * t_len
    hk = HGRN_HEADS * HEAD_DIM
    conv_ch = dw_w.shape[2]
    lb_table = jnp.cumsum(jax.nn.softmax(lb_logits.astype(F32), axis=1), axis=1)
    x2 = x.reshape(n_tok, d)
    proj = _in_proj(x2, norm1_w[0], w_in[0].astype(BF16))
    a = _hgrn(proj, lb_table[0, 0], lb_table[1, 0], hgrn_norm_w[0], batch, t_len)
    b = _conv(proj, dw_w[0], dw_b[0], conv_ln_w[0], conv_ln_b[0], batch, t_len, (5 * hk) // conv_ch)
    h_mid, n2_bf, dloc, gate, cnt_tiles = _mix(a, b, x2, w_out[0].astype(BF16), norm2_w[0],
                                                     router_w[0], router_b[0])
    nt = n_tok // ROUTE_TILE
    e_ids = jnp.arange(N_EXPERTS, dtype=I32)
    t_ids = jnp.arange(nt, dtype=I32)
    share = (cnt_tiles[:, :, 0].astype(I32) + SUBLANES - 1) // SUBLANES * SUBLANES
    in_slab = jnp.sum(jnp.where(e_ids[None, None, :] < e_ids[None, :, None], share[:, None, :], 0), axis=2)
    in_expert = jnp.sum(jnp.where((t_ids[None, :] < t_ids[:, None])[:, :, None], share[None, :, :], 0), axis=1)
    padded = (jnp.sum(share, axis=0) + MOE_BLOCK - 1) // MOE_BLOCK * MOE_BLOCK
    pad_end = jnp.sum(jnp.where(e_ids[None, :] <= e_ids[:, None], padded[None, :], 0), axis=1)
    in_xs = (pad_end - padded)[None, :] + in_expert
    n_blocks = -(-(n_tok * TOP_K + nt * N_EXPERTS * (SUBLANES - 1)) // MOE_BLOCK) + N_EXPERTS
    n_blocks += -n_blocks % FFN_STEP_BLOCKS
    block_start = jnp.arange(n_blocks, dtype=I32) * MOE_BLOCK
    block_expert = jnp.minimum(jnp.sum((pad_end[None, :] <= block_start[:, None]).astype(I32), axis=1),
                               N_EXPERTS - 1)
    n_active = pad_end[-1:] // MOE_BLOCK
    shares = [(v // SUBLANES).reshape(-1) for v in (in_slab, in_xs, share)] + [jnp.sum(share, axis=1) // SUBLANES]
    xs = _dispatch(shares, n2_bf, dloc, n_blocks * MOE_BLOCK)
    ys = _ffn(block_expert, n_active, xs, w1[0], b1[0], w2[0], b2[0])
    out = _combine(shares, h_mid, dloc, gate, final_norm_w, ys)
    return out.reshape(batch, t_len, d)
```

```python
import functools
import math

import numpy as np
import jax
import jax.numpy as jnp
from jax import lax
from jax.experimental import pallas as pl
from jax.experimental.pallas import tpu as pltpu

F32 = jnp.float32
BF16 = jnp.bfloat16
I32 = jnp.int32

EPS = 1e-5
HGRN_HEADS = 4
HEAD_DIM = 128
HGRN_CHUNK = 256
CONV_LEN = 31
N_EXPERTS = 32
TOP_K = 4
SWIGLU_LIMIT = 7.0
SWIGLU_ALPHA = 1.702
MOE_BLOCK = 256
ROUTE_TILE = 512
LANES = 128
SUBLANES = 8
VMEM_LIMIT = 56 << 20

_NT = (((1,), (1,)), ((), ()))


def _sigmoid(x):
    return 0.5 * jnp.tanh(0.5 * x) + 0.5


def _silu(x):
    return x * _sigmoid(x)


def _cparams(sem):
    return pltpu.CompilerParams(dimension_semantics=sem, vmem_limit_bytes=VMEM_LIMIT)


def _inproj_body(x_ref, nw_ref, w_ref, o_ref):
    x = x_ref[...]
    n = x * lax.rsqrt(jnp.mean(x * x, axis=-1, keepdims=True) + EPS) * nw_ref[...]
    o_ref[...] = jnp.dot(n.astype(BF16), w_ref[...], preferred_element_type=F32)


def _in_proj(x2, norm_w, w_bf, tm=512):
    n_tok, d = x2.shape
    cols = w_bf.shape[1]
    return pl.pallas_call(
        _inproj_body,
        grid=(n_tok // tm,),
        in_specs=[pl.BlockSpec((tm, d), lambda i: (i, 0)),
                  pl.BlockSpec((1, d), lambda i: (0, 0)),
                  pl.BlockSpec((d, cols), lambda i: (0, 0))],
        out_specs=pl.BlockSpec((tm, cols), lambda i: (i, 0)),
        out_shape=jax.ShapeDtypeStruct((n_tok, cols), F32),
        compiler_params=_cparams(("parallel",)),
        name="in_proj",
    )(x2, norm_w.reshape(1, d), w_bf)


def _hgrn_levels(c):
    return [c >> (i + 1) for i in range(int(math.log2(c)))]


def _hgrn_sum_matrices(c, fwd):
    r = np.arange(c)
    m = _HGRN_SMALL_LEVEL
    small = np.zeros((c, c), np.float32)
    for t in range(c):
        p0 = t & ~(2 * m - 1)
        upper = (t & m) != 0
        if fwd:
            if upper:
                small[t, p0 + m:t + 1] = 1.0
            else:
                small[t, t + 1:p0 + m] = 1.0
        else:
            if upper:
                small[t, p0 + m:t] = 1.0
            else:
                small[t, t:p0 + m] = 1.0
    mats = [r[None, :] <= r[:, None] if fwd else r[None, :] >= r[:, None], small]
    return np.concatenate([np.asarray(x, np.float32) for x in mats], axis=0)


_HGRN_SMALL_LEVEL = 2
_HGRN_ROW_SLICE = 16


def _hgrn_chunk(q, k, lf, v_bf, vt_bf, st, sums_ref, fwd, c):
    dk = q.shape[1]
    hi = lf.astype(BF16)
    lo = (lf - hi.astype(F32)).astype(BF16)
    both = jnp.dot(sums_ref[...], jnp.concatenate([hi, lo], axis=1), preferred_element_type=F32)
    sums = both[:, :dk] + both[:, dk:]
    b = sums[0:c]
    edge_row = c - 1 if fwd else 0
    d_out = b[edge_row:edge_row + 1] - b
    row = lax.broadcasted_iota(I32, (c, dk), 0)
    rr = lax.broadcasted_iota(I32, (c, c), 0)
    cc = lax.broadcasted_iota(I32, (c, c), 1)
    sep = rr ^ cc
    a = None
    for m in _hgrn_levels(c):
        upper = (row & m) != 0
        is_q = upper if fwd else jnp.logical_not(upper)
        if m == 1:
            d = jnp.where(is_q, lf, 0.0)
        elif m == _HGRN_SMALL_LEVEL:
            d = sums[c:2 * c]
        else:
            blocks = c // (2 * m)
            at = m - 1 if fwd else m
            edge = b.reshape(blocks, 2 * m, dk)[:, at:at + 1, :]
            edge = jnp.broadcast_to(edge, (blocks, 2 * m, dk)).reshape(c, dk)
            d = jnp.where(is_q, b - edge, edge - b)
        r = (jnp.where(is_q, q, k) * jnp.exp2(d)).astype(BF16)
        if m < _HGRN_ROW_SLICE:
            a_l = lax.dot_general(r, r, _NT, preferred_element_type=F32)
            a = a_l if a is None else jnp.where(sep < 2 * m, a_l, a)
        else:
            halves = [(p0 + m, p0 + 2 * m) if fwd else (p0, p0 + m) for p0 in range(0, c, 2 * m)]
            a_q = lax.dot_general(jnp.concatenate([r[lo:hi] for lo, hi in halves], axis=0), r, _NT,
                                  preferred_element_type=F32)
            parts = []
            for n, (lo, hi) in enumerate(halves):
                new = a_q[n * m:(n + 1) * m]
                if a is None:
                    kept = jnp.zeros((m, c), F32)
                else:
                    new = jnp.where(sep[lo:hi] < 2 * m, new, a[lo:hi])
                    kept = a[lo - m:lo] if fwd else a[hi:hi + m]
                parts += [kept, new] if fwd else [new, kept]
            a = jnp.concatenate(parts, axis=0)
    a_diag = lax.dot_general(q.astype(BF16), k.astype(BF16), _NT, preferred_element_type=F32)
    a = jnp.where(sep < 1, a_diag, a)
    a = jnp.where(rr >= cc if fwd else rr <= cc, a, 0.0)
    e_b = jnp.exp2(b)
    q_in = (q * e_b).astype(BF16)
    k_out = (k * jnp.exp2(d_out)).astype(BF16)
    o = jnp.dot(a.astype(BF16), v_bf, preferred_element_type=F32)
    o = o + lax.dot_general(q_in, st.astype(BF16), _NT, preferred_element_type=F32)
    st_new = st * e_b[edge_row:edge_row + 1] + jnp.dot(vt_bf, k_out, preferred_element_type=F32)
    return o, st_new


def _hgrn_body(qz_ref, ff_ref, fb_ref, iv_ref, gz_ref, lbf_ref, lbb_ref, nw_ref, sf_ref, sb_ref, o_ref,
               q_s, kf_s, lff_s, kb_s, lfb_s, v_s, vt_s, acc_s, *, c):
    t_len, dk = qz_ref.shape
    nch = t_len // c
    q_s[...] = _silu(qz_ref[...])

    def gate(fz_ref, lb_ref, k_s, lf_s):
        lb = lb_ref[0]
        f = lb + (1.0 - lb) * _sigmoid(fz_ref[...])
        lf_s[...] = jnp.log2(f)
        k_s[...] = 1.0 - f

    gate(ff_ref, lbf_ref, kf_s, lff_s)
    gate(fb_ref, lbb_ref, kb_s, lfb_s)
    v = iv_ref[...]
    v_s[...] = v.astype(BF16)
    for ci in range(nch):
        vt_s[ci] = v[ci * c:(ci + 1) * c, :].T.astype(BF16)

    def run(k_s, lf_s, sums_ref, fwd):
        def step(i, st):
            ci = i if fwd else nch - 1 - i
            rows = pl.ds(pl.multiple_of(ci * c, c), c)
            o, st = _hgrn_chunk(q_s[rows, :], k_s[rows, :], lf_s[rows, :], v_s[rows, :], vt_s[ci],
                                st, sums_ref, fwd, c)
            if fwd:
                acc_s[rows, :] = o
            else:
                acc_s[rows, :] += o
            return st
        lax.fori_loop(0, nch, step, jnp.zeros((dk, dk), F32), unroll=True)

    run(kf_s, lff_s, sf_ref, True)
    run(kb_s, lfb_s, sb_ref, False)
    o = acc_s[...]
    o = o * lax.rsqrt(jnp.mean(o * o, axis=-1, keepdims=True) + EPS) * nw_ref[0]
    o_ref[...] = (o * _silu(gz_ref[...])).astype(o_ref.dtype)


def _hgrn(proj, lb_f, lb_b, norm_w, batch, t_len):
    h, dk, c = HGRN_HEADS, HEAD_DIM, HGRN_CHUNK
    nrows = 2
    sums_f = jnp.asarray(_hgrn_sum_matrices(c, True), BF16)
    sums_b = jnp.asarray(_hgrn_sum_matrices(c, False), BF16)

    def col(group):
        return pl.BlockSpec((t_len, dk), lambda b, hh: (b, group * h + hh))

    def per_head():
        return pl.BlockSpec((1, 1, dk), lambda b, hh: (hh, 0, 0))

    const = pl.BlockSpec((nrows * c, c), lambda b, hh: (0, 0))
    seq = lambda dt: pltpu.VMEM((t_len, dk), dt)
    return pl.pallas_call(
        functools.partial(_hgrn_body, c=c),
        grid=(batch, h),
        in_specs=[col(0), col(1), col(2), col(3), col(4), per_head(), per_head(), per_head(), const, const],
        out_specs=pl.BlockSpec((t_len, dk), lambda b, hh: (b, hh)),
        out_shape=jax.ShapeDtypeStruct((batch * t_len, h * dk), BF16),
        scratch_shapes=[seq(F32), seq(F32), seq(F32), seq(F32), seq(F32), seq(BF16),
                        pltpu.VMEM((t_len // c, dk, c), BF16), seq(F32)],
        compiler_params=_cparams(("parallel", "parallel")),
        name="hgrn",
    )(proj, proj, proj, proj, proj, lb_f.reshape(h, 1, dk), lb_b.reshape(h, 1, dk),
      norm_w.reshape(h, 1, dk), sums_f, sums_b)


CONV_ROWS = 64
CONV_HALO = 16


def _conv_body(cv_ref, cg_ref, w_ref, b_ref, lnw_ref, lnb_ref, o_ref, u_s, y_s):
    t_len, ch = cv_ref.shape
    halo, rows = CONV_HALO, CONV_ROWS
    shift0 = halo - (CONV_LEN - 1) // 2
    win = rows + 2 * halo
    for g in range(ch // LANES):
        lanes = slice(g * LANES, (g + 1) * LANES)
        u_s[g, 0:halo, :] = jnp.zeros((halo, LANES), F32)
        u_s[g, halo + t_len:, :] = jnp.zeros((halo, LANES), F32)
        u_s[g, halo:halo + t_len, :] = cv_ref[:, lanes] * _sigmoid(cg_ref[:, lanes])

    def conv_step(i, carry):
        t0 = pl.multiple_of(i * rows, rows)
        for g in range(ch // LANES):
            lanes = slice(g * LANES, (g + 1) * LANES)
            window = u_s.at[g, pl.ds(t0, win), :]
            acc = jnp.zeros((rows, LANES), F32)
            for j in range(CONV_LEN):
                off = j + shift0
                acc = acc + w_ref[j:j + 1, lanes] * window[off:off + rows, :]
            y_s[pl.ds(t0, rows), lanes] = acc + b_ref[:, lanes]
        return carry

    lax.fori_loop(0, t_len // rows, conv_step, 0)

    def step(i, carry):
        t0 = pl.multiple_of(i * rows, rows)
        y = y_s[pl.ds(t0, rows), :]
        mu = jnp.mean(y, axis=-1, keepdims=True)
        yc = y - mu
        var = jnp.mean(yc * yc, axis=-1, keepdims=True)
        z = yc * lax.rsqrt(var + EPS) * lnw_ref[...] + lnb_ref[...]
        o_ref[pl.ds(t0, rows), :] = _silu(z).astype(o_ref.dtype)
        return carry

    lax.fori_loop(0, t_len // rows, step, 0, unroll=4)


def _conv(proj, dw_w, dw_b, ln_w, ln_b, batch, t_len, first_col_block):
    ch = dw_w.shape[1]
    vec = pl.BlockSpec((1, ch), lambda b: (0, 0))
    return pl.pallas_call(
        _conv_body,
        grid=(batch,),
        in_specs=[pl.BlockSpec((t_len, ch), lambda b: (b, first_col_block)),
                  pl.BlockSpec((t_len, ch), lambda b: (b, first_col_block + 1)),
                  pl.BlockSpec((CONV_LEN, ch), lambda b: (0, 0)), vec, vec, vec],
        out_specs=pl.BlockSpec((t_len, ch), lambda b: (b, 0)),
        out_shape=jax.ShapeDtypeStruct((batch * t_len, ch), BF16),
        scratch_shapes=[pltpu.VMEM((ch // LANES, t_len + 2 * CONV_HALO, LANES), F32),
                        pltpu.VMEM((t_len, ch), F32)],
        compiler_params=_cparams(("parallel",)),
        name="conv",
    )(proj, proj, dw_w, dw_b.reshape(1, ch), ln_w.reshape(1, ch), ln_b.reshape(1, ch))


def _mix_body(a_ref, b_ref, x_ref, wa_ref, wb_ref, nw_ref, wr_ref, rb_ref, tri_ref, elow_ref,
              h_ref, n2_ref, dloc_ref, gate_ref, cnt_ref):
    tm = x_ref.shape[0]
    ne = wr_ref.shape[0] // 2

    h = x_ref[...] + jnp.dot(a_ref[...], wa_ref[...], preferred_element_type=F32) \
        + jnp.dot(b_ref[...], wb_ref[...], preferred_element_type=F32)
    h_ref[...] = h
    n2 = h * lax.rsqrt(jnp.mean(h * h, axis=-1, keepdims=True) + EPS) * nw_ref[...]
    hi = n2.astype(BF16)
    n2_ref[...] = hi
    lo = (n2 - hi.astype(F32)).astype(BF16)
    by_hi = lax.dot_general(wr_ref[...], hi, _NT, preferred_element_type=F32)
    by_lo = lax.dot_general(wr_ref[0:ne, :], lo, _NT, preferred_element_type=F32)
    logits = by_hi[0:ne] + by_hi[ne:] + by_lo + rb_ref[...]
    e_iota = lax.broadcasted_iota(I32, (ne, tm), 0)
    work = logits
    sels, vals = [], []
    for _ in range(TOP_K):
        mx = jnp.max(work, axis=0, keepdims=True)
        idx = jnp.min(jnp.where(work == mx, e_iota, ne), axis=0, keepdims=True)
        sel = e_iota == idx
        work = jnp.where(sel, -jnp.inf, work)
        sels.append(sel)
        vals.append(mx)
    exps = [jnp.exp(v - vals[0]) for v in vals]
    denom = exps[0] + exps[1] + exps[2] + exps[3]
    chosen = jnp.zeros((ne, tm), F32)
    for sel in sels:
        chosen = jnp.where(sel, 1.0, chosen)
    chosen_bf = chosen.astype(BF16)
    prior = jnp.dot(chosen_bf, tri_ref[...], preferred_element_type=F32)
    cnt = jnp.dot(chosen_bf, jnp.ones((tm, LANES), BF16), preferred_element_type=F32)
    cnt_ref[0] = cnt
    share = jnp.floor((cnt + (SUBLANES - 1)) * (1.0 / SUBLANES)) * SUBLANES
    lower = jnp.dot(elow_ref[...], share.astype(BF16), preferred_element_type=F32)
    slab_row = prior + jnp.concatenate([lower] * (tm // LANES), axis=1)
    for k in range(TOP_K):
        dloc_ref[k:k + 1, :] = jnp.sum(jnp.where(sels[k], slab_row, 0.0), axis=0, keepdims=True).astype(I32)
        gate_ref[k:k + 1, :] = exps[k] / denom


def _mix(a, b, x2, w_out_bf, norm_w, router_w, router_b, tm=ROUTE_TILE):
    n_tok, d = x2.shape
    wa, wb = w_out_bf[:a.shape[1]], w_out_bf[a.shape[1]:]
    ne = router_w.shape[1]
    wr_t = router_w.T
    wr_hi = wr_t.astype(BF16)
    wr_lo = (wr_t - wr_hi.astype(F32)).astype(BF16)
    tri = jnp.asarray(np.triu(np.ones((tm, tm), np.float32), k=1), BF16)
    e_lower = jnp.asarray(np.tril(np.ones((ne, ne), np.float32), k=-1), BF16)
    const = lambda shape: pl.BlockSpec(shape, lambda i: tuple(0 for _ in shape))
    return pl.pallas_call(
        _mix_body,
        grid=(n_tok // tm,),
        in_specs=[pl.BlockSpec((tm, a.shape[1]), lambda i: (i, 0)),
                  pl.BlockSpec((tm, b.shape[1]), lambda i: (i, 0)),
                  pl.BlockSpec((tm, d), lambda i: (i, 0)),
                  const(wa.shape), const(wb.shape), const((1, d)),
                  const((2 * ne, d)), const((ne, 1)), const((tm, tm)), const((ne, ne))],
        out_specs=[pl.BlockSpec((tm, d), lambda i: (i, 0)),
                   pl.BlockSpec((tm, d), lambda i: (i, 0)),
                   pl.BlockSpec((TOP_K, tm), lambda i: (0, i)),
                   pl.BlockSpec((TOP_K, tm), lambda i: (0, i)),
                   pl.BlockSpec((1, ne, LANES), lambda i: (i, 0, 0))],
        out_shape=[jax.ShapeDtypeStruct((n_tok, d), F32),
                   jax.ShapeDtypeStruct((n_tok, d), BF16),
                   jax.ShapeDtypeStruct((TOP_K, n_tok), I32),
                   jax.ShapeDtypeStruct((TOP_K, n_tok), F32),
                   jax.ShapeDtypeStruct((n_tok // tm, ne, LANES), F32)],
        compiler_params=_cparams(("parallel",)),
        name="mix_router",
    )(a, b, x2, wa, wb, norm_w.reshape(1, d), jnp.concatenate([wr_hi, wr_lo], axis=0), router_b.reshape(ne, 1),
      tri, e_lower)


def _powers_of_two_upto(n):
    return tuple(1 << i for i in reversed(range(n.bit_length())))


SLAB_ROWS = TOP_K * ROUTE_TILE + N_EXPERTS * SUBLANES
SLAB_CHUNK = 256
SLAB_PIECES = _powers_of_two_upto(ROUTE_TILE // SUBLANES)
SLAB_COMMON = ROUTE_TILE * TOP_K // N_EXPERTS // SUBLANES


def _for_each_piece(ngroups, fn):
    def pieces(sizes):
        for size in sizes:
            first = lax.bitwise_and(ngroups, ~(2 * size - 1))

            @pl.when(lax.bitwise_and(ngroups, size) != 0)
            def _():
                fn(first, size)

    rare = tuple(s for s in SLAB_PIECES if s >= 2 * SLAB_COMMON)

    @pl.when(ngroups >= min(rare))
    def _():
        pieces(rare)

    pieces(tuple(s for s in SLAB_PIECES if s < 2 * SLAB_COMMON))


def _group_rows(first_group, groups):
    return pl.ds(pl.multiple_of(first_group * SUBLANES, SUBLANES), groups * SUBLANES)


def _wait_groups(ngroups, stage_slot, hbm, sem):
    for size in _powers_of_two_upto(SLAB_ROWS // SUBLANES):
        @pl.when(lax.bitwise_and(ngroups, size) != 0)
        def _():
            pltpu.make_async_copy(hbm.at[_group_rows(0, size)], stage_slot.at[_group_rows(0, size)], sem).wait()


def _dispatch_body(src_ref, dst_ref, ngrp_ref, tot_ref, n2_ref, dloc_ref, xs_hbm, stage, sem):
    step = pl.program_id(0)
    tm = n2_ref.shape[0]
    slot = step % 2

    def shares(at_step, at_slot, act):
        def per_expert(e, carry):
            idx = at_step * N_EXPERTS + e
            src, dst = src_ref[idx], dst_ref[idx]

            def piece(first, size):
                act(pltpu.make_async_copy(stage.at[at_slot, _group_rows(src + first, size)],
                                          xs_hbm.at[_group_rows(dst + first, size)], sem.at[at_slot]))

            _for_each_piece(ngrp_ref[idx], piece)
            return carry

        lax.fori_loop(0, N_EXPERTS, per_expert, 0)

    def wait_slab(at_step, at_slot):
        _wait_groups(tot_ref[at_step], stage.at[at_slot], xs_hbm, sem.at[at_slot])

    @pl.when(step >= 2)
    def _():
        wait_slab(step - 2, slot)

    n2 = n2_ref[...]
    dloc = dloc_ref[...]
    for c in range(SLAB_ROWS // SLAB_CHUNK):
        rows = lax.broadcasted_iota(I32, (SLAB_CHUNK, tm), 0) + c * SLAB_CHUNK
        pick = jnp.zeros((SLAB_CHUNK, tm), F32)
        for k in range(TOP_K):
            pick = jnp.where(dloc[k:k + 1, :] == rows, 1.0, pick)
        stage[slot, c * SLAB_CHUNK:(c + 1) * SLAB_CHUNK, :] = jnp.dot(pick.astype(BF16), n2,
                                                                       preferred_element_type=F32)
    shares(step, slot, lambda cp: cp.start())

    @pl.when(step == pl.num_programs(0) - 1)
    def _():
        wait_slab(step, slot)

        @pl.when(step >= 1)
        def _():
            wait_slab(step - 1, 1 - slot)


def _dispatch(shares, n2_bf, dloc, n_rows):
    n_tok, d = n2_bf.shape
    tm = ROUTE_TILE
    return pl.pallas_call(
        _dispatch_body,
        grid_spec=pltpu.PrefetchScalarGridSpec(
            num_scalar_prefetch=4,
            grid=(n_tok // tm,),
            in_specs=[pl.BlockSpec((tm, d), lambda i, *_: (i, 0)),
                      pl.BlockSpec((TOP_K, tm), lambda i, *_: (0, i))],
            out_specs=pl.BlockSpec(memory_space=pl.ANY),
            scratch_shapes=[pltpu.VMEM((2, SLAB_ROWS, d), F32), pltpu.SemaphoreType.DMA((2,))]),
        out_shape=jax.ShapeDtypeStruct((n_rows, d), F32),
        compiler_params=_cparams(("arbitrary",)),
        name="dispatch",
    )(*shares, n2_bf, dloc)


FFN_STEP_BLOCKS = 2


def _ffn_body(be_ref, first_ref, slot_ref, next_ref, nact_ref, xs_ref, w1_hbm, b1_ref, w2_hbm, b2_ref, ys_ref,
              w1_f, w2_f, w1_b, w2_b, sem):
    first_block = pl.program_id(0) * FFN_STEP_BLOCKS
    n_active = nact_ref[0]

    def fetch(expert, slot):
        return (pltpu.make_async_copy(w1_hbm.at[expert], w1_f.at[slot], sem.at[0, slot]),
                pltpu.make_async_copy(w2_hbm.at[expert], w2_f.at[slot], sem.at[1, slot]))

    def start_run(blk):
        slot = slot_ref[blk]

        @pl.when(blk == 0)
        def _():
            for cp in fetch(be_ref[0], slot):
                cp.start()

        for cp in fetch(be_ref[blk], slot):
            cp.wait()

        @pl.when(next_ref[blk] >= 0)
        def _():
            for cp in fetch(next_ref[blk], 1 - slot):
                cp.start()

    def compute(blk, lo, hi, fresh=False):
        expert = be_ref[blk]
        if fresh:
            slot = slot_ref[blk]
            w1 = w1_f[slot].astype(BF16)
            w2 = w2_f[slot].astype(BF16)
            w1_b[...] = w1
            w2_b[...] = w2
        else:
            w1, w2 = w1_b[...], w2_b[...]
        x = xs_ref[lo:hi, :].astype(BF16)
        hdn = jnp.dot(x, w1, preferred_element_type=F32) + b1_ref[expert]
        d_ff = hdn.shape[1] // 2
        glu = jnp.minimum(hdn[:, :d_ff], SWIGLU_LIMIT)
        lin = jnp.clip(hdn[:, d_ff:], -SWIGLU_LIMIT, SWIGLU_LIMIT)
        act = glu * _sigmoid(SWIGLU_ALPHA * glu) * (lin + 1.0)
        ys_ref[lo:hi, :] = jnp.dot(act.astype(BF16), w2, preferred_element_type=F32) + b2_ref[expert]

    second = first_block + 1

    @pl.when(first_block < n_active)
    def _():
        starts_run = first_ref[first_block] == 1

        @pl.when(starts_run)
        def _():
            start_run(first_block)

        same_run = jnp.logical_and(second < n_active, first_ref[second] == 0)
        for fresh in (False, True):
            matches = starts_run if fresh else jnp.logical_not(starts_run)

            @pl.when(jnp.logical_and(matches, same_run))
            def _():
                compute(first_block, 0, 2 * MOE_BLOCK, fresh)

            @pl.when(jnp.logical_and(matches, jnp.logical_not(same_run)))
            def _():
                compute(first_block, 0, MOE_BLOCK, fresh)

                @pl.when(second < n_active)
                def _():
                    start_run(second)
                    compute(second, MOE_BLOCK, 2 * MOE_BLOCK, True)


def _ffn(block_expert, n_active, xs, w1, b1, w2, b2):
    ne, d, f2 = w1.shape
    nb = xs.shape[0] // MOE_BLOCK
    assert nb % FFN_STEP_BLOCKS == 0
    block = (FFN_STEP_BLOCKS * MOE_BLOCK, d)
    idx = jnp.arange(nb, dtype=I32)
    active = idx < n_active[0]
    prev = jnp.concatenate([block_expert[:1] - 1, block_expert[:-1]])
    first = (active & (block_expert != prev)).astype(I32)
    run_slot = (jnp.sum(jnp.where(idx[None, :] <= idx[:, None], first[None, :], 0), axis=1) - 1) & 1
    later_first = (first[None, :] == 1) & (idx[None, :] > idx[:, None])
    next_block = jnp.min(jnp.where(later_first, idx[None, :], nb), axis=1)
    next_expert = jnp.where(next_block < nb, block_expert[jnp.minimum(next_block, nb - 1)], -1).astype(I32)

    def blk(i, be, fi, sl, nx, nact):
        return (jnp.minimum(i, (nact[0] + FFN_STEP_BLOCKS - 1) // FFN_STEP_BLOCKS - 1), 0)

    whole = lambda shape: pl.BlockSpec(shape, lambda i, *_: (0,) * len(shape))

    return pl.pallas_call(
        _ffn_body,
        grid_spec=pltpu.PrefetchScalarGridSpec(
            num_scalar_prefetch=5,
            grid=(nb // FFN_STEP_BLOCKS,),
            in_specs=[pl.BlockSpec(block, blk),
                      pl.BlockSpec(memory_space=pl.ANY),
                      whole((ne, 1, f2)),
                      pl.BlockSpec(memory_space=pl.ANY),
                      whole((ne, 1, d))],
            out_specs=pl.BlockSpec(block, blk),
            scratch_shapes=[pltpu.VMEM((2, d, f2), F32), pltpu.VMEM((2, f2 // 2, d), F32),
                            pltpu.VMEM((d, f2), BF16), pltpu.VMEM((f2 // 2, d), BF16),
                            pltpu.SemaphoreType.DMA((2, 2))]),
        out_shape=jax.ShapeDtypeStruct(xs.shape, F32),
        compiler_params=_cparams(("arbitrary",)),
        name="expert_ffn",
    )(block_expert, first, run_slot.astype(I32), next_expert, n_active, xs, w1, b1.reshape(ne, 1, f2), w2,
      b2.reshape(ne, 1, d))


def _combine_body(src_ref, dst_ref, ngrp_ref, tot_ref, h_ref, dloc_ref, gate_ref, nw_ref, ys_hbm, o_ref, stage,
                  sem):
    step = pl.program_id(0)
    tm = h_ref.shape[0]
    slot = step % 2

    def shares(at_step, at_slot, act):
        def per_expert(e, carry):
            idx = at_step * N_EXPERTS + e
            src, dst = src_ref[idx], dst_ref[idx]

            def piece(first, size):
                act(pltpu.make_async_copy(ys_hbm.at[_group_rows(dst + first, size)],
                                          stage.at[at_slot, _group_rows(src + first, size)], sem.at[at_slot]))

            _for_each_piece(ngrp_ref[idx], piece)
            return carry

        lax.fori_loop(0, N_EXPERTS, per_expert, 0)

    @pl.when(step == 0)
    def _():
        stage[...] = jnp.zeros(stage.shape, F32)
        shares(0, 0, lambda cp: cp.start())

    @pl.when(step + 1 < pl.num_programs(0))
    def _():
        shares(step + 1, 1 - slot, lambda cp: cp.start())

    _wait_groups(tot_ref[step], stage.at[slot], ys_hbm, sem.at[slot])
    dloc = dloc_ref[...]
    gates = gate_ref[...]
    moe = jnp.zeros(o_ref.shape, F32)
    for c in range(SLAB_ROWS // SLAB_CHUNK):
        srow = lax.broadcasted_iota(I32, (SLAB_CHUNK, tm), 0) + c * SLAB_CHUNK
        weight_t = jnp.zeros((SLAB_CHUNK, tm), F32)
        for k in range(TOP_K):
            weight_t = jnp.where(dloc[k:k + 1, :] == srow, gates[k:k + 1, :], weight_t)
        rows = stage[slot, c * SLAB_CHUNK:(c + 1) * SLAB_CHUNK, :].astype(BF16)
        moe = moe + lax.dot_general(weight_t.astype(BF16), rows, (((0,), (0,)), ((), ())),
                                    preferred_element_type=F32)
    y = h_ref[...] + moe
    o_ref[...] = y * lax.rsqrt(jnp.mean(y * y, axis=-1, keepdims=True) + EPS) * nw_ref[...]


def _combine(shares, h, dloc_tk, gates_tk, norm_w, ys):
    n_tok, d = h.shape
    tm = ROUTE_TILE
    return pl.pallas_call(
        _combine_body,
        grid_spec=pltpu.PrefetchScalarGridSpec(
            num_scalar_prefetch=4,
            grid=(n_tok // tm,),
            in_specs=[pl.BlockSpec((tm, d), lambda i, *_: (i, 0)),
                      pl.BlockSpec((TOP_K, tm), lambda i, *_: (0, i)),
                      pl.BlockSpec((TOP_K, tm), lambda i, *_: (0, i)),
                      pl.BlockSpec((1, d), lambda i, *_: (0, 0)),
                      pl.BlockSpec(memory_space=pl.ANY)],
            out_specs=pl.BlockSpec((tm, d), lambda i, *_: (i, 0)),
            scratch_shapes=[pltpu.VMEM((2, SLAB_ROWS, d), F32), pltpu.SemaphoreType.DMA((2,))]),
        out_shape=jax.ShapeDtypeStruct((n_tok, d), F32),
        compiler_params=_cparams(("arbitrary",)),
        name="combine_norm",
    )(*shares, h, dloc_tk, gates_tk, norm_w.reshape(1, d), ys)


def kernel(x, norm1_w, w_in, lb_logits, hgrn_norm_w, dw_w, dw_b, conv_ln_w, conv_ln_b, w_out, norm2_w,
           router_w, router_b, w1, b1, w2, b2, final_norm_w):
    batch, t_len, d = x.shape
    assert w_in.shape[0] == 1, "single-layer block"
    n_tok = batch * t_len
    hk = HGRN_HEADS * HEAD_DIM
    conv_ch = dw_w.shape[2]
    lb_table = jnp.cumsum(jax.nn.softmax(lb_logits.astype(F32), axis=1), axis=1)
    x2 = x.reshape(n_tok, d)
    proj = _in_proj(x2, norm1_w[0], w_in[0].astype(BF16))
    a = _hgrn(proj, lb_table[0, 0], lb_table[1, 0], hgrn_norm_w[0], batch, t_len)
    b = _conv(proj, dw_w[0], dw_b[0], conv_ln_w[0], conv_ln_b[0], batch, t_len, (5 * hk) // conv_ch)
    h_mid, n2_bf, dloc, gate, cnt_tiles = _mix(a, b, x2, w_out[0].astype(BF16), norm2_w[0],
                                                     router_w[0], router_b[0])
    nt = n_tok // ROUTE_TILE
    e_ids = jnp.arange(N_EXPERTS, dtype=I32)
    t_ids = jnp.arange(nt, dtype=I32)
    share = (cnt_tiles[:, :, 0].astype(I32) + SUBLANES - 1) // SUBLANES * SUBLANES
    in_slab = jnp.sum(jnp.where(e_ids[None, None, :] < e_ids[None, :, None], share[:, None, :], 0), axis=2)
    in_expert = jnp.sum(jnp.where((t_ids[None, :] < t_ids[:, None])[:, :, None], share[None, :, :], 0), axis=1)
    padded = (jnp.sum(share, axis=0) + MOE_BLOCK - 1) // MOE_BLOCK * MOE_BLOCK
    pad_end = jnp.sum(jnp.where(e_ids[None, :] <= e_ids[:, None], padded[None, :], 0), axis=1)
    in_xs = (pad_end - padded)[None, :] + in_expert
    n_blocks = -(-(n_tok * TOP_K + nt * N_EXPERTS * (SUBLANES - 1)) // MOE_BLOCK) + N_EXPERTS
    n_blocks += -n_blocks % FFN_STEP_BLOCKS
    block_start = jnp.arange(n_blocks, dtype=I32) * MOE_BLOCK
    block_expert = jnp.minimum(jnp.sum((pad_end[None, :] <= block_start[:, None]).astype(I32), axis=1),
                               N_EXPERTS - 1)
    n_active = pad_end[-1:] // MOE_BLOCK
    shares = [(v // SUBLANES).reshape(-1) for v in (in_slab, in_xs, share)] + [jnp.sum(share, axis=1) // SUBLANES]
    xs = _dispatch(shares, n2_bf, dloc, n_blocks * MOE_BLOCK)
    ys = _ffn(block_expert, n_active, xs, w1[0], b1[0], w2[0], b2[0])
    out = _combine(shares, h_mid, dloc, gate, final_norm_w, ys)
    return out.reshape(batch, t_len, d)
```

```python
import functools
import math

import numpy as np
import jax
import jax.numpy as jnp
from jax import lax
from jax.experimental import pallas as pl
from jax.experimental.pallas import tpu as pltpu

F32 = jnp.float32
BF16 = jnp.bfloat16
I32 = jnp.int32

EPS = 1e-5
HGRN_HEADS = 4
HEAD_DIM = 128
HGRN_CHUNK = 256
CONV_LEN = 31
N_EXPERTS = 32
TOP_K = 4
SWIGLU_LIMIT = 7.0
SWIGLU_ALPHA = 1.702
MOE_BLOCK = 256
ROUTE_TILE = 512
LANES = 128
SUBLANES = 8
VMEM_LIMIT = 56 << 20

_NT = (((1,), (1,)), ((), ()))


def _sigmoid(x):
    return 0.5 * jnp.tanh(0.5 * x) + 0.5


def _silu(x):
    return x * _sigmoid(x)


def _cparams(sem):
    return pltpu.CompilerParams(dimension_semantics=sem, vmem_limit_bytes=VMEM_LIMIT)


def _inproj_body(x_ref, nw_ref, w_ref, o_ref):
    x = x_ref[...]
    n = x * lax.rsqrt(jnp.mean(x * x, axis=-1, keepdims=True) + EPS) * nw_ref[...]
    o_ref[...] = jnp.dot(n.astype(BF16), w_ref[...].astype(BF16), preferred_element_type=F32)


def _in_proj(x2, norm_w, w_bf, tm=512):
    n_tok, d = x2.shape
    cols = w_bf.shape[1]
    return pl.pallas_call(
        _inproj_body,
        grid=(n_tok // tm,),
        in_specs=[pl.BlockSpec((tm, d), lambda i: (i, 0)),
                  pl.BlockSpec((1, d), lambda i: (0, 0)),
                  pl.BlockSpec((d, cols), lambda i: (0, 0))],
        out_specs=pl.BlockSpec((tm, cols), lambda i: (i, 0)),
        out_shape=jax.ShapeDtypeStruct((n_tok, cols), F32),
        compiler_params=_cparams(("parallel",)),
        name="in_proj",
    )(x2, norm_w.reshape(1, d), w_bf)


def _hgrn_levels(c):
    return [c >> (i + 1) for i in range(int(math.log2(c)))]


def _hgrn_sum_matrices(c, fwd):
    r = np.arange(c)
    m = _HGRN_SMALL_LEVEL
    small = np.zeros((c, c), np.float32)
    for t in range(c):
        p0 = t & ~(2 * m - 1)
        upper = (t & m) != 0
        if fwd:
            if upper:
                small[t, p0 + m:t + 1] = 1.0
            else:
                small[t, t + 1:p0 + m] = 1.0
        else:
            if upper:
                small[t, p0 + m:t] = 1.0
            else:
                small[t, t:p0 + m] = 1.0
    mats = [r[None, :] <= r[:, None] if fwd else r[None, :] >= r[:, None], small]
    return np.concatenate([np.asarray(x, np.float32) for x in mats], axis=0)


_HGRN_SMALL_LEVEL = 2
_HGRN_ROW_SLICE = 16


def _hgrn_chunk(q, k, lf, v_bf, vt_bf, st, sums_ref, fwd, c):
    dk = q.shape[1]
    hi = lf.astype(BF16)
    lo = (lf - hi.astype(F32)).astype(BF16)
    both = jnp.dot(sums_ref[...], jnp.concatenate([hi, lo], axis=1), preferred_element_type=F32)
    sums = both[:, :dk] + both[:, dk:]
    b = sums[0:c]
    edge_row = c - 1 if fwd else 0
    d_out = b[edge_row:edge_row + 1] - b
    row = lax.broadcasted_iota(I32, (c, dk), 0)
    rr = lax.broadcasted_iota(I32, (c, c), 0)
    cc = lax.broadcasted_iota(I32, (c, c), 1)
    sep = rr ^ cc
    a = None
    for m in _hgrn_levels(c):
        upper = (row & m) != 0
        is_q = upper if fwd else jnp.logical_not(upper)
        if m == 1:
            d = jnp.where(is_q, lf, 0.0)
        elif m == _HGRN_SMALL_LEVEL:
            d = sums[c:2 * c]
        else:
            blocks = c // (2 * m)
            at = m - 1 if fwd else m
            edge = b.reshape(blocks, 2 * m, dk)[:, at:at + 1, :]
            edge = jnp.broadcast_to(edge, (blocks, 2 * m, dk)).reshape(c, dk)
            d = jnp.where(is_q, b - edge, edge - b)
        r = (jnp.where(is_q, q, k) * jnp.exp2(d)).astype(BF16)
        if m < _HGRN_ROW_SLICE:
            a_l = lax.dot_general(r, r, _NT, preferred_element_type=F32)
            a = a_l if a is None else jnp.where(sep < 2 * m, a_l, a)
        else:
            halves = [(p0 + m, p0 + 2 * m) if fwd else (p0, p0 + m) for p0 in range(0, c, 2 * m)]
            a_q = lax.dot_general(jnp.concatenate([r[lo:hi] for lo, hi in halves], axis=0), r, _NT,
                                  preferred_element_type=F32)
            parts = []
            for n, (lo, hi) in enumerate(halves):
                new = a_q[n * m:(n + 1) * m]
                if a is None:
                    kept = jnp.zeros((m, c), F32)
                else:
                    new = jnp.where(sep[lo:hi] < 2 * m, new, a[lo:hi])
                    kept = a[lo - m:lo] if fwd else a[hi:hi + m]
                parts += [kept, new] if fwd else [new, kept]
            a = jnp.concatenate(parts, axis=0)
    a_diag = lax.dot_general(q.astype(BF16), k.astype(BF16), _NT, preferred_element_type=F32)
    a = jnp.where(sep < 1, a_diag, a)
    a = jnp.where(rr >= cc if fwd else rr <= cc, a, 0.0)
    e_b = jnp.exp2(b)
    q_in = (q * e_b).astype(BF16)
    k_out = (k * jnp.exp2(d_out)).astype(BF16)
    o = jnp.dot(a.astype(BF16), v_bf, preferred_element_type=F32)
    o = o + lax.dot_general(q_in, st.astype(BF16), _NT, preferred_element_type=F32)
    st_new = st * e_b[edge_row:edge_row + 1] + jnp.dot(vt_bf, k_out, preferred_element_type=F32)
    return o, st_new


def _hgrn_body(qz_ref, ff_ref, fb_ref, iv_ref, gz_ref, lbf_ref, lbb_ref, nw_ref, sf_ref, sb_ref, o_ref,
               q_s, kf_s, lff_s, kb_s, lfb_s, v_s, vt_s, acc_s, *, c):
    t_len, dk = qz_ref.shape
    nch = t_len // c
    q_s[...] = _silu(qz_ref[...])

    def gate(fz_ref, lb_ref, k_s, lf_s):
        lb = lb_ref[0]
        f = lb + (1.0 - lb) * _sigmoid(fz_ref[...])
        lf_s[...] = jnp.log2(f)
        k_s[...] = 1.0 - f

    gate(ff_ref, lbf_ref, kf_s, lff_s)
    gate(fb_ref, lbb_ref, kb_s, lfb_s)
    v = iv_ref[...]
    v_s[...] = v.astype(BF16)
    for ci in range(nch):
        vt_s[ci] = v[ci * c:(ci + 1) * c, :].T.astype(BF16)

    def run(k_s, lf_s, sums_ref, fwd):
        def step(i, st):
            ci = i if fwd else nch - 1 - i
            rows = pl.ds(pl.multiple_of(ci * c, c), c)
            o, st = _hgrn_chunk(q_s[rows, :], k_s[rows, :], lf_s[rows, :], v_s[rows, :], vt_s[ci],
                                st, sums_ref, fwd, c)
            if fwd:
                acc_s[rows, :] = o
            else:
                acc_s[rows, :] += o
            return st
        lax.fori_loop(0, nch, step, jnp.zeros((dk, dk), F32), unroll=True)

    run(kf_s, lff_s, sf_ref, True)
    run(kb_s, lfb_s, sb_ref, False)
    o = acc_s[...]
    o = o * lax.rsqrt(jnp.mean(o * o, axis=-1, keepdims=True) + EPS) * nw_ref[0]
    o_ref[...] = (o * _silu(gz_ref[...])).astype(o_ref.dtype)


def _hgrn(proj, lb_f, lb_b, norm_w, batch, t_len):
    h, dk, c = HGRN_HEADS, HEAD_DIM, HGRN_CHUNK
    nrows = 2
    sums_f = jnp.asarray(_hgrn_sum_matrices(c, True), BF16)
    sums_b = jnp.asarray(_hgrn_sum_matrices(c, False), BF16)

    def col(group):
        return pl.BlockSpec((t_len, dk), lambda b, hh: (b, group * h + hh))

    def per_head():
        return pl.BlockSpec((1, 1, dk), lambda b, hh: (hh, 0, 0))

    const = pl.BlockSpec((nrows * c, c), lambda b, hh: (0, 0))
    seq = lambda dt: pltpu.VMEM((t_len, dk), dt)
    return pl.pallas_call(
        functools.partial(_hgrn_body, c=c),
        grid=(batch, h),
        in_specs=[col(0), col(1), col(2), col(3), col(4), per_head(), per_head(), per_head(), const, const],
        out_specs=pl.BlockSpec((t_len, dk), lambda b, hh: (b, hh)),
        out_shape=jax.ShapeDtypeStruct((batch * t_len, h * dk), BF16),
        scratch_shapes=[seq(F32), seq(F32), seq(F32), seq(F32), seq(F32), seq(BF16),
                        pltpu.VMEM((t_len // c, dk, c), BF16), seq(F32)],
        compiler_params=_cparams(("parallel", "parallel")),
        name="hgrn",
    )(proj, proj, proj, proj, proj, lb_f.reshape(h, 1, dk), lb_b.reshape(h, 1, dk),
      norm_w.reshape(h, 1, dk), sums_f, sums_b)


CONV_ROWS = 64
CONV_HALO = 16


def _conv_body(cv_ref, cg_ref, w_ref, b_ref, lnw_ref, lnb_ref, o_ref, u_s, y_s):
    t_len, ch = cv_ref.shape
    halo, rows = CONV_HALO, CONV_ROWS
    shift0 = halo - (CONV_LEN - 1) // 2
    win = rows + 2 * halo
    for g in range(ch // LANES):
        lanes = slice(g * LANES, (g + 1) * LANES)
        u_s[g, 0:halo, :] = jnp.zeros((halo, LANES), F32)
        u_s[g, halo + t_len:, :] = jnp.zeros((halo, LANES), F32)
        u_s[g, halo:halo + t_len, :] = cv_ref[:, lanes] * _sigmoid(cg_ref[:, lanes])

    def conv_step(i, carry):
        t0 = pl.multiple_of(i * rows, rows)
        for g in range(ch // LANES):
            lanes = slice(g * LANES, (g + 1) * LANES)
            window = u_s.at[g, pl.ds(t0, win), :]
            acc = jnp.zeros((rows, LANES), F32)
            for j in range(CONV_LEN):
                off = j + shift0
                acc = acc + w_ref[j:j + 1, lanes] * window[off:off + rows, :]
            y_s[pl.ds(t0, rows), lanes] = acc + b_ref[:, lanes]
        return carry

    lax.fori_loop(0, t_len // rows, conv_step, 0)

    def step(i, carry):
        t0 = pl.multiple_of(i * rows, rows)
        y = y_s[pl.ds(t0, rows), :]
        mu = jnp.mean(y, axis=-1, keepdims=True)
        yc = y - mu
        var = jnp.mean(yc * yc, axis=-1, keepdims=True)
        z = yc * lax.rsqrt(var + EPS) * lnw_ref[...] + lnb_ref[...]
        o_ref[pl.ds(t0, rows), :] = _silu(z).astype(o_ref.dtype)
        return carry

    lax.fori_loop(0, t_len // rows, step, 0, unroll=4)


def _conv(proj, dw_w, dw_b, ln_w, ln_b, batch, t_len, first_col_block):
    ch = dw_w.shape[1]
    vec = pl.BlockSpec((1, ch), lambda b: (0, 0))
    return pl.pallas_call(
        _conv_body,
        grid=(batch,),
        in_specs=[pl.BlockSpec((t_len, ch), lambda b: (b, first_col_block)),
                  pl.BlockSpec((t_len, ch), lambda b: (b, first_col_block + 1)),
                  pl.BlockSpec((CONV_LEN, ch), lambda b: (0, 0)), vec, vec, vec],
        out_specs=pl.BlockSpec((t_len, ch), lambda b: (b, 0)),
        out_shape=jax.ShapeDtypeStruct((batch * t_len, ch), BF16),
        scratch_shapes=[pltpu.VMEM((ch // LANES, t_len + 2 * CONV_HALO, LANES), F32),
                        pltpu.VMEM((t_len, ch), F32)],
        compiler_params=_cparams(("parallel",)),
        name="conv",
    )(proj, proj, dw_w, dw_b.reshape(1, ch), ln_w.reshape(1, ch), ln_b.reshape(1, ch))


def _mix_body(a_ref, b_ref, x_ref, wa_ref, wb_ref, nw_ref, wr_ref, rb_ref, tri_ref, elow_ref,
              h_ref, n2_ref, dloc_ref, gate_ref, cnt_ref):
    tm = x_ref.shape[0]
    ne = wr_ref.shape[0] // 2

    h = x_ref[...] + jnp.dot(a_ref[...], wa_ref[...].astype(BF16), preferred_element_type=F32) \
        + jnp.dot(b_ref[...], wb_ref[...].astype(BF16), preferred_element_type=F32)
    h_ref[...] = h
    n2 = h * lax.rsqrt(jnp.mean(h * h, axis=-1, keepdims=True) + EPS) * nw_ref[...]
    hi = n2.astype(BF16)
    n2_ref[...] = hi
    lo = (n2 - hi.astype(F32)).astype(BF16)
    by_hi = lax.dot_general(wr_ref[...], hi, _NT, preferred_element_type=F32)
    by_lo = lax.dot_general(wr_ref[0:ne, :], lo, _NT, preferred_element_type=F32)
    logits = by_hi[0:ne] + by_hi[ne:] + by_lo + rb_ref[...]
    e_iota = lax.broadcasted_iota(I32, (ne, tm), 0)
    work = logits
    sels, vals = [], []
    for _ in range(TOP_K):
        mx = jnp.max(work, axis=0, keepdims=True)
        idx = jnp.min(jnp.where(work == mx, e_iota, ne), axis=0, keepdims=True)
        sel = e_iota == idx
        work = jnp.where(sel, -jnp.inf, work)
        sels.append(sel)
        vals.append(mx)
    exps = [jnp.exp(v - vals[0]) for v in vals]
    denom = exps[0] + exps[1] + exps[2] + exps[3]
    chosen = jnp.zeros((ne, tm), F32)
    for sel in sels:
        chosen = jnp.where(sel, 1.0, chosen)
    chosen_bf = chosen.astype(BF16)
    prior = jnp.dot(chosen_bf, tri_ref[...], preferred_element_type=F32)
    cnt = jnp.dot(chosen_bf, jnp.ones((tm, LANES), BF16), preferred_element_type=F32)
    cnt_ref[0] = cnt
    share = jnp.floor((cnt + (SUBLANES - 1)) * (1.0 / SUBLANES)) * SUBLANES
    lower = jnp.dot(elow_ref[...], share.astype(BF16), preferred_element_type=F32)
    slab_row = prior + jnp.concatenate([lower] * (tm // LANES), axis=1)
    for k in range(TOP_K):
        dloc_ref[k:k + 1, :] = jnp.sum(jnp.where(sels[k], slab_row, 0.0), axis=0, keepdims=True).astype(I32)
        gate_ref[k:k + 1, :] = exps[k] / denom


def _mix(a, b, x2, w_out_bf, norm_w, router_w, router_b, tm=ROUTE_TILE):
    n_tok, d = x2.shape
    wa, wb = w_out_bf[:a.shape[1]], w_out_bf[a.shape[1]:]
    ne = router_w.shape[1]
    wr_t = router_w.T
    wr_hi = wr_t.astype(BF16)
    wr_lo = (wr_t - wr_hi.astype(F32)).astype(BF16)
    tri = jnp.asarray(np.triu(np.ones((tm, tm), np.float32), k=1), BF16)
    e_lower = jnp.asarray(np.tril(np.ones((ne, ne), np.float32), k=-1), BF16)
    const = lambda shape: pl.BlockSpec(shape, lambda i: tuple(0 for _ in shape))
    return pl.pallas_call(
        _mix_body,
        grid=(n_tok // tm,),
        in_specs=[pl.BlockSpec((tm, a.shape[1]), lambda i: (i, 0)),
                  pl.BlockSpec((tm, b.shape[1]), lambda i: (i, 0)),
                  pl.BlockSpec((tm, d), lambda i: (i, 0)),
                  const(wa.shape), const(wb.shape), const((1, d)),
                  const((2 * ne, d)), const((ne, 1)), const((tm, tm)), const((ne, ne))],
        out_specs=[pl.BlockSpec((tm, d), lambda i: (i, 0)),
                   pl.BlockSpec((tm, d), lambda i: (i, 0)),
                   pl.BlockSpec((TOP_K, tm), lambda i: (0, i)),
                   pl.BlockSpec((TOP_K, tm), lambda i: (0, i)),
                   pl.BlockSpec((1, ne, LANES), lambda i: (i, 0, 0))],
        out_shape=[jax.ShapeDtypeStruct((n_tok, d), F32),
                   jax.ShapeDtypeStruct((n_tok, d), BF16),
                   jax.ShapeDtypeStruct((TOP_K, n_tok), I32),
                   jax.ShapeDtypeStruct((TOP_K, n_tok), F32),
                   jax.ShapeDtypeStruct((n_tok // tm, ne, LANES), F32)],
        compiler_params=_cparams(("parallel",)),
        name="mix_router",
    )(a, b, x2, wa, wb, norm_w.reshape(1, d), jnp.concatenate([wr_hi, wr_lo], axis=0), router_b.reshape(ne, 1),
      tri, e_lower)


def _powers_of_two_upto(n):
    return tuple(1 << i for i in reversed(range(n.bit_length())))


SLAB_ROWS = TOP_K * ROUTE_TILE + N_EXPERTS * SUBLANES
SLAB_CHUNK = 256
SLAB_PIECES = _powers_of_two_upto(ROUTE_TILE // SUBLANES)
SLAB_COMMON = ROUTE_TILE * TOP_K // N_EXPERTS // SUBLANES


def _for_each_piece(ngroups, fn):
    def pieces(sizes):
        for size in sizes:
            first = lax.bitwise_and(ngroups, ~(2 * size - 1))

            @pl.when(lax.bitwise_and(ngroups, size) != 0)
            def _():
                fn(first, size)

    rare = tuple(s for s in SLAB_PIECES if s >= 2 * SLAB_COMMON)

    @pl.when(ngroups >= min(rare))
    def _():
        pieces(rare)

    pieces(tuple(s for s in SLAB_PIECES if s < 2 * SLAB_COMMON))


def _group_rows(first_group, groups):
    return pl.ds(pl.multiple_of(first_group * SUBLANES, SUBLANES), groups * SUBLANES)


def _wait_groups(ngroups, stage_slot, hbm, sem):
    for size in _powers_of_two_upto(SLAB_ROWS // SUBLANES):
        @pl.when(lax.bitwise_and(ngroups, size) != 0)
        def _():
            pltpu.make_async_copy(hbm.at[_group_rows(0, size)], stage_slot.at[_group_rows(0, size)], sem).wait()


def _dispatch_body(src_ref, dst_ref, ngrp_ref, tot_ref, n2_ref, dloc_ref, xs_hbm, stage, sem):
    step = pl.program_id(0)
    tm = n2_ref.shape[0]
    slot = step % 2

    def shares(at_step, at_slot, act):
        def per_expert(e, carry):
            idx = at_step * N_EXPERTS + e
            src, dst = src_ref[idx], dst_ref[idx]

            def piece(first, size):
                act(pltpu.make_async_copy(stage.at[at_slot, _group_rows(src + first, size)],
                                          xs_hbm.at[_group_rows(dst + first, size)], sem.at[at_slot]))

            _for_each_piece(ngrp_ref[idx], piece)
            return carry

        lax.fori_loop(0, N_EXPERTS, per_expert, 0)

    def wait_slab(at_step, at_slot):
        _wait_groups(tot_ref[at_step], stage.at[at_slot], xs_hbm, sem.at[at_slot])

    @pl.when(step >= 2)
    def _():
        wait_slab(step - 2, slot)

    n2 = n2_ref[...]
    dloc = dloc_ref[...]
    for c in range(SLAB_ROWS // SLAB_CHUNK):
        rows = lax.broadcasted_iota(I32, (SLAB_CHUNK, tm), 0) + c * SLAB_CHUNK
        pick = jnp.zeros((SLAB_CHUNK, tm), F32)
        for k in range(TOP_K):
            pick = jnp.where(dloc[k:k + 1, :] == rows, 1.0, pick)
        stage[slot, c * SLAB_CHUNK:(c + 1) * SLAB_CHUNK, :] = jnp.dot(pick.astype(BF16), n2,
                                                                       preferred_element_type=F32)
    shares(step, slot, lambda cp: cp.start())

    @pl.when(step == pl.num_programs(0) - 1)
    def _():
        wait_slab(step, slot)

        @pl.when(step >= 1)
        def _():
            wait_slab(step - 1, 1 - slot)


def _dispatch(shares, n2_bf, dloc, n_rows):
    n_tok, d = n2_bf.shape
    tm = ROUTE_TILE
    return pl.pallas_call(
        _dispatch_body,
        grid_spec=pltpu.PrefetchScalarGridSpec(
            num_scalar_prefetch=4,
            grid=(n_tok // tm,),
            in_specs=[pl.BlockSpec((tm, d), lambda i, *_: (i, 0)),
                      pl.BlockSpec((TOP_K, tm), lambda i, *_: (0, i))],
            out_specs=pl.BlockSpec(memory_space=pl.ANY),
            scratch_shapes=[pltpu.VMEM((2, SLAB_ROWS, d), F32), pltpu.SemaphoreType.DMA((2,))]),
        out_shape=jax.ShapeDtypeStruct((n_rows, d), F32),
        compiler_params=_cparams(("arbitrary",)),
        name="dispatch",
    )(*shares, n2_bf, dloc)


FFN_STEP_BLOCKS = 2


def _ffn_body(be_ref, first_ref, slot_ref, next_ref, nact_ref, xs_ref, w1_hbm, b1_ref, w2_hbm, b2_ref, ys_ref,
              w1_f, w2_f, w1_b, w2_b, sem):
    first_block = pl.program_id(0) * FFN_STEP_BLOCKS
    n_active = nact_ref[0]

    def fetch(expert, slot):
        return (pltpu.make_async_copy(w1_hbm.at[expert], w1_f.at[slot], sem.at[0, slot]),
                pltpu.make_async_copy(w2_hbm.at[expert], w2_f.at[slot], sem.at[1, slot]))

    def start_run(blk):
        slot = slot_ref[blk]

        @pl.when(blk == 0)
        def _():
            for cp in fetch(be_ref[0], slot):
                cp.start()

        for cp in fetch(be_ref[blk], slot):
            cp.wait()

        @pl.when(next_ref[blk] >= 0)
        def _():
            for cp in fetch(next_ref[blk], 1 - slot):
                cp.start()

    def compute(blk, lo, hi, fresh=False):
        expert = be_ref[blk]
        if fresh:
            slot = slot_ref[blk]
            w1 = w1_f[slot].astype(BF16)
            w2 = w2_f[slot].astype(BF16)
            w1_b[...] = w1
            w2_b[...] = w2
        else:
            w1, w2 = w1_b[...], w2_b[...]
        x = xs_ref[lo:hi, :].astype(BF16)
        hdn = jnp.dot(x, w1, preferred_element_type=F32) + b1_ref[expert]
        d_ff = hdn.shape[1] // 2
        glu = jnp.minimum(hdn[:, :d_ff], SWIGLU_LIMIT)
        lin = jnp.clip(hdn[:, d_ff:], -SWIGLU_LIMIT, SWIGLU_LIMIT)
        act = glu * _sigmoid(SWIGLU_ALPHA * glu) * (lin + 1.0)
        ys_ref[lo:hi, :] = jnp.dot(act.astype(BF16), w2, preferred_element_type=F32) + b2_ref[expert]

    second = first_block + 1

    @pl.when(first_block < n_active)
    def _():
        starts_run = first_ref[first_block] == 1

        @pl.when(starts_run)
        def _():
            start_run(first_block)

        same_run = jnp.logical_and(second < n_active, first_ref[second] == 0)
        for fresh in (False, True):
            matches = starts_run if fresh else jnp.logical_not(starts_run)

            @pl.when(jnp.logical_and(matches, same_run))
            def _():
                compute(first_block, 0, 2 * MOE_BLOCK, fresh)

            @pl.when(jnp.logical_and(matches, jnp.logical_not(same_run)))
            def _():
                compute(first_block, 0, MOE_BLOCK, fresh)

                @pl.when(second < n_active)
                def _():
                    start_run(second)
                    compute(second, MOE_BLOCK, 2 * MOE_BLOCK, True)


def _ffn(block_expert, n_active, xs, w1, b1, w2, b2):
    ne, d, f2 = w1.shape
    nb = xs.shape[0] // MOE_BLOCK
    assert nb % FFN_STEP_BLOCKS == 0
    block = (FFN_STEP_BLOCKS * MOE_BLOCK, d)
    idx = jnp.arange(nb, dtype=I32)
    active = idx < n_active[0]
    prev = jnp.concatenate([block_expert[:1] - 1, block_expert[:-1]])
    first = (active & (block_expert != prev)).astype(I32)
    run_slot = (jnp.sum(jnp.where(idx[None, :] <= idx[:, None], first[None, :], 0), axis=1) - 1) & 1
    later_first = (first[None, :] == 1) & (idx[None, :] > idx[:, None])
    next_block = jnp.min(jnp.where(later_first, idx[None, :], nb), axis=1)
    next_expert = jnp.where(next_block < nb, block_expert[jnp.minimum(next_block, nb - 1)], -1).astype(I32)

    def blk(i, be, fi, sl, nx, nact):
        return (jnp.minimum(i, (nact[0] + FFN_STEP_BLOCKS - 1) // FFN_STEP_BLOCKS - 1), 0)

    whole = lambda shape: pl.BlockSpec(shape, lambda i, *_: (0,) * len(shape))

    return pl.pallas_call(
        _ffn_body,
        grid_spec=pltpu.PrefetchScalarGridSpec(
            num_scalar_prefetch=5,
            grid=(nb // FFN_STEP_BLOCKS,),
            in_specs=[pl.BlockSpec(block, blk),
                      pl.BlockSpec(memory_space=pl.ANY),
                      whole((ne, 1, f2)),
                      pl.BlockSpec(memory_space=pl.ANY),
                      whole((ne, 1, d))],
            out_specs=pl.BlockSpec(block, blk),
            scratch_shapes=[pltpu.VMEM((2, d, f2), F32), pltpu.VMEM((2, f2 // 2, d), F32),
                            pltpu.VMEM((d, f2), BF16), pltpu.VMEM((f2 // 2, d), BF16),
                            pltpu.SemaphoreType.DMA((2, 2))]),
        out_shape=jax.ShapeDtypeStruct(xs.shape, F32),
        compiler_params=_cparams(("arbitrary",)),
        name="expert_ffn",
    )(block_expert, first, run_slot.astype(I32), next_expert, n_active, xs, w1, b1.reshape(ne, 1, f2), w2,
      b2.reshape(ne, 1, d))


def _combine_body(src_ref, dst_ref, ngrp_ref, tot_ref, h_ref, dloc_ref, gate_ref, nw_ref, ys_hbm, o_ref, stage,
                  sem):
    step = pl.program_id(0)
    tm = h_ref.shape[0]
    slot = step % 2

    def shares(at_step, at_slot, act):
        def per_expert(e, carry):
            idx = at_step * N_EXPERTS + e
            src, dst = src_ref[idx], dst_ref[idx]

            def piece(first, size):
                act(pltpu.make_async_copy(ys_hbm.at[_group_rows(dst + first, size)],
                                          stage.at[at_slot, _group_rows(src + first, size)], sem.at[at_slot]))

            _for_each_piece(ngrp_ref[idx], piece)
            return carry

        lax.fori_loop(0, N_EXPERTS, per_expert, 0)

    @pl.when(step == 0)
    def _():
        stage[...] = jnp.zeros(stage.shape, F32)
        shares(0, 0, lambda cp: cp.start())

    @pl.when(step + 1 < pl.num_programs(0))
    def _():
        shares(step + 1, 1 - slot, lambda cp: cp.start())

    _wait_groups(tot_ref[step], stage.at[slot], ys_hbm, sem.at[slot])
    dloc = dloc_ref[...]
    gates = gate_ref[...]
    moe = jnp.zeros(o_ref.shape, F32)
    for c in range(SLAB_ROWS // SLAB_CHUNK):
        srow = lax.broadcasted_iota(I32, (SLAB_CHUNK, tm), 0) + c * SLAB_CHUNK
        weight_t = jnp.zeros((SLAB_CHUNK, tm), F32)
        for k in range(TOP_K):
            weight_t = jnp.where(dloc[k:k + 1, :] == srow, gates[k:k + 1, :], weight_t)
        rows = stage[slot, c * SLAB_CHUNK:(c + 1) * SLAB_CHUNK, :].astype(BF16)
        moe = moe + lax.dot_general(weight_t.astype(BF16), rows, (((0,), (0,)), ((), ())),
                                    preferred_element_type=F32)
    y = h_ref[...] + moe
    o_ref[...] = y * lax.rsqrt(jnp.mean(y * y, axis=-1, keepdims=True) + EPS) * nw_ref[...]


def _combine(shares, h, dloc_tk, gates_tk, norm_w, ys):
    n_tok, d = h.shape
    tm = ROUTE_TILE
    return pl.pallas_call(
        _combine_body,
        grid_spec=pltpu.PrefetchScalarGridSpec(
            num_scalar_prefetch=4,
            grid=(n_tok // tm,),
            in_specs=[pl.BlockSpec((tm, d), lambda i, *_: (i, 0)),
                      pl.BlockSpec((TOP_K, tm), lambda i, *_: (0, i)),
                      pl.BlockSpec((TOP_K, tm), lambda i, *_: (0, i)),
                      pl.BlockSpec((1, d), lambda i, *_: (0, 0)),
                      pl.BlockSpec(memory_space=pl.ANY)],
            out_specs=pl.BlockSpec((tm, d), lambda i, *_: (i, 0)),
            scratch_shapes=[pltpu.VMEM((2, SLAB_ROWS, d), F32), pltpu.SemaphoreType.DMA((2,))]),
        out_shape=jax.ShapeDtypeStruct((n_tok, d), F32),
        compiler_params=_cparams(("arbitrary",)),
        name="combine_norm",
    )(*shares, h, dloc_tk, gates_tk, norm_w.reshape(1, d), ys)


def kernel(x, norm1_w, w_in, lb_logits, hgrn_norm_w, dw_w, dw_b, conv_ln_w, conv_ln_b, w_out, norm2_w,
           router_w, router_b, w1, b1, w2, b2, final_norm_w):
    batch, t_len, d = x.shape
    assert w_in.shape[0] == 1, "single-layer block"
    n_tok = batch * t_len
    hk = HGRN_HEADS * HEAD_DIM
    conv_ch = dw_w.shape[2]
    lb_table = jnp.cumsum(jax.nn.softmax(lb_logits.astype(F32), axis=1), axis=1)
    x2 = x.reshape(n_tok, d)
    proj = _in_proj(x2, norm1_w[0], w_in[0])
    a = _hgrn(proj, lb_table[0, 0], lb_table[1, 0], hgrn_norm_w[0], batch, t_len)
    b = _conv(proj, dw_w[0], dw_b[0], conv_ln_w[0], conv_ln_b[0], batch, t_len, (5 * hk) // conv_ch)
    h_mid, n2_bf, dloc, gate, cnt_tiles = _mix(a, b, x2, w_out[0], norm2_w[0],
                                                     router_w[0], router_b[0])
    nt = n_tok // ROUTE_TILE
    e_ids = jnp.arange(N_EXPERTS, dtype=I32)
    t_ids = jnp.arange(nt, dtype=I32)
    share = (cnt_tiles[:, :, 0].astype(I32) + SUBLANES - 1) // SUBLANES * SUBLANES
    in_slab = jnp.sum(jnp.where(e_ids[None, None, :] < e_ids[None, :, None], share[:, None, :], 0), axis=2)
    in_expert = jnp.sum(jnp.where((t_ids[None, :] < t_ids[:, None])[:, :, None], share[None, :, :], 0), axis=1)
    padded = (jnp.sum(share, axis=0) + MOE_BLOCK - 1) // MOE_BLOCK * MOE_BLOCK
    pad_end = jnp.sum(jnp.where(e_ids[None, :] <= e_ids[:, None], padded[None, :], 0), axis=1)
    in_xs = (pad_end - padded)[None, :] + in_expert
    n_blocks = -(-(n_tok * TOP_K + nt * N_EXPERTS * (SUBLANES - 1)) // MOE_BLOCK) + N_EXPERTS
    n_blocks += -n_blocks % FFN_STEP_BLOCKS
    block_start = jnp.arange(n_blocks, dtype=I32) * MOE_BLOCK
    block_expert = jnp.minimum(jnp.sum((pad_end[None, :] <= block_start[:, None]).astype(I32), axis=1),
                               N_EXPERTS - 1)
    n_active = pad_end[-1:] // MOE_BLOCK
    shares = [(v // SUBLANES).reshape(-1) for v in (in_slab, in_xs, share)] + [jnp.sum(share, axis=1) // SUBLANES]
    xs = _dispatch(shares, n2_bf, dloc, n_blocks * MOE_BLOCK)
    ys = _ffn(block_expert, n_active, xs, w1[0], b1[0], w2[0], b2[0])
    out = _combine(shares, h_mid, dloc, gate, final_norm_w, ys)
    return out.reshape(batch, t_len, d)
```
